```python
import jax, jax.numpy as jnp
from jax import lax
import numpy as np

D_MODEL = 1024
BATCH = 8
SEQ = 4096
DEPTH = 1

CHUNK = 64
Q_BLOCK = 128
EPS = 1e-6
HEAD_DIM = 64
A_HEADS = 6
A_WIDTH = A_HEADS * HEAD_DIM
IDX_HEADS = 8
IDX_DIM = 64
TOPK_MAX = 256
ALIBI_MAX = 8.0
B_HEADS = 6
B_NOPE = 64
B_ROPE = 32
B_V = 64
B_WIDTH = B_HEADS * B_V
Q_LORA = 256
KV_LORA = 128
ROPE_THETA = 10000.0
M_HEADS = 4
M_WIDTH = M_HEADS * HEAD_DIM
N_MEM = 256
D_MIX = A_WIDTH + B_WIDTH + M_WIDTH
IN_SPLITS = (A_WIDTH, A_WIDTH, A_WIDTH, IDX_HEADS * IDX_DIM, IDX_DIM, IDX_HEADS, A_WIDTH,
             Q_LORA, KV_LORA, B_ROPE, B_WIDTH,
             M_WIDTH, M_WIDTH)
D_IN = sum(IN_SPLITS)

kernel_name = "hybrid_dsa_mla_memory_gated_layer"


def rms_norm(x, g):
    xf = x.astype(jnp.float32)
    y = xf * lax.rsqrt(jnp.mean(xf * xf, axis=-1, keepdims=True) + EPS)
    return (y * g.astype(jnp.float32)).astype(x.dtype)


def rope(x, pos):
    half = x.shape[-1] // 2
    inv = 1.0 / (ROPE_THETA ** (jnp.arange(half, dtype=jnp.float32) / half))
    ang = pos.astype(jnp.float32)[:, None] * inv[None, :]
    cos = jnp.cos(ang)[None, :, None, :]
    sin = jnp.sin(ang)[None, :, None, :]
    xf = x.astype(jnp.float32)
    x1, x2 = xf[..., :half], xf[..., half:]
    return jnp.concatenate([x1 * cos - x2 * sin, x1 * sin + x2 * cos], axis=-1).astype(x.dtype)


def alibi_slopes(n):
    return 2.0 ** (-ALIBI_MAX * jnp.arange(1, n + 1, dtype=jnp.float32) / n)


def sweep_query_blocks(fn, seq):
    out = lax.map(fn, jnp.arange(seq // Q_BLOCK))
    nb, b, qb, h, d = out.shape
    return jnp.transpose(out, (1, 0, 2, 3, 4)).reshape(b, nb * qb, h, d)


def dsa_attention(q, k, v, q_idx, k_idx, w_idx, g_q, g_k, pos, topk):
    B, S, _ = q.shape
    q = rms_norm(q.reshape(B, S, A_HEADS, HEAD_DIM), g_q)
    k = rms_norm(k.reshape(B, S, A_HEADS, HEAD_DIM), g_k)
    v = v.reshape(B, S, A_HEADS, HEAD_DIM)
    q_idx = q_idx.reshape(B, S, IDX_HEADS, IDX_DIM)
    slopes = alibi_slopes(A_HEADS)
    chunk = pos // CHUNK
    scale = HEAD_DIM ** -0.5

    def block(i):
        start = i * Q_BLOCK
        qpos = lax.dynamic_slice_in_dim(pos, start, Q_BLOCK)
        qchunk = qpos // CHUNK
        qi_blk = lax.dynamic_slice_in_dim(q_idx, start, Q_BLOCK, axis=1)
        w_blk = lax.dynamic_slice_in_dim(w_idx, start, Q_BLOCK, axis=1)
        q_blk = lax.dynamic_slice_in_dim(q, start, Q_BLOCK, axis=1)
        rel = jnp.einsum('bqhd,bkd->bqhk', qi_blk, k_idx).astype(jnp.float32)
        score = jnp.einsum('bqh,bqhk->bqk', w_blk.astype(jnp.float32), jax.nn.relu(rel))
        admissible = chunk[None, :] <= qchunk[:, None]
        score = jnp.where(admissible[None], score, -jnp.inf)
        _, idx = lax.top_k(score, topk)
        valid = (idx // CHUNK) <= qchunk[None, :, None]
        k_sel = jax.vmap(lambda kb, ib: kb[ib])(k, idx)
        v_sel = jax.vmap(lambda vb, ib: vb[ib])(v, idx)
        s = jnp.einsum('bqhd,bqkhd->bhqk', q_blk, k_sel).astype(jnp.float32) * scale
        dist = jnp.abs(qpos[None, :, None] - idx).astype(jnp.float32)
        s = s - slopes[None, :, None, None] * dist[:, None]
        s = jnp.where(valid[:, None], s, -jnp.inf)
        p = jax.nn.softmax(s, axis=-1).astype(v.dtype)
        return jnp.einsum('bhqk,bqkhd->bqhd', p, v_sel)

    return sweep_query_blocks(block, S)


def mla_attention(c_q, c_kv, k_rope, w_uq, g_cq, w_ukv, g_ckv, g_q, g_k, pos):
    B, S, _ = c_q.shape
    q = (rms_norm(c_q, g_cq) @ w_uq).reshape(B, S, B_HEADS, B_NOPE + B_ROPE)
    kv = (rms_norm(c_kv, g_ckv) @ w_ukv).reshape(B, S, B_HEADS, B_NOPE + B_V)
    k_nope, v = kv[..., :B_NOPE], kv[..., B_NOPE:]
    k = jnp.concatenate(
        [k_nope, jnp.broadcast_to(k_rope[:, :, None, :], (B, S, B_HEADS, B_ROPE))], axis=-1)
    q = rms_norm(q, g_q)
    k = rms_norm(k, g_k)
    q = jnp.concatenate([q[..., :B_NOPE], rope(q[..., B_NOPE:], pos)], axis=-1)
    k = jnp.concatenate([k[..., :B_NOPE], rope(k[..., B_NOPE:], pos)], axis=-1)
    scale = (B_NOPE + B_ROPE) ** -0.5
    chunk = pos // CHUNK

    def block(i):
        start = i * Q_BLOCK
        q_blk = lax.dynamic_slice_in_dim(q, start, Q_BLOCK, axis=1)
        qchunk = lax.dynamic_slice_in_dim(chunk, start, Q_BLOCK)
        s = jnp.einsum('bqhd,bkhd->bhqk', q_blk, k).astype(jnp.float32) * scale
        s = jnp.where(chunk[None, :] <= qchunk[:, None], s, -jnp.inf)
        p = jax.nn.softmax(s, axis=-1).astype(v.dtype)
        return jnp.einsum('bhqk,bkhd->bqhd', p, v)

    return sweep_query_blocks(block, S)


def memory_attention(q, mem, g_mem, w_mk, w_mv, g_q, g_k):
    B, S, _ = q.shape
    N = mem.shape[1]
    q = rms_norm(q.reshape(B, S, M_HEADS, HEAD_DIM), g_q)
    m = rms_norm(mem, g_mem)
    k = rms_norm((m @ w_mk).reshape(B, N, M_HEADS, HEAD_DIM), g_k)
    v = (m @ w_mv).reshape(B, N, M_HEADS, HEAD_DIM)
    s = jnp.einsum('bqhd,bkhd->bhqk', q, k).astype(jnp.float32) * (HEAD_DIM ** -0.5)
    p = jax.nn.softmax(s, axis=-1).astype(v.dtype)
    return jnp.einsum('bhqk,bkhd->bqhd', p, v)


def setup_inputs(seed: int = 0) -> dict:
    key = jax.random.key(seed)
    ks = jax.random.split(key, 20)
    f32 = jnp.float32

    def w(k, shape, fan_in):
        return jax.random.normal(k, shape, f32) * (fan_in ** -0.5)

    def gain(k, shape):
        return 1.0 + 0.1 * jax.random.normal(k, shape, f32)

    return {
        "x": jax.random.normal(ks[0], (BATCH, SEQ, D_MODEL), f32),
        "mem": jax.random.normal(ks[1], (BATCH, N_MEM, D_MODEL), f32),
        "g_in": gain(ks[2], (DEPTH, D_MODEL)),
        "w_in": w(ks[3], (DEPTH, D_MODEL, D_IN), D_MODEL),
        "w_uq": w(ks[4], (DEPTH, Q_LORA, B_HEADS * (B_NOPE + B_ROPE)), Q_LORA),
        "g_cq": gain(ks[5], (DEPTH, Q_LORA)),
        "w_ukv": w(ks[6], (DEPTH, KV_LORA, B_HEADS * (B_NOPE + B_V)), KV_LORA),
        "g_ckv": gain(ks[7], (DEPTH, KV_LORA)),
        "g_qa": gain(ks[8], (DEPTH, HEAD_DIM)),
        "g_ka": gain(ks[9], (DEPTH, HEAD_DIM)),
        "g_qb": gain(ks[10], (DEPTH, B_NOPE + B_ROPE)),
        "g_kb": gain(ks[11], (DEPTH, B_NOPE + B_ROPE)),
        "g_mem": gain(ks[12], (DEPTH, D_MODEL)),
        "w_mk": w(ks[13], (DEPTH, D_MODEL, M_WIDTH), D_MODEL),
        "w_mv": w(ks[14], (DEPTH, D_MODEL, M_WIDTH), D_MODEL),
        "g_qm": gain(ks[15], (DEPTH, HEAD_DIM)),
        "g_km": gain(ks[16], (DEPTH, HEAD_DIM)),
        "w_out": w(ks[17], (DEPTH, D_MIX, D_MODEL), D_MIX),
    }


def reference(x, mem, g_in, w_in, w_uq, g_cq, w_ukv, g_ckv, g_qa, g_ka, g_qb, g_kb,
              g_mem, w_mk, w_mv, g_qm, g_km, w_out):
    B, S, _ = x.shape
    pos = jnp.arange(S, dtype=jnp.int32)
    topk = min(TOPK_MAX, S // 4)
    offsets = [int(o) for o in np.cumsum(IN_SPLITS)[:-1]]
    for l in range(DEPTH):
        h = rms_norm(x, g_in[l])
        u = h @ w_in[l]
        (q_a, k_a, v_a, q_idx, k_idx, w_idx, z_a,
         c_q, c_kv, k_rope, z_b, q_m, z_m) = jnp.split(u, offsets, axis=-1)
        y_a = dsa_attention(q_a, k_a, v_a, q_idx, k_idx, w_idx, g_qa[l], g_ka[l], pos, topk)
        y_b = mla_attention(c_q, c_kv, k_rope, w_uq[l], g_cq[l], w_ukv[l], g_ckv[l],
                            g_qb[l], g_kb[l], pos)
        y_m = memory_attention(q_m, mem, g_mem[l], w_mk[l], w_mv[l], g_qm[l], g_km[l])
        y = jnp.concatenate([
            y_a.reshape(B, S, A_WIDTH) * jax.nn.silu(z_a),
            y_b.reshape(B, S, B_WIDTH) * jax.nn.silu(z_b),
            y_m.reshape(B, S, M_WIDTH) * jax.nn.silu(z_m),
        ], axis=-1)
        x = x + y @ w_out[l]
    return x
```

```python
import functools

import jax
import jax.numpy as jnp
from jax import lax
from jax.experimental import pallas as pl
from jax.experimental.pallas import tpu as pltpu

F32 = jnp.float32
BF16 = jnp.bfloat16

CHUNK = 64
EPS = 1e-6
HEAD_DIM = 64
A_HEADS = 6
IDX_HEADS = 8
IDX_DIM = 64
TOPK_MAX = 256
ALIBI_MAX = 8.0
B_HEADS = 6
B_NOPE = 64
B_ROPE = 32
B_V = 64
B_QK = B_NOPE + B_ROPE
Q_LORA = 256
KV_LORA = 128
ROPE_THETA = 10000.0
M_HEADS = 4
A_WIDTH = A_HEADS * HEAD_DIM
B_WIDTH = B_HEADS * B_V
M_WIDTH = M_HEADS * HEAD_DIM

LANES = 128
VMEM_LIMIT = 52 * 1024 * 1024

TM = 512
TQ = 256
TK = 256
RC = 64
N_BISECT = 24
NEG = -1e30

_C_QA = 0
_C_KA = _C_QA + A_WIDTH
_C_VA = _C_KA + A_WIDTH
_C_QI = _C_VA + A_WIDTH
_C_KI = _C_QI + IDX_HEADS * IDX_DIM
_C_WI = _C_KI + LANES
_C_GATE = _C_WI + LANES
_C_CQ = _C_GATE + A_WIDTH + B_WIDTH + M_WIDTH
_C_CKV = _C_CQ + Q_LORA
_C_KR = _C_CKV + KV_LORA
_C_QM = _C_KR + LANES
_C_END = _C_QM + M_WIDTH


def _nt_dot(a, b):
    return lax.dot_general(a, b, (((1,), (1,)), ((), ())), preferred_element_type=F32)


def _rope(xh, c, s1, s2):
    nxt = pltpu.roll(xh, LANES - B_ROPE // 2, 1)
    prv = pltpu.roll(xh, B_ROPE // 2, 1)
    return xh * c + nxt * s1 + prv * s2


def _inproj_kernel(x_ref, gin_ref, w_ref, gmat_ref, gqa_ref, gka_ref, gcq_ref, wuq_ref,
                   gqb_ref, gckv_ref, wukvk_ref, wukvv_ref, gkb_ref, gqm_ref,
                   rc_ref, rs1_ref, rs2_ref,
                   qa_ref, ka_ref, va_ref, qi_ref, ki_ref, wi_ref, gate_ref,
                   qb_ref, kb_ref, vb_ref, qm_ref):
    x = x_ref[...]
    ms = jnp.mean(x * x, axis=-1, keepdims=True)
    h = (x * lax.rsqrt(ms + EPS) * gin_ref[...]).astype(BF16)

    def proj(a, b):
        return jnp.dot(h, w_ref[:, a:b], preferred_element_type=F32)

    def head64_norm(u, g):
        n = u.shape[1]
        msq = jnp.dot((u * u).astype(BF16), gmat_ref[:n, :n], preferred_element_type=F32)
        return u * lax.rsqrt(msq + EPS) * g

    qa_ref[...] = head64_norm(proj(_C_QA, _C_KA), gqa_ref[...]).astype(BF16)
    ka = head64_norm(proj(_C_KA, _C_VA), gka_ref[...]).astype(BF16)
    va = proj(_C_VA, _C_QI).astype(BF16)
    for g in range(A_WIDTH // LANES):
        ka_ref[g] = ka[:, g * LANES:(g + 1) * LANES]
        va_ref[g] = va[:, g * LANES:(g + 1) * LANES]

    qi_ref[...] = proj(_C_QI, _C_KI).astype(BF16)
    ki_ref[...] = proj(_C_KI, _C_WI).astype(BF16)
    wi_ref[...] = proj(_C_WI, _C_GATE)[:, :IDX_HEADS]

    z = proj(_C_GATE, _C_CQ)
    gate_ref[...] = (z / (1.0 + jnp.exp(-z))).astype(BF16)

    rc = rc_ref[...]
    rs1 = rs1_ref[...]
    rs2 = rs2_ref[...]

    def head96(uh, g):
        ss = jnp.sum(uh * uh, axis=-1, keepdims=True) * (1.0 / B_QK)
        return _rope(uh * lax.rsqrt(ss + EPS) * g, rc, rs1, rs2).astype(BF16)

    cq = proj(_C_CQ, _C_CKV)
    cq = cq * lax.rsqrt(jnp.mean(cq * cq, axis=-1, keepdims=True) + EPS) * gcq_ref[...]
    q = jnp.dot(cq.astype(BF16), wuq_ref[...], preferred_element_type=F32)
    gqb = gqb_ref[...]
    for hd in range(B_HEADS):
        qb_ref[hd] = head96(q[:, hd * LANES:(hd + 1) * LANES], gqb)

    ckv = proj(_C_CKV, _C_KR)
    ckv = ckv * lax.rsqrt(jnp.mean(ckv * ckv, axis=-1, keepdims=True) + EPS) * gckv_ref[...]
    ckv = ckv.astype(BF16)
    kk = jnp.dot(ckv, wukvk_ref[...], preferred_element_type=F32)
    vv = jnp.dot(ckv, wukvv_ref[...], preferred_element_type=F32).astype(BF16)
    krope = proj(_C_KR, _C_QM)
    gkb = gkb_ref[...]
    for hd in range(B_HEADS):
        kb_ref[hd] = head96(kk[:, hd * LANES:(hd + 1) * LANES] + krope, gkb)
    for g in range(B_WIDTH // LANES):
        vb_ref[g] = vv[:, g * LANES:(g + 1) * LANES]

    qm_ref[...] = head64_norm(proj(_C_QM, _C_END), gqm_ref[...]).astype(BF16)


def _memkv_kernel(mem_ref, gmem_ref, wmk_ref, wmv_ref, gmat_ref, gkm_ref, km_ref, vm_ref):
    x = mem_ref[...]
    ms = jnp.mean(x * x, axis=-1, keepdims=True)
    m = (x * lax.rsqrt(ms + EPS) * gmem_ref[...]).astype(BF16)
    k = jnp.dot(m, wmk_ref[...], preferred_element_type=F32)
    msq = jnp.dot((k * k).astype(BF16), gmat_ref[:M_WIDTH, :M_WIDTH], preferred_element_type=F32)
    k = (k * lax.rsqrt(msq + EPS) * gkm_ref[...]).astype(BF16)
    v = jnp.dot(m, wmv_ref[...], preferred_element_type=F32).astype(BF16)
    for g in range(M_WIDTH // LANES):
        km_ref[g] = k[:, g * LANES:(g + 1) * LANES]
        vm_ref[g] = v[:, g * LANES:(g + 1) * LANES]


def _flash_init(m_ref, l_ref, acc_ref):
    m_ref[...] = jnp.full(m_ref.shape, NEG, F32)
    l_ref[...] = jnp.zeros(l_ref.shape, F32)
    acc_ref[...] = jnp.zeros(acc_ref.shape, F32)


def _flash_step(hd, s, vblk, m_ref, l_ref, acc_ref):
    rep = s.shape[1] // LANES
    m_prev = m_ref[hd]
    m_new = jnp.maximum(m_prev, jnp.max(s, axis=1, keepdims=True))
    alpha = jnp.exp(m_prev - m_new)
    p = jnp.exp(s - jnp.concatenate([m_new] * rep, axis=1))
    l_ref[hd] = alpha * l_ref[hd] + jnp.sum(p, axis=1, keepdims=True)
    acc_ref[hd] = alpha * acc_ref[hd] + jnp.dot(p.astype(BF16), vblk, preferred_element_type=F32)
    m_ref[hd] = m_new


def _merge_head_pairs(out_ref, l_ref, acc_ref, n_pairs):
    low = lax.broadcasted_iota(jnp.int32, (out_ref.shape[0], LANES), 1) < HEAD_DIM
    for g in range(n_pairs):
        o0 = acc_ref[2 * g] / l_ref[2 * g]
        o1 = acc_ref[2 * g + 1] / l_ref[2 * g + 1]
        out_ref[:, g * LANES:(g + 1) * LANES] = jnp.where(low, o0, o1).astype(out_ref.dtype)


def _split_head_pairs(src_ref, dst_ref, n_pairs):
    low = lax.broadcasted_iota(jnp.int32, (src_ref.shape[0], LANES), 1) < HEAD_DIM
    for g in range(n_pairs):
        pair = src_ref[:, g * LANES:(g + 1) * LANES].astype(F32)
        dst_ref[2 * g] = jnp.where(low, pair, 0.0).astype(dst_ref.dtype)
        dst_ref[2 * g + 1] = jnp.where(low, 0.0, pair).astype(dst_ref.dtype)


def _dsa_kernel(slopes_ref, qi_ref, wi_ref, ki_ref, qa_ref, ka_ref, va_ref, utri_ref,
                out_ref,
                score_ref, qim_ref, wb_ref, qam_ref, m_ref, l_ref, acc_ref, *, topk):
    i = pl.program_id(1)
    nkb = i + 1
    q0 = i * TQ

    _split_head_pairs(qi_ref, qim_ref, IDX_HEADS // 2)
    _split_head_pairs(qa_ref, qam_ref, A_HEADS // 2)
    w = wi_ref[...]
    for hd in range(IDX_HEADS):
        wb_ref[hd] = jnp.broadcast_to(w[:, hd:hd + 1], (TQ, LANES))

    row = lax.broadcasted_iota(jnp.int32, (TQ, TK), 0)
    col = lax.broadcasted_iota(jnp.int32, (TQ, TK), 1)

    def score_block(kb, carry):
        k0 = pl.multiple_of(kb * TK, TK)
        kblk = ki_ref[pl.ds(k0, TK), :]
        acc = jnp.zeros((TQ, TK), F32)
        for hd in range(IDX_HEADS):
            rel = _nt_dot(qim_ref[hd], kblk)
            wv = wb_ref[hd]
            acc = acc + jnp.maximum(rel, 0.0) * jnp.concatenate([wv] * (TK // LANES), axis=1)
        adm = ((k0 + col) >> 6) <= ((q0 + row) >> 6)
        score_ref[kb] = jnp.where(adm, acc, -jnp.inf)
        return carry

    lax.fori_loop(0, nkb, score_block, 0)

    def search_rows(rc, carry):
        r0 = pl.multiple_of(rc * RC, RC)
        rows = pl.ds(r0, RC)
        qpos = q0 + r0 + lax.broadcasted_iota(jnp.int32, (RC, 1), 0)
        n_adm = ((qpos >> 6) + 1) << 6
        k_eff = jnp.minimum(n_adm, topk).astype(F32)

        def pieces(kb):
            s = score_ref[kb, rows, :]
            return [s[:, j * LANES:(j + 1) * LANES] for j in range(TK // LANES)]

        def lanes(t):
            return jnp.broadcast_to(t, (RC, LANES))

        def stats(kb, c):
            mn, mx = c
            for s in pieces(kb):
                mn = jnp.minimum(mn, jnp.where(s == -jnp.inf, jnp.inf, s))
                mx = jnp.maximum(mx, s)
            return mn, mx

        mn, mx = lax.fori_loop(0, nkb, stats, (jnp.full((RC, LANES), jnp.inf, F32),
                                               jnp.full((RC, LANES), -jnp.inf, F32)))
        lo = jnp.min(mn, axis=1, keepdims=True)
        mx = jnp.max(mx, axis=1, keepdims=True)
        hi = mx + jnp.abs(mx) + 1.0

        def count_ge(t):
            tb = lanes(t)

            def body(kb, acc):
                for s in pieces(kb):
                    acc = acc + jnp.where(s >= tb, 1.0, 0.0)
                return acc

            acc = lax.fori_loop(0, nkb, body, jnp.zeros((RC, LANES), F32))
            return jnp.sum(acc, axis=1, keepdims=True)

        def bisect(_, c):
            lo, hi = c
            mid = 0.5 * (lo + hi)
            ge = count_ge(mid) >= k_eff
            return jnp.where(ge, mid, lo), jnp.where(ge, hi, mid)

        lo, hi = lax.fori_loop(0, N_BISECT, bisect, (lo, hi))

        def refine_cond(c):
            return c[3] > 0.0

        def refine(c):
            hi, thr, c_gt, _ = c
            hb = lanes(hi)

            def below(kb, acc):
                for s in pieces(kb):
                    acc = jnp.maximum(acc, jnp.where(s < hb, s, -jnp.inf))
                return acc

            m = jnp.max(lax.fori_loop(0, nkb, below, jnp.full((RC, LANES), -jnp.inf, F32)),
                        axis=1, keepdims=True)
            mb = lanes(m)

            def counts(kb, c2):
                ge, gt = c2
                for s in pieces(kb):
                    ge = ge + jnp.where(s >= mb, 1.0, 0.0)
                    gt = gt + jnp.where(s > mb, 1.0, 0.0)
                return ge, gt

            z = jnp.zeros((RC, LANES), F32)
            ge, gt = lax.fori_loop(0, nkb, counts, (z, z))
            ge = jnp.sum(ge, axis=1, keepdims=True)
            gt = jnp.sum(gt, axis=1, keepdims=True)
            ok = ge >= k_eff
            pending = jnp.sum(jnp.where(ok, 0.0, 1.0))
            return jnp.where(ok, hi, m), jnp.where(ok, m, thr), jnp.where(ok, gt, c_gt), pending

        zc = jnp.zeros((RC, 1), F32)
        _, thr, c_gt, _ = lax.while_loop(refine_cond, refine, (hi, zc, zc, jnp.float32(1.0)))

        thb = jnp.broadcast_to(thr, (RC, TK))
        need = jnp.broadcast_to(k_eff - c_gt, (RC, TK))

        def bias_block(kb, seen):
            s = score_ref[kb, rows, :]
            eq = s == thb
            eqf = jnp.where(eq, 1.0, 0.0)
            rank = seen + jnp.dot(eqf.astype(BF16), utri_ref[...], preferred_element_type=F32)
            tie = jnp.where(rank <= need, 0.0, NEG)
            score_ref[kb, rows, :] = jnp.where(s > thb, 0.0, jnp.where(eq, tie, NEG))
            return seen + jnp.sum(eqf, axis=1, keepdims=True)

        lax.fori_loop(0, nkb, bias_block, jnp.zeros((RC, 1), F32))
        return carry

    lax.fori_loop(0, TQ // RC, search_rows, 0)

    _flash_init(m_ref, l_ref, acc_ref)

    def attend_block(kb, carry):
        k0 = pl.multiple_of(kb * TK, TK)
        dist = jnp.abs((q0 + row) - (k0 + col)).astype(F32)
        bias = score_ref[kb]
        for hd in range(A_HEADS):
            g = hd // 2
            s = _nt_dot(qam_ref[hd], ka_ref[g, pl.ds(k0, TK), :]) + (bias - slopes_ref[hd] * dist)
            _flash_step(hd, s, va_ref[g, pl.ds(k0, TK), :], m_ref, l_ref, acc_ref)
        return carry

    lax.fori_loop(0, nkb, attend_block, 0)
    _merge_head_pairs(out_ref, l_ref, acc_ref, A_HEADS // 2)


def _mla_kernel(qb_ref, kb_ref, vb_ref, out_ref, m_ref, l_ref, acc_ref):
    i = pl.program_id(1)
    q0 = i * TQ
    row = lax.broadcasted_iota(jnp.int32, (TQ, TK), 0)
    col = lax.broadcasted_iota(jnp.int32, (TQ, TK), 1)
    _flash_init(m_ref, l_ref, acc_ref)

    def attend_block(kb, carry):
        k0 = pl.multiple_of(kb * TK, TK)
        adm = ((k0 + col) >> 6) <= ((q0 + row) >> 6)
        bias = jnp.where(adm, 0.0, NEG)
        for hd in range(B_HEADS):
            s = _nt_dot(qb_ref[hd], kb_ref[hd, pl.ds(k0, TK), :]) + bias
            _flash_step(hd, s, vb_ref[hd // 2, pl.ds(k0, TK), :], m_ref, l_ref, acc_ref)
        return carry

    lax.fori_loop(0, i + 1, attend_block, 0)
    _merge_head_pairs(out_ref, l_ref, acc_ref, B_HEADS // 2)


def _out_kernel(x_ref, ya_ref, yb_ref, gate_ref, qm_ref, km_ref, vm_ref, wout_ref, out_ref):
    tm = x_ref.shape[0]
    low = lax.broadcasted_iota(jnp.int32, (tm, LANES), 1) < HEAD_DIM
    ym = []
    for g in range(M_HEADS // 2):
        pair = qm_ref[:, g * LANES:(g + 1) * LANES].astype(F32)
        outs = []
        for q in (jnp.where(low, pair, 0.0), jnp.where(low, 0.0, pair)):
            s = _nt_dot(q.astype(BF16), km_ref[g])
            p = jnp.exp(s - jnp.max(s, axis=1, keepdims=True))
            o = jnp.dot(p.astype(BF16), vm_ref[g], preferred_element_type=F32)
            outs.append(o / jnp.sum(p, axis=1, keepdims=True))
        ym.append(jnp.where(low, outs[0], outs[1]))
    y = jnp.concatenate([ya_ref[...].astype(F32), yb_ref[...].astype(F32)] + ym, axis=1)
    y = (y * gate_ref[...].astype(F32)).astype(BF16)
    out_ref[...] = x_ref[...] + jnp.dot(y, wout_ref[...], preferred_element_type=F32)


def _full(shape):
    n = len(shape)
    return pl.BlockSpec(shape, lambda *_: (0,) * n)


def _params(*sem):
    return pltpu.CompilerParams(dimension_semantics=sem, vmem_limit_bytes=VMEM_LIMIT)


def kernel(x, mem, g_in, w_in, w_uq, g_cq, w_ukv, g_ckv, g_qa, g_ka, g_qb, g_kb,
           g_mem, w_mk, w_mv, g_qm, g_km, w_out):
    B, S, D = x.shape
    n_mem = mem.shape[1]
    assert S % TM == 0 and S % TQ == 0 and TQ == TK and TQ % RC == 0
    topk = min(TOPK_MAX, S // 4)
    l = 0

    wi = w_in[l]
    o = [0]
    for n in (A_WIDTH, A_WIDTH, A_WIDTH, IDX_HEADS * IDX_DIM, IDX_DIM, IDX_HEADS, A_WIDTH,
              Q_LORA, KV_LORA, B_ROPE, B_WIDTH, M_WIDTH, M_WIDTH):
        o.append(o[-1] + n)
    (c_qa, c_ka, c_va, c_qi, c_ki, c_wi, c_za, c_cq, c_ckv, c_kr, c_zb, c_qm, c_zm) = [
        wi[:, o[j]:o[j + 1]] for j in range(13)]
    zeros = lambda n: jnp.zeros((D, n), wi.dtype)
    w_pack = jnp.concatenate([
        c_qa, c_ka, c_va, c_qi,
        c_ki, c_ki,
        c_wi, zeros(LANES - IDX_HEADS),
        c_za, c_zb, c_zm,
        c_cq, c_ckv,
        zeros(B_NOPE), c_kr, zeros(LANES - B_QK),
        c_qm], axis=1).astype(BF16)
    assert w_pack.shape[1] == _C_END

    pad_heads = lambda w, n_in, per, keep: jnp.pad(
        w.reshape(n_in, B_HEADS, per)[:, :, :keep], ((0, 0), (0, 0), (0, LANES - keep))
    ).reshape(n_in, B_HEADS * LANES)
    wuq_p = pad_heads(w_uq[l], Q_LORA, B_QK, B_QK).astype(BF16)
    wukv = w_ukv[l].reshape(KV_LORA, B_HEADS, B_NOPE + B_V)
    wukv_k = jnp.pad(wukv[:, :, :B_NOPE], ((0, 0), (0, 0), (0, LANES - B_NOPE))
                     ).reshape(KV_LORA, B_HEADS * LANES).astype(BF16)
    wukv_v = wukv[:, :, B_NOPE:].reshape(KV_LORA, B_WIDTH).astype(BF16)

    idx = jnp.arange(A_WIDTH)
    gmat = jnp.where((idx[:, None] // HEAD_DIM) == (idx[None, :] // HEAD_DIM),
                     1.0 / HEAD_DIM, 0.0).astype(BF16)
    row2 = lambda v: v.reshape(1, -1).astype(F32)
    gqa_t = row2(jnp.tile(g_qa[l], A_HEADS) * HEAD_DIM ** -0.5)
    gka_t = row2(jnp.tile(g_ka[l], A_HEADS))
    gqm_t = row2(jnp.tile(g_qm[l], M_HEADS) * HEAD_DIM ** -0.5)
    gkm_t = row2(jnp.tile(g_km[l], M_HEADS))
    gqb_p = row2(jnp.pad(g_qb[l], (0, LANES - B_QK)) * B_QK ** -0.5)
    gkb_p = row2(jnp.pad(g_kb[l], (0, LANES - B_QK)))

    half = B_ROPE // 2
    pos = jnp.arange(S, dtype=jnp.int32)
    inv = 1.0 / (ROPE_THETA ** (jnp.arange(half, dtype=F32) / half))
    ang = pos.astype(F32)[:, None] * inv[None, :]
    cos, sin = jnp.cos(ang), jnp.sin(ang)
    zs = lambda n: jnp.zeros((S, n), F32)
    rope_c = jnp.concatenate([jnp.ones((S, B_NOPE), F32), cos, cos, jnp.ones((S, LANES - B_QK), F32)], 1)
    rope_s1 = jnp.concatenate([zs(B_NOPE), -sin, zs(half), zs(LANES - B_QK)], 1)
    rope_s2 = jnp.concatenate([zs(B_NOPE), zs(half), sin, zs(LANES - B_QK)], 1)

    slopes = 2.0 ** (-ALIBI_MAX * jnp.arange(1, A_HEADS + 1, dtype=F32) / A_HEADS)
    kk = jnp.arange(TK)
    utri = (kk[:, None] <= kk[None, :]).astype(BF16)

    tok = lambda w: pl.BlockSpec((None, TM, w), lambda b, t: (b, t, 0))
    grp = lambda n: pl.BlockSpec((None, n, TM, LANES), lambda b, t: (b, 0, t, 0))
    tab = pl.BlockSpec((TM, LANES), lambda b, t: (t, 0))
    sds = jax.ShapeDtypeStruct
    (qa, ka, va, qi, ki, wi_tok, gate, qb, kb, vb, qm) = pl.pallas_call(
        _inproj_kernel,
        grid=(B, S // TM),
        in_specs=[tok(D), _full((1, D)), _full((D, _C_END)), _full((A_WIDTH, A_WIDTH)),
                  _full((1, A_WIDTH)), _full((1, A_WIDTH)), _full((1, Q_LORA)),
                  _full((Q_LORA, B_HEADS * LANES)), _full((1, LANES)), _full((1, KV_LORA)),
                  _full((KV_LORA, B_HEADS * LANES)), _full((KV_LORA, B_WIDTH)),
                  _full((1, LANES)), _full((1, M_WIDTH)), tab, tab, tab],
        out_specs=[tok(A_WIDTH), grp(3), grp(3), tok(IDX_HEADS * IDX_DIM), tok(LANES),
                   tok(IDX_HEADS), tok(D), grp(B_HEADS), grp(B_HEADS), grp(3), tok(M_WIDTH)],
        out_shape=[sds((B, S, A_WIDTH), BF16), sds((B, 3, S, LANES), BF16),
                   sds((B, 3, S, LANES), BF16), sds((B, S, IDX_HEADS * IDX_DIM), BF16),
                   sds((B, S, LANES), BF16), sds((B, S, IDX_HEADS), F32), sds((B, S, D), BF16),
                   sds((B, B_HEADS, S, LANES), BF16), sds((B, B_HEADS, S, LANES), BF16),
                   sds((B, 3, S, LANES), BF16), sds((B, S, M_WIDTH), BF16)],
        compiler_params=_params("parallel", "parallel"),
        name="inproj",
    )(x, row2(g_in[l]), w_pack, gmat, gqa_t, gka_t, row2(g_cq[l]), wuq_p, gqb_p,
      row2(g_ckv[l]), wukv_k, wukv_v, gkb_p, gqm_t, rope_c, rope_s1, rope_s2)

    km, vm = pl.pallas_call(
        _memkv_kernel,
        grid=(B,),
        in_specs=[pl.BlockSpec((None, n_mem, D), lambda b: (b, 0, 0)), _full((1, D)),
                  _full((D, M_WIDTH)), _full((D, M_WIDTH)), _full((A_WIDTH, A_WIDTH)),
                  _full((1, M_WIDTH))],
        out_specs=[pl.BlockSpec((None, 2, n_mem, LANES), lambda b: (b, 0, 0, 0))] * 2,
        out_shape=[sds((B, 2, n_mem, LANES), BF16)] * 2,
        compiler_params=_params("parallel"),
        name="memkv",
    )(mem, row2(g_mem[l]), w_mk[l].astype(BF16), w_mv[l].astype(BF16), gmat, gkm_t)

    qtile = lambda w: pl.BlockSpec((None, TQ, w), lambda b, t: (b, t, 0))
    seq = lambda n: pl.BlockSpec((None, n, S, LANES), lambda b, t: (b, 0, 0, 0))
    heads = lambda n: pl.BlockSpec((None, n, TQ, LANES), lambda b, t: (b, 0, t, 0))
    hscr = lambda n, dt: pltpu.VMEM((n, TQ, LANES), dt)
    ya = pl.pallas_call(
        functools.partial(_dsa_kernel, topk=topk),
        grid=(B, S // TQ),
        in_specs=[pl.BlockSpec(memory_space=pltpu.SMEM), qtile(IDX_HEADS * IDX_DIM),
                  qtile(IDX_HEADS), pl.BlockSpec((None, S, LANES), lambda b, t: (b, 0, 0)),
                  qtile(A_WIDTH), seq(3), seq(3), _full((TK, TK))],
        out_specs=qtile(A_WIDTH),
        out_shape=sds((B, S, A_WIDTH), BF16),
        scratch_shapes=[pltpu.VMEM((S // TK, TQ, TK), F32), hscr(IDX_HEADS, BF16),
                        hscr(IDX_HEADS, F32), hscr(A_HEADS, BF16),
                        hscr(A_HEADS, F32), hscr(A_HEADS, F32), hscr(A_HEADS, F32)],
        compiler_params=_params("parallel", "arbitrary"),
        name="dsa",
    )(slopes, qi, wi_tok, ki, qa, ka, va, utri)

    yb = pl.pallas_call(
        _mla_kernel,
        grid=(B, S // TQ),
        in_specs=[heads(B_HEADS), seq(B_HEADS), seq(3)],
        out_specs=qtile(B_WIDTH),
        out_shape=sds((B, S, B_WIDTH), BF16),
        scratch_shapes=[hscr(B_HEADS, F32)] * 3,
        compiler_params=_params("parallel", "arbitrary"),
        name="mla",
    )(qb, kb, vb)

    memkv = pl.BlockSpec((None, 2, n_mem, LANES), lambda b, t: (b, 0, 0, 0))
    return pl.pallas_call(
        _out_kernel,
        grid=(B, S // TM),
        in_specs=[tok(D), tok(A_WIDTH), tok(B_WIDTH), tok(D), tok(M_WIDTH), memkv, memkv,
                  _full((D, D))],
        out_specs=tok(D),
        out_shape=sds((B, S, D), x.dtype),
        compiler_params=_params("parallel", "parallel"),
        name="outproj",
    )(x, ya, yb, gate, qm, km, vm, w_out[l].astype(BF16))
```

```python
import functools
import math

import jax
import jax.numpy as jnp
from jax import lax
from jax.experimental import pallas as pl
from jax.experimental.pallas import tpu as pltpu

F32 = jnp.float32
BF16 = jnp.bfloat16

CHUNK = 64
EPS = 1e-6
HEAD_DIM = 64
A_HEADS = 6
IDX_HEADS = 8
IDX_DIM = 64
TOPK_MAX = 256
ALIBI_MAX = 8.0
B_HEADS = 6
B_NOPE = 64
B_ROPE = 32
B_V = 64
B_QK = B_NOPE + B_ROPE
Q_LORA = 256
KV_LORA = 128
ROPE_THETA = 10000.0
M_HEADS = 4
A_WIDTH = A_HEADS * HEAD_DIM
B_WIDTH = B_HEADS * B_V
M_WIDTH = M_HEADS * HEAD_DIM
LOG2E = math.log2(math.e)

LANES = 128
SUBLANES = 8
VMEM_LIMIT = 52 * 1024 * 1024

TM = 512
TQ = 256
TK = 256
N_BISECT = 22
NEG = -1e30

_C_QA = 0
_C_KA = _C_QA + A_WIDTH
_C_VA = _C_KA + A_WIDTH
_C_QI = _C_VA + A_WIDTH
_C_KI = _C_QI + IDX_HEADS * IDX_DIM
_C_WI = _C_KI + LANES
_C_GATE = _C_WI + LANES
_C_CQ = _C_GATE + A_WIDTH + B_WIDTH + M_WIDTH
_C_CKV = _C_CQ + Q_LORA
_C_KR = _C_CKV + KV_LORA
_C_QM = _C_KR + LANES
_C_END = _C_QM + M_WIDTH


def _nt_dot(a, b):
    return lax.dot_general(a, b, (((1,), (1,)), ((), ())), preferred_element_type=F32)


def _rope(xh, c, s1, s2):
    nxt = pltpu.roll(xh, LANES - B_ROPE // 2, 1)
    prv = pltpu.roll(xh, B_ROPE // 2, 1)
    return xh * c + nxt * s1 + prv * s2


def _inproj_kernel(x_ref, gin_ref, w_ref, gmat_ref, gqa_ref, gka_ref, gcq_ref, wuq_ref,
                   gqb_ref, gckv_ref, wukvk_ref, wukvv_ref, gkb_ref, gqm_ref,
                   rc_ref, rs1_ref, rs2_ref,
                   qa_ref, ka_ref, vat_ref, qi_ref, ki_ref, wit_ref, gate_ref,
                   qb_ref, kb_ref, vbt_ref, qm_ref):
    x = x_ref[...]
    ms = jnp.mean(x * x, axis=-1, keepdims=True)
    h = (x * lax.rsqrt(ms + EPS) * gin_ref[...]).astype(BF16)

    def proj(a, b):
        return jnp.dot(h, w_ref[:, a:b], preferred_element_type=F32)

    def head64_norm(u, g):
        n = u.shape[1]
        msq = jnp.dot((u * u).astype(BF16), gmat_ref[:n, :n], preferred_element_type=F32)
        return u * lax.rsqrt(msq + EPS) * g

    def store_transposed(dst_ref, v):
        vt = v.T.astype(BF16)
        for j in range(TM // TK):
            dst_ref[j] = vt[:, j * TK:(j + 1) * TK]

    qa_ref[...] = head64_norm(proj(_C_QA, _C_KA), gqa_ref[...]).astype(BF16)
    ka = head64_norm(proj(_C_KA, _C_VA), gka_ref[...]).astype(BF16)
    for g in range(A_WIDTH // LANES):
        ka_ref[g] = ka[:, g * LANES:(g + 1) * LANES]
    store_transposed(vat_ref, proj(_C_VA, _C_QI))

    qi_ref[...] = proj(_C_QI, _C_KI).astype(BF16)
    ki_ref[...] = proj(_C_KI, _C_WI).astype(BF16)
    wit_ref[...] = proj(_C_WI, _C_GATE).T[:IDX_HEADS, :]

    z = proj(_C_GATE, _C_CQ)
    gate_ref[...] = (z / (1.0 + jnp.exp(-z))).astype(BF16)

    rc = rc_ref[...]
    rs1 = rs1_ref[...]
    rs2 = rs2_ref[...]

    def head96(uh, g):
        ss = jnp.sum(uh * uh, axis=-1, keepdims=True) * (1.0 / B_QK)
        return _rope(uh * lax.rsqrt(ss + EPS) * g, rc, rs1, rs2).astype(BF16)

    cq = proj(_C_CQ, _C_CKV)
    cq = cq * lax.rsqrt(jnp.mean(cq * cq, axis=-1, keepdims=True) + EPS) * gcq_ref[...]
    q = jnp.dot(cq.astype(BF16), wuq_ref[...], preferred_element_type=F32)
    gqb = gqb_ref[...]
    for hd in range(B_HEADS):
        qb_ref[hd] = head96(q[:, hd * LANES:(hd + 1) * LANES], gqb)

    ckv = proj(_C_CKV, _C_KR)
    ckv = ckv * lax.rsqrt(jnp.mean(ckv * ckv, axis=-1, keepdims=True) + EPS) * gckv_ref[...]
    ckv = ckv.astype(BF16)
    kk = jnp.dot(ckv, wukvk_ref[...], preferred_element_type=F32)
    store_transposed(vbt_ref, jnp.dot(ckv, wukvv_ref[...], preferred_element_type=F32))
    krope = proj(_C_KR, _C_QM)
    gkb = gkb_ref[...]
    for hd in range(B_HEADS):
        kb_ref[hd] = head96(kk[:, hd * LANES:(hd + 1) * LANES] + krope, gkb)

    qm_ref[...] = head64_norm(proj(_C_QM, _C_END), gqm_ref[...]).astype(BF16)


def _memkv_kernel(mem_ref, gmem_ref, wmk_ref, wmv_ref, gmat_ref, gkm_ref, km_ref, vm_ref):
    x = mem_ref[...]
    ms = jnp.mean(x * x, axis=-1, keepdims=True)
    m = (x * lax.rsqrt(ms + EPS) * gmem_ref[...]).astype(BF16)
    k = jnp.dot(m, wmk_ref[...], preferred_element_type=F32)
    msq = jnp.dot((k * k).astype(BF16), gmat_ref[:M_WIDTH, :M_WIDTH], preferred_element_type=F32)
    k = (k * lax.rsqrt(msq + EPS) * gkm_ref[...]).astype(BF16)
    v = jnp.dot(m, wmv_ref[...], preferred_element_type=F32).astype(BF16)
    for g in range(M_WIDTH // LANES):
        km_ref[g] = k[:, g * LANES:(g + 1) * LANES]
        vm_ref[g] = v[:, g * LANES:(g + 1) * LANES]


def _flash_init(m_ref, l_ref, acc_ref):
    m_ref[...] = jnp.full(m_ref.shape, NEG, F32)
    l_ref[...] = jnp.zeros(l_ref.shape, F32)
    acc_ref[...] = jnp.zeros(acc_ref.shape, F32)


def _flash_block(n_heads, logits_fn, vt_fn, s_ref, p_ref, alpha_ref, m_ref, l_ref, acc_ref):
    for hd in range(n_heads):
        s_ref[hd] = logits_fn(hd)
    for hd in range(n_heads):
        s = s_ref[hd]
        m_prev = m_ref[hd]
        m_new = jnp.maximum(m_prev, jnp.max(s, axis=0, keepdims=True))
        alpha = jnp.exp2(m_prev - m_new)
        p = jnp.exp2(s - m_new)
        l_ref[hd] = alpha * l_ref[hd] + jnp.sum(p, axis=0, keepdims=True)
        p_ref[hd] = p.astype(BF16)
        alpha_ref[hd] = alpha
        m_ref[hd] = m_new
    for hd in range(n_heads):
        acc_ref[hd] = alpha_ref[hd] * acc_ref[hd] + jnp.dot(
            vt_fn(hd), p_ref[hd], preferred_element_type=F32)


def _flash_store(out_ref, l_ref, acc_ref, n_heads):
    for g in range(n_heads // 2):
        o = jnp.concatenate([acc_ref[2 * g] / l_ref[2 * g],
                             acc_ref[2 * g + 1] / l_ref[2 * g + 1]], axis=0)
        out_ref[:, g * LANES:(g + 1) * LANES] = o.T.astype(out_ref.dtype)


def _split_head_pairs(src_ref, dst_ref, n_pairs):
    low = lax.broadcasted_iota(jnp.int32, (src_ref.shape[0], LANES), 1) < HEAD_DIM
    for g in range(n_pairs):
        pair = src_ref[:, g * LANES:(g + 1) * LANES].astype(F32)
        dst_ref[2 * g] = jnp.where(low, pair, 0.0).astype(dst_ref.dtype)
        dst_ref[2 * g + 1] = jnp.where(low, 0.0, pair).astype(dst_ref.dtype)


def _fold8(x):
    return x.reshape(x.shape[0] // SUBLANES, SUBLANES, x.shape[1])


def _dsa_kernel(slopes_ref, qi_ref, wit_ref, ki_ref, qa_ref, ka_ref, vat_ref, ltri_ref,
                out_ref,
                score_ref, qim_ref, qam_ref, lo_ref, s_ref, p_ref, alpha_ref, m_ref, l_ref, acc_ref,
                *, topk):
    i = pl.program_id(1)
    nkb = i + 1
    q0 = i * TQ

    _split_head_pairs(qi_ref, qim_ref, IDX_HEADS // 2)
    _split_head_pairs(qa_ref, qam_ref, A_HEADS // 2)

    kidx = lax.broadcasted_iota(jnp.int32, (TK, TQ), 0)
    qpos = q0 + lax.broadcasted_iota(jnp.int32, (TK, TQ), 1)

    wt = wit_ref[...]

    def score_block(kb, carry):
        k0 = pl.multiple_of(kb * TK, TK)
        kblk = ki_ref[pl.ds(k0, TK), :]
        acc = jnp.zeros((TK, TQ), F32)
        for hd in range(IDX_HEADS):
            acc = acc + jnp.maximum(_nt_dot(kblk, qim_ref[hd]), 0.0) * wt[hd:hd + 1, :]
        adm = ((k0 + kidx) >> 6) <= (qpos >> 6)
        score_ref[kb] = jnp.where(adm, acc, -jnp.inf)
        return carry

    lax.fori_loop(0, nkb, score_block, 0)

    qrow = q0 + lax.broadcasted_iota(jnp.int32, (1, TQ), 1)
    n_adm = ((qrow >> 6) + 1) << 6
    k_eff = jnp.minimum(n_adm, topk).astype(F32)

    def stats(kb, c):
        mn, mx = c
        s = _fold8(score_ref[kb])
        mn = jnp.minimum(mn, jnp.min(jnp.where(s == -jnp.inf, jnp.inf, s), axis=0))
        return mn, jnp.maximum(mx, jnp.max(s, axis=0))

    mn, mx = lax.fori_loop(0, nkb, stats, (jnp.full((SUBLANES, TQ), jnp.inf, F32),
                                           jnp.full((SUBLANES, TQ), -jnp.inf, F32)))
    lo = jnp.min(mn, axis=0, keepdims=True)
    hi = jnp.max(mx, axis=0, keepdims=True)

    def count_ge(t):
        def body(kb, acc):
            return acc + jnp.sum(jnp.where(_fold8(score_ref[kb]) >= t, 1.0, 0.0), axis=0)

        acc = lax.fori_loop(0, nkb, body, jnp.zeros((SUBLANES, TQ), F32))
        return jnp.sum(acc, axis=0, keepdims=True)

    def bisect(_, c):
        lo, hi, c_lo = c
        mid = 0.5 * (lo + hi)
        cnt = count_ge(mid)
        ge = cnt >= k_eff
        return jnp.where(ge, mid, lo), jnp.where(ge, hi, mid), jnp.where(ge, cnt, c_lo)

    lo, hi, c_lo = lax.fori_loop(0, N_BISECT, bisect, (lo, hi, n_adm.astype(F32)))
    lo_ref[...] = lo

    for c in range(TQ // LANES):
        cs = slice(c * LANES, (c + 1) * LANES)

        @pl.when(jnp.max(c_lo[:, cs] - k_eff[:, cs]) > 0.0)
        def _():
            k_c = k_eff[:, cs]

            def counts(t):
                def body(kb, c2):
                    s = _fold8(score_ref[kb, :, cs])
                    return (c2[0] + jnp.sum(jnp.where(s >= t, 1.0, 0.0), axis=0),
                            c2[1] + jnp.sum(jnp.where(s > t, 1.0, 0.0), axis=0))

                z = jnp.zeros((SUBLANES, LANES), F32)
                ge, gt = lax.fori_loop(0, nkb, body, (z, z))
                return jnp.sum(ge, axis=0, keepdims=True), jnp.sum(gt, axis=0, keepdims=True)

            def below(t):
                def body(kb, acc):
                    s = _fold8(score_ref[kb, :, cs])
                    return jnp.maximum(acc, jnp.max(jnp.where(s < t, s, -jnp.inf), axis=0))

                acc = lax.fori_loop(0, nkb, body, jnp.full((SUBLANES, LANES), -jnp.inf, F32))
                return jnp.max(acc, axis=0, keepdims=True)

            def pending(ge):
                return jnp.sum(jnp.where(ge >= k_c, 0.0, 1.0))

            def walk(c3):
                t, ge, _, _ = c3
                t = jnp.where(ge >= k_c, t, below(t))
                ge, gt = counts(t)
                return t, ge, gt, pending(ge)

            ge0, gt0 = counts(hi[:, cs])
            thr, _, c_gt, _ = lax.while_loop(lambda c3: c3[3] > 0.0, walk,
                                             (hi[:, cs], ge0, gt0, pending(ge0)))
            need = k_c - c_gt

            def mark(kb, seen):
                s = score_ref[kb, :, cs]
                eqf = jnp.where(s == thr, 1.0, 0.0)
                rank = seen + jnp.dot(ltri_ref[...], eqf.astype(BF16), preferred_element_type=F32)
                tie = jnp.where(rank <= need, eqf, 0.0)
                score_ref[kb, :, cs] = jnp.where(s > thr, 1.0, tie) * 2.0 - 1.0
                return seen + jnp.sum(eqf, axis=0, keepdims=True)

            lax.fori_loop(0, nkb, mark, jnp.zeros((1, LANES), F32))
            lo_ref[:, cs] = jnp.zeros((1, LANES), F32)

    _flash_init(m_ref, l_ref, acc_ref)
    lo_row = lo_ref[...]

    def attend_block(kb, carry):
        k0 = pl.multiple_of(kb * TK, TK)
        dist = jnp.abs(qpos - (k0 + kidx)).astype(F32)
        bias = jnp.where(score_ref[kb] >= lo_row, 0.0, NEG)
        _flash_block(
            A_HEADS,
            lambda hd: _nt_dot(ka_ref[hd // 2, pl.ds(k0, TK), :], qam_ref[hd])
            + (bias - slopes_ref[hd] * dist),
            lambda hd: vat_ref[kb, hd * HEAD_DIM:(hd + 1) * HEAD_DIM, :],
            s_ref, p_ref, alpha_ref, m_ref, l_ref, acc_ref)
        return carry

    lax.fori_loop(0, nkb, attend_block, 0)
    _flash_store(out_ref, l_ref, acc_ref, A_HEADS)


def _mla_kernel(qb_ref, kb_ref, vbt_ref, out_ref, s_ref, p_ref, alpha_ref, m_ref, l_ref, acc_ref):
    i = pl.program_id(1)
    _flash_init(m_ref, l_ref, acc_ref)

    def attend(kb, bias):
        k0 = pl.multiple_of(kb * TK, TK)

        def logits(hd):
            s = _nt_dot(kb_ref[hd, pl.ds(k0, TK), :], qb_ref[hd])
            return s if bias is None else s + bias

        _flash_block(B_HEADS, logits, lambda hd: vbt_ref[kb, hd * B_V:(hd + 1) * B_V, :],
                     s_ref, p_ref, alpha_ref, m_ref, l_ref, acc_ref)

    def full_block(kb, carry):
        attend(kb, None)
        return carry

    lax.fori_loop(0, i, full_block, 0)
    kidx = lax.broadcasted_iota(jnp.int32, (TK, TQ), 0)
    qidx = lax.broadcasted_iota(jnp.int32, (TK, TQ), 1)
    attend(i, jnp.where((kidx >> 6) <= (qidx >> 6), 0.0, NEG))
    _flash_store(out_ref, l_ref, acc_ref, B_HEADS)


def _out_kernel(x_ref, ya_ref, yb_ref, gate_ref, qm_ref, km_ref, vm_ref, wout_ref, out_ref):
    tm = x_ref.shape[0]
    low = lax.broadcasted_iota(jnp.int32, (tm, LANES), 1) < HEAD_DIM
    ym = []
    for g in range(M_HEADS // 2):
        pair = qm_ref[:, g * LANES:(g + 1) * LANES].astype(F32)
        outs = []
        for q in (jnp.where(low, pair, 0.0), jnp.where(low, 0.0, pair)):
            s = _nt_dot(q.astype(BF16), km_ref[g])
            p = jnp.exp2(s - jnp.max(s, axis=1, keepdims=True))
            o = jnp.dot(p.astype(BF16), vm_ref[g], preferred_element_type=F32)
            outs.append(o / jnp.sum(p, axis=1, keepdims=True))
        ym.append(jnp.where(low, outs[0], outs[1]))
    y = jnp.concatenate([ya_ref[...].astype(F32), yb_ref[...].astype(F32)] + ym, axis=1)
    y = (y * gate_ref[...].astype(F32)).astype(BF16)
    out_ref[...] = x_ref[...] + jnp.dot(y, wout_ref[...], preferred_element_type=F32)


def _full(shape):
    n = len(shape)
    return pl.BlockSpec(shape, lambda *_: (0,) * n)


def _params(*sem):
    return pltpu.CompilerParams(dimension_semantics=sem, vmem_limit_bytes=VMEM_LIMIT)


def kernel(x, mem, g_in, w_in, w_uq, g_cq, w_ukv, g_ckv, g_qa, g_ka, g_qb, g_kb,
           g_mem, w_mk, w_mv, g_qm, g_km, w_out):
    B, S, D = x.shape
    n_mem = mem.shape[1]
    assert S % TM == 0 and TM % TK == 0 and TQ == TK
    topk = min(TOPK_MAX, S // 4)
    nkb = S // TK
    l = 0

    wi = w_in[l]
    o = [0]
    for n in (A_WIDTH, A_WIDTH, A_WIDTH, IDX_HEADS * IDX_DIM, IDX_DIM, IDX_HEADS, A_WIDTH,
              Q_LORA, KV_LORA, B_ROPE, B_WIDTH, M_WIDTH, M_WIDTH):
        o.append(o[-1] + n)
    (c_qa, c_ka, c_va, c_qi, c_ki, c_wi, c_za, c_cq, c_ckv, c_kr, c_zb, c_qm, c_zm) = [
        wi[:, o[j]:o[j + 1]] for j in range(13)]
    zeros = lambda n: jnp.zeros((D, n), wi.dtype)
    w_pack = jnp.concatenate([
        c_qa, c_ka, c_va, c_qi,
        c_ki, c_ki,
        c_wi, zeros(LANES - IDX_HEADS),
        c_za, c_zb, c_zm,
        c_cq, c_ckv,
        zeros(B_NOPE), c_kr, zeros(LANES - B_QK),
        c_qm], axis=1).astype(BF16)
    assert w_pack.shape[1] == _C_END

    wuq_p = jnp.pad(w_uq[l].reshape(Q_LORA, B_HEADS, B_QK), ((0, 0), (0, 0), (0, LANES - B_QK))
                    ).reshape(Q_LORA, B_HEADS * LANES).astype(BF16)
    wukv = w_ukv[l].reshape(KV_LORA, B_HEADS, B_NOPE + B_V)
    wukv_k = jnp.pad(wukv[:, :, :B_NOPE], ((0, 0), (0, 0), (0, LANES - B_NOPE))
                     ).reshape(KV_LORA, B_HEADS * LANES).astype(BF16)
    wukv_v = wukv[:, :, B_NOPE:].reshape(KV_LORA, B_WIDTH).astype(BF16)

    idx = jnp.arange(A_WIDTH)
    gmat = jnp.where((idx[:, None] // HEAD_DIM) == (idx[None, :] // HEAD_DIM),
                     1.0 / HEAD_DIM, 0.0).astype(BF16)
    row2 = lambda v: v.reshape(1, -1).astype(F32)
    gqa_t = row2(jnp.tile(g_qa[l], A_HEADS) * (HEAD_DIM ** -0.5 * LOG2E))
    gka_t = row2(jnp.tile(g_ka[l], A_HEADS))
    gqm_t = row2(jnp.tile(g_qm[l], M_HEADS) * (HEAD_DIM ** -0.5 * LOG2E))
    gkm_t = row2(jnp.tile(g_km[l], M_HEADS))
    gqb_p = row2(jnp.pad(g_qb[l], (0, LANES - B_QK)) * (B_QK ** -0.5 * LOG2E))
    gkb_p = row2(jnp.pad(g_kb[l], (0, LANES - B_QK)))

    half = B_ROPE // 2
    pos = jnp.arange(S, dtype=jnp.int32)
    inv = 1.0 / (ROPE_THETA ** (jnp.arange(half, dtype=F32) / half))
    ang = pos.astype(F32)[:, None] * inv[None, :]
    cos, sin = jnp.cos(ang), jnp.sin(ang)
    zs = lambda n: jnp.zeros((S, n), F32)
    rope_c = jnp.concatenate([jnp.ones((S, B_NOPE), F32), cos, cos, jnp.ones((S, LANES - B_QK), F32)], 1)
    rope_s1 = jnp.concatenate([zs(B_NOPE), -sin, zs(half), zs(LANES - B_QK)], 1)
    rope_s2 = jnp.concatenate([zs(B_NOPE), zs(half), sin, zs(LANES - B_QK)], 1)

    slopes = 2.0 ** (-ALIBI_MAX * jnp.arange(1, A_HEADS + 1, dtype=F32) / A_HEADS) * LOG2E
    kk = jnp.arange(TK)
    ltri = (kk[None, :] <= kk[:, None]).astype(BF16)

    tok = lambda w: pl.BlockSpec((None, TM, w), lambda b, t: (b, t, 0))
    grp = lambda n: pl.BlockSpec((None, n, TM, LANES), lambda b, t: (b, 0, t, 0))
    vt_out = lambda w: pl.BlockSpec((None, TM // TK, w, TK), lambda b, t: (b, t, 0, 0))
    tab = pl.BlockSpec((TM, LANES), lambda b, t: (t, 0))
    sds = jax.ShapeDtypeStruct
    (qa, ka, vat, qi, ki, wit, gate, qb, kb, vbt, qm) = pl.pallas_call(
        _inproj_kernel,
        grid=(B, S // TM),
        in_specs=[tok(D), _full((1, D)), _full((D, _C_END)), _full((A_WIDTH, A_WIDTH)),
                  _full((1, A_WIDTH)), _full((1, A_WIDTH)), _full((1, Q_LORA)),
                  _full((Q_LORA, B_HEADS * LANES)), _full((1, LANES)), _full((1, KV_LORA)),
                  _full((KV_LORA, B_HEADS * LANES)), _full((KV_LORA, B_WIDTH)),
                  _full((1, LANES)), _full((1, M_WIDTH)), tab, tab, tab],
        out_specs=[tok(A_WIDTH), grp(3), vt_out(A_WIDTH), tok(IDX_HEADS * IDX_DIM), tok(LANES),
                   pl.BlockSpec((None, IDX_HEADS, TM), lambda b, t: (b, 0, t)), tok(D),
                   grp(B_HEADS), grp(B_HEADS), vt_out(B_WIDTH), tok(M_WIDTH)],
        out_shape=[sds((B, S, A_WIDTH), BF16), sds((B, 3, S, LANES), BF16),
                   sds((B, nkb, A_WIDTH, TK), BF16), sds((B, S, IDX_HEADS * IDX_DIM), BF16),
                   sds((B, S, LANES), BF16), sds((B, IDX_HEADS, S), F32), sds((B, S, D), BF16),
                   sds((B, B_HEADS, S, LANES), BF16), sds((B, B_HEADS, S, LANES), BF16),
                   sds((B, nkb, B_WIDTH, TK), BF16), sds((B, S, M_WIDTH), BF16)],
        compiler_params=_params("parallel", "parallel"),
        name="inproj",
    )(x, row2(g_in[l]), w_pack, gmat, gqa_t, gka_t, row2(g_cq[l]), wuq_p, gqb_p,
      row2(g_ckv[l]), wukv_k, wukv_v, gkb_p, gqm_t, rope_c, rope_s1, rope_s2)

    km, vm = pl.pallas_call(
        _memkv_kernel,
        grid=(B,),
        in_specs=[pl.BlockSpec((None, n_mem, D), lambda b: (b, 0, 0)), _full((1, D)),
                  _full((D, M_WIDTH)), _full((D, M_WIDTH)), _full((A_WIDTH, A_WIDTH)),
                  _full((1, M_WIDTH))],
        out_specs=[pl.BlockSpec((None, 2, n_mem, LANES), lambda b: (b, 0, 0, 0))] * 2,
        out_shape=[sds((B, 2, n_mem, LANES), BF16)] * 2,
        compiler_params=_params("parallel"),
        name="memkv",
    )(mem, row2(g_mem[l]), w_mk[l].astype(BF16), w_mv[l].astype(BF16), gmat, gkm_t)

    qtile = lambda w: pl.BlockSpec((None, TQ, w), lambda b, t: (b, t, 0))
    seq = lambda n: pl.BlockSpec((None, n, S, LANES), lambda b, t: (b, 0, 0, 0))
    vt_in = lambda w: pl.BlockSpec((None, nkb, w, TK), lambda b, t: (b, 0, 0, 0))
    heads = lambda n: pl.BlockSpec((None, n, TQ, LANES), lambda b, t: (b, 0, t, 0))
    flash_scratch = lambda n: [pltpu.VMEM((n, TK, TQ), F32), pltpu.VMEM((n, TK, TQ), BF16),
                               pltpu.VMEM((n, 1, TQ), F32), pltpu.VMEM((n, 1, TQ), F32),
                               pltpu.VMEM((n, 1, TQ), F32), pltpu.VMEM((n, HEAD_DIM, TQ), F32)]
    ya = pl.pallas_call(
        functools.partial(_dsa_kernel, topk=topk),
        grid=(B, S // TQ),
        in_specs=[pl.BlockSpec(memory_space=pltpu.SMEM), qtile(IDX_HEADS * IDX_DIM),
                  pl.BlockSpec((None, IDX_HEADS, TQ), lambda b, t: (b, 0, t)),
                  pl.BlockSpec((None, S, LANES), lambda b, t: (b, 0, 0)),
                  qtile(A_WIDTH), seq(3), vt_in(A_WIDTH), _full((TK, TK))],
        out_specs=qtile(A_WIDTH),
        out_shape=sds((B, S, A_WIDTH), BF16),
        scratch_shapes=[pltpu.VMEM((nkb, TK, TQ), F32),
                        pltpu.VMEM((IDX_HEADS, TQ, LANES), BF16),
                        pltpu.VMEM((A_HEADS, TQ, LANES), BF16),
                        pltpu.VMEM((1, TQ), F32)] + flash_scratch(A_HEADS),
        compiler_params=_params("parallel", "arbitrary"),
        name="dsa",
    )(slopes, qi, wit, ki, qa, ka, vat, ltri)

    yb = pl.pallas_call(
        _mla_kernel,
        grid=(B, S // TQ),
        in_specs=[heads(B_HEADS), seq(B_HEADS), vt_in(B_WIDTH)],
        out_specs=qtile(B_WIDTH),
        out_shape=sds((B, S, B_WIDTH), BF16),
        scratch_shapes=flash_scratch(B_HEADS),
        compiler_params=_params("parallel", "arbitrary"),
        name="mla",
    )(qb, kb, vbt)

    memkv = pl.BlockSpec((None, 2, n_mem, LANES), lambda b, t: (b, 0, 0, 0))
    return pl.pallas_call(
        _out_kernel,
        grid=(B, S // TM),
        in_specs=[tok(D), tok(A_WIDTH), tok(B_WIDTH), tok(D), tok(M_WIDTH), memkv, memkv,
                  _full((D, D))],
        out_specs=tok(D),
        out_shape=sds((B, S, D), x.dtype),
        compiler_params=_params("parallel", "parallel"),
        name="outproj",
    )(x, ya, yb, gate, qm, km, vm, w_out[l].astype(BF16))
```

```python
import functools
import math

import jax
import jax.numpy as jnp
from jax import lax
from jax.experimental import pallas as pl
from jax.experimental.pallas import tpu as pltpu

F32 = jnp.float32
BF16 = jnp.bfloat16

CHUNK = 64
EPS = 1e-6
HEAD_DIM = 64
A_HEADS = 6
IDX_HEADS = 8
IDX_DIM = 64
TOPK_MAX = 256
ALIBI_MAX = 8.0
B_HEADS = 6
B_NOPE = 64
B_ROPE = 32
B_V = 64
B_QK = B_NOPE + B_ROPE
Q_LORA = 256
KV_LORA = 128
ROPE_THETA = 10000.0
M_HEADS = 4
A_WIDTH = A_HEADS * HEAD_DIM
B_WIDTH = B_HEADS * B_V
M_WIDTH = M_HEADS * HEAD_DIM
D_MIX = A_WIDTH + B_WIDTH + M_WIDTH
LOG2E = math.log2(math.e)
N_ALIBI_PIECES = 3

LANES = 128
SUBLANES = 8
VMEM_LIMIT = 52 * 1024 * 1024

TM = 512
TN = 512
TQ = 256
TK = 256
N_BISECT = 20
V_ROWS = 80
NEG = -1e30

_C_QI = 0
_C_KI = _C_QI + IDX_HEADS * IDX_DIM
_C_WI = _C_KI + LANES
_C_QM = _C_WI + LANES
_C_GATE = _C_QM + M_WIDTH
_C_QA = _C_GATE + D_MIX
_C_KA = _C_QA + A_WIDTH
_C_VA = _C_KA + A_WIDTH
_C_CQ = _C_VA + A_WIDTH
_C_CKV = _C_CQ + Q_LORA
_C_KR = _C_CKV + KV_LORA
_C_END = _C_KR + LANES


def _nt_dot(a, b):
    return lax.dot_general(a, b, (((1,), (1,)), ((), ())), preferred_element_type=F32)


def _rope(xh, c, s1, s2):
    nxt = pltpu.roll(xh, LANES - B_ROPE // 2, 1)
    prv = pltpu.roll(xh, B_ROPE // 2, 1)
    return xh * c + nxt * s1 + prv * s2


def _head64_norm(u, gmat_ref, g):
    n = u.shape[1]
    msq = jnp.dot((u * u).astype(BF16), gmat_ref[:n, :n], preferred_element_type=F32)
    return u * lax.rsqrt(msq + EPS) * g


def _inproj_kernel(x_ref, gin_ref, w_ref, u_ref):
    x = x_ref[...]
    ms = jnp.mean(x * x, axis=-1, keepdims=True)
    h = (x * lax.rsqrt(ms + EPS) * gin_ref[...]).astype(BF16)
    for a in range(0, _C_END, TN):
        b = min(a + TN, _C_END)
        u_ref[:, a:b] = jnp.dot(h, w_ref[:, a:b], preferred_element_type=F32).astype(BF16)


def _tokprep_kernel(u_ref, gmat_ref, gqa_ref, gka_ref, qfeat_ref, kfeat_ref, gcq_ref, wuq_ref,
                    gqb_ref, gckv_ref, wukvk_ref, wukvv_ref, gkb_ref, rc_ref, rs1_ref, rs2_ref,
                    qa_ref, ka_ref, vat_ref, qb_ref, kb_ref, vbt_ref):
    def cols(a, b):
        return u_ref[:, a:b].astype(F32)

    ones_rows = (lax.broadcasted_iota(jnp.int32, (V_ROWS - HEAD_DIM, TK), 0) == 0).astype(BF16)

    def store_transposed(dst_ref, v):
        vt = v.T.astype(BF16)
        for j in range(TM // TK):
            for hd in range(v.shape[1] // HEAD_DIM):
                r = hd * V_ROWS
                dst_ref[j, r:r + HEAD_DIM, :] = vt[hd * HEAD_DIM:(hd + 1) * HEAD_DIM,
                                                   j * TK:(j + 1) * TK]
                dst_ref[j, r + HEAD_DIM:r + V_ROWS, :] = ones_rows

    low = lax.broadcasted_iota(jnp.int32, (TM, LANES), 1) < HEAD_DIM

    def store_heads(dst_ref, v, feat_fn):
        for hd in range(A_HEADS):
            pair = v[:, (hd // 2) * LANES:(hd // 2 + 1) * LANES]
            if hd % 2:
                pair = pltpu.roll(pair, HEAD_DIM, 1)
            dst_ref[hd] = (jnp.where(low, pair, 0.0) + feat_fn(hd)).astype(BF16)

    store_heads(qa_ref, _head64_norm(cols(_C_QA, _C_KA), gmat_ref, gqa_ref[...]),
                lambda hd: qfeat_ref[hd:hd + 1, :])
    kfeat = kfeat_ref[...]
    store_heads(ka_ref, _head64_norm(cols(_C_KA, _C_VA), gmat_ref, gka_ref[...]), lambda hd: kfeat)
    store_transposed(vat_ref, cols(_C_VA, _C_CQ))

    rc = rc_ref[...]
    rs1 = rs1_ref[...]
    rs2 = rs2_ref[...]

    def head96(uh, g):
        ss = jnp.sum(uh * uh, axis=-1, keepdims=True) * (1.0 / B_QK)
        return _rope(uh * lax.rsqrt(ss + EPS) * g, rc, rs1, rs2).astype(BF16)

    cq = cols(_C_CQ, _C_CKV)
    cq = cq * lax.rsqrt(jnp.mean(cq * cq, axis=-1, keepdims=True) + EPS) * gcq_ref[...]
    q = jnp.dot(cq.astype(BF16), wuq_ref[...], preferred_element_type=F32)
    gqb = gqb_ref[...]
    for hd in range(B_HEADS):
        qb_ref[hd] = head96(q[:, hd * LANES:(hd + 1) * LANES], gqb)

    ckv = cols(_C_CKV, _C_KR)
    ckv = ckv * lax.rsqrt(jnp.mean(ckv * ckv, axis=-1, keepdims=True) + EPS) * gckv_ref[...]
    ckv = ckv.astype(BF16)
    kk = jnp.dot(ckv, wukvk_ref[...], preferred_element_type=F32)
    store_transposed(vbt_ref, jnp.dot(ckv, wukvv_ref[...], preferred_element_type=F32))
    krope = cols(_C_KR, _C_END)
    gkb = gkb_ref[...]
    for hd in range(B_HEADS):
        kb_ref[hd] = head96(kk[:, hd * LANES:(hd + 1) * LANES] + krope, gkb)


def _memkv_kernel(mem_ref, gmem_ref, wmk_ref, wmv_ref, gmat_ref, gkm_ref, km_ref, vm_ref):
    x = mem_ref[...]
    ms = jnp.mean(x * x, axis=-1, keepdims=True)
    m = (x * lax.rsqrt(ms + EPS) * gmem_ref[...]).astype(BF16)
    k = _head64_norm(jnp.dot(m, wmk_ref[...], preferred_element_type=F32), gmat_ref,
                     gkm_ref[...]).astype(BF16)
    v = jnp.dot(m, wmv_ref[...], preferred_element_type=F32).astype(BF16)
    for g in range(M_WIDTH // LANES):
        km_ref[g] = k[:, g * LANES:(g + 1) * LANES]
        vm_ref[g] = v[:, g * LANES:(g + 1) * LANES]


def _flash_init(m_ref, acc_ref):
    m_ref[...] = jnp.full(m_ref.shape, NEG, F32)
    acc_ref[...] = jnp.zeros(acc_ref.shape, F32)


def _flash_softmax_pv(n_heads, s_ref, vt_fn, p_ref, alpha_ref, m_ref, acc_ref):
    for hd in range(n_heads):
        s = s_ref[hd]
        m_prev = m_ref[hd]
        m_new = jnp.maximum(m_prev, jnp.max(s, axis=0, keepdims=True))
        p_ref[hd] = jnp.exp2(s - m_new).astype(BF16)
        alpha_ref[hd] = jnp.exp2(m_prev - m_new)
        m_ref[hd] = m_new
    for hd in range(n_heads):
        acc_ref[hd] = alpha_ref[hd] * acc_ref[hd] + jnp.dot(
            vt_fn(hd), p_ref[hd], preferred_element_type=F32)


def _flash_loop(n_blocks, logits_fn, update_fn):
    last = n_blocks - 1
    logits_fn(0, 0)

    def pair(j, carry):
        kb = 2 * j
        logits_fn(jnp.minimum(kb + 1, last), 1)
        update_fn(kb, 0)
        logits_fn(jnp.minimum(kb + 2, last), 0)
        update_fn(kb + 1, 1)
        return carry

    lax.fori_loop(0, n_blocks // 2, pair, 0)

    @pl.when(n_blocks % 2 == 1)
    def _():
        update_fn(last, 0)


def _flash_store(out_ref, acc_ref, n_heads):
    def head(hd):
        return acc_ref[hd, :HEAD_DIM, :] / acc_ref[hd, HEAD_DIM:HEAD_DIM + 1, :]

    for g in range(n_heads // 2):
        o = jnp.concatenate([head(2 * g), head(2 * g + 1)], axis=0)
        out_ref[:, g * LANES:(g + 1) * LANES] = o.T.astype(out_ref.dtype)


def _split_head_pairs(src_ref, dst_ref, n_pairs):
    low = lax.broadcasted_iota(jnp.int32, (src_ref.shape[0], LANES), 1) < HEAD_DIM
    for g in range(n_pairs):
        pair = src_ref[:, g * LANES:(g + 1) * LANES].astype(F32)
        dst_ref[2 * g] = jnp.where(low, pair, 0.0).astype(dst_ref.dtype)
        dst_ref[2 * g + 1] = jnp.where(low, 0.0, pair).astype(dst_ref.dtype)


def _fold8(x):
    return x.reshape(x.shape[0] // SUBLANES, SUBLANES, x.shape[1])


def _dsa_kernel(slopes_ref, qi_ref, wi_ref, ki_ref, qa_ref, ka_ref, vat_ref, ltri_ref,
                out_ref,
                score_ref, qim_ref, lo_ref, s_ref, p_ref, alpha_ref, m_ref, acc_ref,
                *, topk):
    i = pl.program_id(1)
    nkb = i + 1
    npair = (nkb + 1) // 2
    q0 = i * TQ

    _split_head_pairs(qi_ref, qim_ref, IDX_HEADS // 2)

    kidx = lax.broadcasted_iota(jnp.int32, (TK, TQ), 0)
    qidx = lax.broadcasted_iota(jnp.int32, (TK, TQ), 1)

    wt = wi_ref[...].astype(F32).T

    def score_block(kb, carry):
        k0 = pl.multiple_of(kb * TK, TK)
        kblk = ki_ref[pl.ds(k0, TK), :]
        acc = jnp.zeros((TK, TQ), F32)
        for hd in range(IDX_HEADS):
            acc = acc + jnp.maximum(_nt_dot(kblk, qim_ref[hd]), 0.0) * wt[hd:hd + 1, :]
        adm = ((k0 + kidx) >> 6) <= ((q0 + qidx) >> 6)
        score_ref[kb] = jnp.where(adm, acc, -jnp.inf)
        return carry

    lax.fori_loop(0, nkb, score_block, 0)

    @pl.when(nkb % 2 == 1)
    def _():
        score_ref[nkb] = jnp.full((TK, TQ), -jnp.inf, F32)

    qrow = q0 + lax.broadcasted_iota(jnp.int32, (1, TQ), 1)
    n_adm = ((qrow >> 6) + 1) << 6
    k_eff = jnp.minimum(n_adm, topk).astype(F32)

    def stats(j, c):
        mn, mx = c
        for kb in (2 * j, 2 * j + 1):
            s = _fold8(score_ref[kb])
            mn = jnp.minimum(mn, jnp.min(jnp.where(s == -jnp.inf, jnp.inf, s), axis=0))
            mx = jnp.maximum(mx, jnp.max(s, axis=0))
        return mn, mx

    mn, mx = lax.fori_loop(0, npair, stats, (jnp.full((SUBLANES, TQ), jnp.inf, F32),
                                             jnp.full((SUBLANES, TQ), -jnp.inf, F32)))
    lo = jnp.min(mn, axis=0, keepdims=True)
    hi = jnp.max(mx, axis=0, keepdims=True)

    def count_ge(t):
        def body(j, acc):
            a0, a1 = acc
            a0 = a0 + jnp.sum(jnp.where(_fold8(score_ref[2 * j]) >= t, 1.0, 0.0), axis=0)
            a1 = a1 + jnp.sum(jnp.where(_fold8(score_ref[2 * j + 1]) >= t, 1.0, 0.0), axis=0)
            return a0, a1

        z = jnp.zeros((SUBLANES, TQ), F32)
        a0, a1 = lax.fori_loop(0, npair, body, (z, z))
        return jnp.sum(a0 + a1, axis=0, keepdims=True)

    def bisect(_, c):
        lo, hi, c_lo = c
        mid = 0.5 * (lo + hi)
        cnt = count_ge(mid)
        ge = cnt >= k_eff
        return jnp.where(ge, mid, lo), jnp.where(ge, hi, mid), jnp.where(ge, cnt, c_lo)

    lo, hi, c_lo = lax.fori_loop(0, N_BISECT, bisect, (lo, hi, n_adm.astype(F32)))
    lo_ref[...] = lo

    for c in range(TQ // LANES):
        cs = slice(c * LANES, (c + 1) * LANES)

        @pl.when(jnp.max(c_lo[:, cs] - k_eff[:, cs]) > 0.0)
        def _():
            k_c = k_eff[:, cs]

            def counts(t):
                def body(kb, c2):
                    s = _fold8(score_ref[kb, :, cs])
                    return (c2[0] + jnp.sum(jnp.where(s >= t, 1.0, 0.0), axis=0),
                            c2[1] + jnp.sum(jnp.where(s > t, 1.0, 0.0), axis=0))

                z = jnp.zeros((SUBLANES, LANES), F32)
                ge, gt = lax.fori_loop(0, nkb, body, (z, z))
                return jnp.sum(ge, axis=0, keepdims=True), jnp.sum(gt, axis=0, keepdims=True)

            def largest(keep):
                def body(kb, acc):
                    s = _fold8(score_ref[kb, :, cs])
                    return jnp.maximum(acc, jnp.max(jnp.where(keep(s), s, -jnp.inf), axis=0))

                acc = lax.fori_loop(0, nkb, body, jnp.full((SUBLANES, LANES), -jnp.inf, F32))
                return jnp.max(acc, axis=0, keepdims=True)

            def pending(ge):
                return jnp.sum(jnp.where(ge >= k_c, 0.0, 1.0))

            def walk(c3):
                t, ge, _, _ = c3
                t = jnp.where(ge >= k_c, t, largest(lambda s: s < t))
                ge, gt = counts(t)
                return t, ge, gt, pending(ge)

            hi_c = hi[:, cs]
            t0 = largest(lambda s: s <= hi_c)
            ge0, gt0 = counts(t0)
            thr, _, c_gt, _ = lax.while_loop(lambda c3: c3[3] > 0.0, walk,
                                             (t0, ge0, gt0, pending(ge0)))
            need = k_c - c_gt

            def mark(kb, seen):
                s = score_ref[kb, :, cs]
                eqf = jnp.where(s == thr, 1.0, 0.0)
                rank = seen + jnp.dot(ltri_ref[...], eqf.astype(BF16), preferred_element_type=F32)
                tie = jnp.where(rank <= need, eqf, 0.0)
                score_ref[kb, :, cs] = jnp.where(s > thr, 1.0, tie) * 2.0 - 1.0
                return seen + jnp.sum(eqf, axis=0, keepdims=True)

            lax.fori_loop(0, nkb, mark, jnp.zeros((1, LANES), F32))
            lo_ref[:, cs] = jnp.zeros((1, LANES), F32)

    _flash_init(m_ref, acc_ref)
    lo_row = lo_ref[...]

    def logits(kb, buf, future=None):
        k0 = pl.multiple_of(kb * TK, TK)
        bias = jnp.where(score_ref[kb] >= lo_row, 0.0, NEG)
        for hd in range(A_HEADS):
            s = _nt_dot(ka_ref[hd, pl.ds(k0, TK), :], qa_ref[hd])
            s_ref[buf, hd] = s + (bias if future is None else bias - slopes_ref[hd] * future)

    def update(kb, buf):
        _flash_softmax_pv(A_HEADS, s_ref.at[buf],
                          lambda hd: vat_ref[kb, hd * V_ROWS:(hd + 1) * V_ROWS, :],
                          p_ref, alpha_ref, m_ref, acc_ref)

    @pl.when(i > 0)
    def _():
        _flash_loop(i, logits, update)

    logits(i, 0, 2.0 * jnp.maximum(kidx - qidx, 0).astype(F32))
    update(i, 0)
    _flash_store(out_ref, acc_ref, A_HEADS)


def _mla_kernel(qb_ref, kb_ref, vbt_ref, out_ref, s_ref, p_ref, alpha_ref, m_ref, acc_ref):
    i = pl.program_id(1)
    _flash_init(m_ref, acc_ref)

    def logits(kb, buf, bias=None):
        k0 = pl.multiple_of(kb * TK, TK)
        for hd in range(B_HEADS):
            s = _nt_dot(kb_ref[hd, pl.ds(k0, TK), :], qb_ref[hd])
            s_ref[buf, hd] = s if bias is None else s + bias

    def update(kb, buf):
        _flash_softmax_pv(B_HEADS, s_ref.at[buf],
                          lambda hd: vbt_ref[kb, hd * V_ROWS:(hd + 1) * V_ROWS, :],
                          p_ref, alpha_ref, m_ref, acc_ref)

    @pl.when(i > 0)
    def _():
        _flash_loop(i, logits, update)

    kidx = lax.broadcasted_iota(jnp.int32, (TK, TQ), 0)
    qidx = lax.broadcasted_iota(jnp.int32, (TK, TQ), 1)
    logits(i, 0, jnp.where((kidx >> 6) <= (qidx >> 6), 0.0, NEG))
    update(i, 0)
    _flash_store(out_ref, acc_ref, B_HEADS)


def _out_kernel(x_ref, ya_ref, yb_ref, z_ref, qm_ref, gmat_ref, gqm_ref, km_ref, vm_ref, wout_ref,
                out_ref):
    tm = x_ref.shape[0]
    low = lax.broadcasted_iota(jnp.int32, (tm, LANES), 1) < HEAD_DIM
    qm = _head64_norm(qm_ref[...].astype(F32), gmat_ref, gqm_ref[...])
    ym = []
    for g in range(M_HEADS // 2):
        pair = qm[:, g * LANES:(g + 1) * LANES]
        outs = []
        for q in (jnp.where(low, pair, 0.0), jnp.where(low, 0.0, pair)):
            s = _nt_dot(q.astype(BF16), km_ref[g])
            p = jnp.exp2(s - jnp.max(s, axis=1, keepdims=True))
            o = jnp.dot(p.astype(BF16), vm_ref[g], preferred_element_type=F32)
            outs.append(o / jnp.sum(p, axis=1, keepdims=True))
        ym.append(jnp.where(low, outs[0], outs[1]))
    y = jnp.concatenate([ya_ref[...].astype(F32), yb_ref[...].astype(F32)] + ym, axis=1)
    z = z_ref[...].astype(F32)
    y = (y * (z / (1.0 + jnp.exp(-z)))).astype(BF16)
    out_ref[...] = x_ref[...] + jnp.dot(y, wout_ref[...], preferred_element_type=F32)


def _full(shape):
    n = len(shape)
    return pl.BlockSpec(shape, lambda *_: (0,) * n)


def _params(*sem):
    return pltpu.CompilerParams(dimension_semantics=sem, vmem_limit_bytes=VMEM_LIMIT)


def _bf16_pieces(v, n):
    out, rest = [], v
    for _ in range(n):
        piece = rest.astype(BF16).astype(F32)
        out.append(piece)
        rest = rest - piece
    return jnp.stack(out, axis=-1)


def kernel(x, mem, g_in, w_in, w_uq, g_cq, w_ukv, g_ckv, g_qa, g_ka, g_qb, g_kb,
           g_mem, w_mk, w_mv, g_qm, g_km, w_out):
    B, S, D = x.shape
    n_mem = mem.shape[1]
    assert S % TM == 0 and TM % TK == 0 and TQ == TK and D == D_MIX and (S // TK) % 2 == 0
    assert _C_QI == 0 and _C_KI % LANES == 0 and _C_WI % LANES == 0
    assert _C_QM % M_WIDTH == 0 and _C_GATE % D_MIX == 0
    topk = min(TOPK_MAX, S // 4)
    nkb = S // TK
    l = 0

    wi = w_in[l]
    o = [0]
    for n in (A_WIDTH, A_WIDTH, A_WIDTH, IDX_HEADS * IDX_DIM, IDX_DIM, IDX_HEADS, A_WIDTH,
              Q_LORA, KV_LORA, B_ROPE, B_WIDTH, M_WIDTH, M_WIDTH):
        o.append(o[-1] + n)
    (c_qa, c_ka, c_va, c_qi, c_ki, c_wi, c_za, c_cq, c_ckv, c_kr, c_zb, c_qm, c_zm) = [
        wi[:, o[j]:o[j + 1]] for j in range(13)]
    zeros = lambda n: jnp.zeros((D, n), wi.dtype)
    w_pack = jnp.concatenate([
        c_qi,
        c_ki, c_ki,
        c_wi, zeros(LANES - IDX_HEADS),
        c_qm,
        c_za, c_zb, c_zm,
        c_qa, c_ka, c_va,
        c_cq, c_ckv,
        zeros(B_NOPE), c_kr, zeros(LANES - B_QK),
    ], axis=1).astype(BF16)
    assert w_pack.shape[1] == _C_END

    wuq_p = jnp.pad(w_uq[l].reshape(Q_LORA, B_HEADS, B_QK), ((0, 0), (0, 0), (0, LANES - B_QK))
                    ).reshape(Q_LORA, B_HEADS * LANES).astype(BF16)
    wukv = w_ukv[l].reshape(KV_LORA, B_HEADS, B_NOPE + B_V)
    wukv_k = jnp.pad(wukv[:, :, :B_NOPE], ((0, 0), (0, 0), (0, LANES - B_NOPE))
                     ).reshape(KV_LORA, B_HEADS * LANES).astype(BF16)
    wukv_v = wukv[:, :, B_NOPE:].reshape(KV_LORA, B_WIDTH).astype(BF16)

    idx = jnp.arange(A_WIDTH)
    gmat = jnp.where((idx[:, None] // HEAD_DIM) == (idx[None, :] // HEAD_DIM),
                     1.0 / HEAD_DIM, 0.0).astype(BF16)
    row2 = lambda v: v.reshape(1, -1).astype(F32)
    gqa_t = row2(jnp.tile(g_qa[l], A_HEADS) * (HEAD_DIM ** -0.5 * LOG2E))
    gka_t = row2(jnp.tile(g_ka[l], A_HEADS))
    gqm_t = row2(jnp.tile(g_qm[l], M_HEADS) * (HEAD_DIM ** -0.5 * LOG2E))
    gkm_t = row2(jnp.tile(g_km[l], M_HEADS))
    gqb_p = row2(jnp.pad(g_qb[l], (0, LANES - B_QK)) * (B_QK ** -0.5 * LOG2E))
    gkb_p = row2(jnp.pad(g_kb[l], (0, LANES - B_QK)))

    half = B_ROPE // 2
    pos = jnp.arange(S, dtype=jnp.int32)
    inv = 1.0 / (ROPE_THETA ** (jnp.arange(half, dtype=F32) / half))
    ang = pos.astype(F32)[:, None] * inv[None, :]
    cos, sin = jnp.cos(ang), jnp.sin(ang)
    zs = lambda n: jnp.zeros((S, n), F32)
    rope_c = jnp.concatenate([jnp.ones((S, B_NOPE), F32), cos, cos, jnp.ones((S, LANES - B_QK), F32)], 1)
    rope_s1 = jnp.concatenate([zs(B_NOPE), -sin, zs(half), zs(LANES - B_QK)], 1)
    rope_s2 = jnp.concatenate([zs(B_NOPE), zs(half), sin, zs(LANES - B_QK)], 1)

    slopes = 2.0 ** (-ALIBI_MAX * jnp.arange(1, A_HEADS + 1, dtype=F32) / A_HEADS) * LOG2E
    pieces = _bf16_pieces(slopes, N_ALIBI_PIECES)
    n_feat = 2 * N_ALIBI_PIECES
    qfeat = jnp.concatenate([jnp.zeros((A_HEADS, HEAD_DIM), F32), pieces * float(CHUNK), pieces,
                             jnp.zeros((A_HEADS, LANES - HEAD_DIM - n_feat), F32)], axis=1)
    pos_hi = jnp.repeat((pos // CHUNK).astype(F32)[:, None], N_ALIBI_PIECES, axis=1)
    pos_lo = jnp.repeat((pos % CHUNK).astype(F32)[:, None], N_ALIBI_PIECES, axis=1)
    kfeat = jnp.concatenate([zs(HEAD_DIM), pos_hi, pos_lo, zs(LANES - HEAD_DIM - n_feat)], axis=1)

    kk = jnp.arange(TK)
    ltri = (kk[None, :] <= kk[:, None]).astype(BF16)

    sds = jax.ShapeDtypeStruct
    tok = lambda w, j=0: pl.BlockSpec((None, TM, w), lambda b, t: (b, t, j))
    u = pl.pallas_call(
        _inproj_kernel,
        grid=(B, S // TM),
        in_specs=[tok(D), _full((1, D)), _full((D, _C_END))],
        out_specs=tok(_C_END),
        out_shape=sds((B, S, _C_END), BF16),
        compiler_params=_params("parallel", "parallel"),
        name="inproj",
    )(x, row2(g_in[l]), w_pack)

    grp = lambda n: pl.BlockSpec((None, n, TM, LANES), lambda b, t: (b, 0, t, 0))
    vt_out = lambda n: pl.BlockSpec((None, TM // TK, n * V_ROWS, TK), lambda b, t: (b, t, 0, 0))
    tab = pl.BlockSpec((TM, LANES), lambda b, t: (t, 0))
    qa, ka, vat, qb, kb, vbt = pl.pallas_call(
        _tokprep_kernel,
        grid=(B, S // TM),
        in_specs=[tok(_C_END), _full((A_WIDTH, A_WIDTH)), _full((1, A_WIDTH)), _full((1, A_WIDTH)),
                  _full((A_HEADS, LANES)), tab, _full((1, Q_LORA)),
                  _full((Q_LORA, B_HEADS * LANES)), _full((1, LANES)), _full((1, KV_LORA)),
                  _full((KV_LORA, B_HEADS * LANES)), _full((KV_LORA, B_WIDTH)),
                  _full((1, LANES)), tab, tab, tab],
        out_specs=[grp(A_HEADS), grp(A_HEADS), vt_out(A_HEADS),
                   grp(B_HEADS), grp(B_HEADS), vt_out(B_HEADS)],
        out_shape=[sds((B, A_HEADS, S, LANES), BF16), sds((B, A_HEADS, S, LANES), BF16),
                   sds((B, nkb, A_HEADS * V_ROWS, TK), BF16),
                   sds((B, B_HEADS, S, LANES), BF16), sds((B, B_HEADS, S, LANES), BF16),
                   sds((B, nkb, B_HEADS * V_ROWS, TK), BF16)],
        compiler_params=_params("parallel", "parallel"),
        name="tokprep",
    )(u, gmat, gqa_t, gka_t, qfeat, kfeat, row2(g_cq[l]), wuq_p, gqb_p,
      row2(g_ckv[l]), wukv_k, wukv_v, gkb_p, rope_c, rope_s1, rope_s2)

    km, vm = pl.pallas_call(
        _memkv_kernel,
        grid=(B,),
        in_specs=[pl.BlockSpec((None, n_mem, D), lambda b: (b, 0, 0)), _full((1, D)),
                  _full((D, M_WIDTH)), _full((D, M_WIDTH)), _full((A_WIDTH, A_WIDTH)),
                  _full((1, M_WIDTH))],
        out_specs=[pl.BlockSpec((None, 2, n_mem, LANES), lambda b: (b, 0, 0, 0))] * 2,
        out_shape=[sds((B, 2, n_mem, LANES), BF16)] * 2,
        compiler_params=_params("parallel"),
        name="memkv",
    )(mem, row2(g_mem[l]), w_mk[l].astype(BF16), w_mv[l].astype(BF16), gmat, gkm_t)

    qtile = lambda w, j=0: pl.BlockSpec((None, TQ, w), lambda b, t: (b, t, j))
    seq = lambda n: pl.BlockSpec((None, n, S, LANES), lambda b, t: (b, 0, 0, 0))
    vt_in = lambda n: pl.BlockSpec((None, nkb, n * V_ROWS, TK), lambda b, t: (b, 0, 0, 0))
    heads = lambda n: pl.BlockSpec((None, n, TQ, LANES), lambda b, t: (b, 0, t, 0))
    flash_scratch = lambda n: [pltpu.VMEM((2, n, TK, TQ), F32), pltpu.VMEM((n, TK, TQ), BF16),
                               pltpu.VMEM((n, 1, TQ), F32), pltpu.VMEM((n, 1, TQ), F32),
                               pltpu.VMEM((n, V_ROWS, TQ), F32)]
    ya = pl.pallas_call(
        functools.partial(_dsa_kernel, topk=topk),
        grid=(B, S // TQ),
        in_specs=[pl.BlockSpec(memory_space=pltpu.SMEM),
                  qtile(IDX_HEADS * IDX_DIM, _C_QI // (IDX_HEADS * IDX_DIM)),
                  qtile(LANES, _C_WI // LANES),
                  pl.BlockSpec((None, S, LANES), lambda b, t: (b, 0, _C_KI // LANES)),
                  heads(A_HEADS), seq(A_HEADS), vt_in(A_HEADS), _full((TK, TK))],
        out_specs=qtile(A_WIDTH),
        out_shape=sds((B, S, A_WIDTH), BF16),
        scratch_shapes=[pltpu.VMEM((nkb, TK, TQ), F32),
                        pltpu.VMEM((IDX_HEADS, TQ, LANES), BF16),
                        pltpu.VMEM((1, TQ), F32)] + flash_scratch(A_HEADS),
        compiler_params=_params("parallel", "arbitrary"),
        name="dsa",
    )(slopes, u, u, u, qa, ka, vat, ltri)

    yb = pl.pallas_call(
        _mla_kernel,
        grid=(B, S // TQ),
        in_specs=[heads(B_HEADS), seq(B_HEADS), vt_in(B_HEADS)],
        out_specs=qtile(B_WIDTH),
        out_shape=sds((B, S, B_WIDTH), BF16),
        scratch_shapes=flash_scratch(B_HEADS),
        compiler_params=_params("parallel", "arbitrary"),
        name="mla",
    )(qb, kb, vbt)

    memkv = pl.BlockSpec((None, 2, n_mem, LANES), lambda b, t: (b, 0, 0, 0))
    return pl.pallas_call(
        _out_kernel,
        grid=(B, S // TM),
        in_specs=[tok(D), tok(A_WIDTH), tok(B_WIDTH), tok(D_MIX, _C_GATE // D_MIX),
                  tok(M_WIDTH, _C_QM // M_WIDTH), _full((A_WIDTH, A_WIDTH)), _full((1, M_WIDTH)),
                  memkv, memkv, _full((D, D))],
        out_specs=tok(D),
        out_shape=sds((B, S, D), x.dtype),
        compiler_params=_params("parallel", "parallel"),
        name="outproj",
    )(x, ya, yb, u, u, gmat, gqm_t, km, vm, w_out[l].astype(BF16))
```

```python
import functools
import math

import jax
import jax.numpy as jnp
from jax import lax
from jax.experimental import pallas as pl
from jax.experimental.pallas import tpu as pltpu

F32 = jnp.float32
BF16 = jnp.bfloat16

CHUNK = 64
EPS = 1e-6
HEAD_DIM = 64
A_HEADS = 6
IDX_HEADS = 8
IDX_DIM = 64
TOPK_MAX = 256
ALIBI_MAX = 8.0
B_HEADS = 6
B_NOPE = 64
B_ROPE = 32
B_V = 64
B_QK = B_NOPE + B_ROPE
Q_LORA = 256
KV_LORA = 128
ROPE_THETA = 10000.0
M_HEADS = 4
A_WIDTH = A_HEADS * HEAD_DIM
B_WIDTH = B_HEADS * B_V
M_WIDTH = M_HEADS * HEAD_DIM
D_MIX = A_WIDTH + B_WIDTH + M_WIDTH
LOG2E = math.log2(math.e)
N_ALIBI_PIECES = 3

LANES = 128
SUBLANES = 8
VMEM_LIMIT = 52 * 1024 * 1024

TM = 512
TN = 512
TQ = 256
TK = 256
N_BISECT = 20
V_ROWS = 80
NEG = -1e30

_C_QI = 0
_C_KI = _C_QI + IDX_HEADS * IDX_DIM
_C_WI = _C_KI + LANES
_C_QM = _C_WI + LANES
_C_GATE = _C_QM + M_WIDTH
_C_QA = _C_GATE + D_MIX
_C_KA = _C_QA + A_WIDTH
_C_VA = _C_KA + A_WIDTH
_C_CQ = _C_VA + A_WIDTH
_C_CKV = _C_CQ + Q_LORA
_C_KR = _C_CKV + KV_LORA
_C_KRP = _C_KR + LANES
_C_END = _C_KRP + LANES


def _nt_dot(a, b):
    return lax.dot_general(a, b, (((1,), (1,)), ((), ())), preferred_element_type=F32)


def _head64_norm(u, gmat_ref, g):
    n = u.shape[1]
    msq = jnp.dot((u * u).astype(BF16), gmat_ref[:n, :n], preferred_element_type=F32)
    return u * lax.rsqrt(msq + EPS) * g


def _inproj_kernel(x_ref, gin_ref, w_ref, u_ref):
    x = x_ref[...]
    ms = jnp.mean(x * x, axis=-1, keepdims=True)
    h = (x * lax.rsqrt(ms + EPS) * gin_ref[...]).astype(BF16)
    for a in range(0, _C_END, TN):
        b = min(a + TN, _C_END)
        u_ref[:, a:b] = jnp.dot(h, w_ref[:, a:b], preferred_element_type=F32).astype(BF16)


def _tokprep_kernel(u_ref, gmat_ref, place_ref, gqa_ref, gka_ref, qfeat_ref, kfeat_ref, gcq_ref,
                    wuq_ref, gckv_ref, wukvk_ref, wukvv_ref, qa_tab_ref, qb_tab_ref, ka_tab_ref,
                    kb_tab_ref, qa_ref, ka_ref, vat_ref, qb_ref, kb_ref, vbt_ref):
    def cols(a, b):
        return u_ref[:, a:b].astype(F32)

    ones_rows = (lax.broadcasted_iota(jnp.int32, (V_ROWS - HEAD_DIM, TK), 0) == 0).astype(BF16)

    def store_transposed(dst_ref, v):
        vt = v.T.astype(BF16)
        for j in range(TM // TK):
            for hd in range(v.shape[1] // HEAD_DIM):
                r = hd * V_ROWS
                dst_ref[j, r:r + HEAD_DIM, :] = vt[hd * HEAD_DIM:(hd + 1) * HEAD_DIM,
                                                   j * TK:(j + 1) * TK]
                dst_ref[j, r + HEAD_DIM:r + V_ROWS, :] = ones_rows

    def store_heads(dst_ref, v, feat_fn):
        placed = jnp.dot(v.astype(BF16), place_ref[...], preferred_element_type=F32)
        for hd in range(A_HEADS):
            dst_ref[hd] = (placed[:, hd * LANES:(hd + 1) * LANES] + feat_fn(hd)).astype(BF16)

    store_heads(qa_ref, _head64_norm(cols(_C_QA, _C_KA), gmat_ref, gqa_ref[...]),
                lambda hd: qfeat_ref[hd:hd + 1, :])
    kfeat = kfeat_ref[...]
    store_heads(ka_ref, _head64_norm(cols(_C_KA, _C_VA), gmat_ref, gka_ref[...]), lambda hd: kfeat)
    store_transposed(vat_ref, cols(_C_VA, _C_CQ))

    def head96(uh, partner, tab_a, tab_b):
        ss = jnp.sum(uh * uh, axis=-1, keepdims=True) * (1.0 / B_QK)
        return ((uh * tab_a + partner * tab_b) * lax.rsqrt(ss + EPS)).astype(BF16)

    cq = cols(_C_CQ, _C_CKV)
    cq = cq * lax.rsqrt(jnp.mean(cq * cq, axis=-1, keepdims=True) + EPS) * gcq_ref[...]
    q = jnp.dot(cq.astype(BF16), wuq_ref[...], preferred_element_type=F32)
    qa_tab = qa_tab_ref[...]
    qb_tab = qb_tab_ref[...]
    for hd in range(B_HEADS):
        qb_ref[hd] = head96(q[:, hd * LANES:(hd + 1) * LANES],
                            q[:, (B_HEADS + hd) * LANES:(B_HEADS + hd + 1) * LANES], qa_tab, qb_tab)

    ckv = cols(_C_CKV, _C_KR)
    ckv = ckv * lax.rsqrt(jnp.mean(ckv * ckv, axis=-1, keepdims=True) + EPS) * gckv_ref[...]
    ckv = ckv.astype(BF16)
    kk = jnp.dot(ckv, wukvk_ref[...], preferred_element_type=F32)
    store_transposed(vbt_ref, jnp.dot(ckv, wukvv_ref[...], preferred_element_type=F32))
    krope = cols(_C_KR, _C_KRP)
    krope_partner = cols(_C_KRP, _C_END)
    ka_tab = ka_tab_ref[...]
    kb_tab = kb_tab_ref[...]
    for hd in range(B_HEADS):
        kb_ref[hd] = head96(kk[:, hd * LANES:(hd + 1) * LANES] + krope, krope_partner, ka_tab, kb_tab)


def _memkv_kernel(mem_ref, gmem_ref, wmk_ref, wmv_ref, gmat_ref, gkm_ref, km_ref, vm_ref):
    x = mem_ref[...]
    ms = jnp.mean(x * x, axis=-1, keepdims=True)
    m = (x * lax.rsqrt(ms + EPS) * gmem_ref[...]).astype(BF16)
    k = _head64_norm(jnp.dot(m, wmk_ref[...], preferred_element_type=F32), gmat_ref,
                     gkm_ref[...]).astype(BF16)
    v = jnp.dot(m, wmv_ref[...], preferred_element_type=F32).astype(BF16)
    for g in range(M_WIDTH // LANES):
        km_ref[g] = k[:, g * LANES:(g + 1) * LANES]
        vm_ref[g] = v[:, g * LANES:(g + 1) * LANES]


def _flash_init(m_ref, acc_ref):
    m_ref[...] = jnp.full(m_ref.shape, NEG, F32)
    acc_ref[...] = jnp.zeros(acc_ref.shape, F32)


def _flash_softmax_pv(n_heads, s_ref, vt_fn, m_ref, acc_ref):
    for hd in range(n_heads):
        s = s_ref[hd]
        m_prev = m_ref[hd]
        m_new = jnp.maximum(m_prev, jnp.max(s, axis=0, keepdims=True))
        p = jnp.exp2(s - m_new).astype(BF16)
        alpha = jnp.exp2(m_prev - m_new)
        m_ref[hd] = m_new
        acc_ref[hd] = alpha * acc_ref[hd] + jnp.dot(vt_fn(hd), p, preferred_element_type=F32)


def _flash_loop(i, diag_logits_fn, logits_fn, update_fn):
    def block(t):
        return jnp.minimum(t, i) - 1

    def consumed(t):
        return jnp.where(t == 0, i, t - 1)

    diag_logits_fn(0)

    def pair(j, carry):
        t = 2 * j
        logits_fn(block(t + 1), 1)
        update_fn(consumed(t), 0)
        logits_fn(block(t + 2), 0)
        update_fn(t, 1)
        return carry

    lax.fori_loop(0, (i + 1) // 2, pair, 0)

    @pl.when(i % 2 == 0)
    def _():
        update_fn(consumed(i), 0)


def _flash_store(out_ref, acc_ref, n_heads):
    def head(hd):
        return acc_ref[hd, :HEAD_DIM, :] / acc_ref[hd, HEAD_DIM:HEAD_DIM + 1, :]

    for g in range(n_heads // 2):
        o = jnp.concatenate([head(2 * g), head(2 * g + 1)], axis=0)
        out_ref[:, g * LANES:(g + 1) * LANES] = o.T.astype(out_ref.dtype)


def _split_head_pairs(src_ref, dst_ref, n_pairs):
    low = lax.broadcasted_iota(jnp.int32, (src_ref.shape[0], LANES), 1) < HEAD_DIM
    for g in range(n_pairs):
        pair = src_ref[:, g * LANES:(g + 1) * LANES].astype(F32)
        dst_ref[2 * g] = jnp.where(low, pair, 0.0).astype(dst_ref.dtype)
        dst_ref[2 * g + 1] = jnp.where(low, 0.0, pair).astype(dst_ref.dtype)


def _fold8(x):
    return x.reshape(x.shape[0] // SUBLANES, SUBLANES, x.shape[1])


def _dsa_kernel(slopes_ref, qi_ref, wi_ref, ki_ref, qa_ref, ka_ref, vat_ref, ltri_ref,
                out_ref,
                score_ref, qim_ref, lo_ref, mn_ref, mx_ref, s_ref, m_ref, acc_ref,
                *, topk):
    i = pl.program_id(1)
    nkb = i + 1
    npair = (nkb + 1) // 2
    q0 = i * TQ

    _split_head_pairs(qi_ref, qim_ref, IDX_HEADS // 2)

    kidx = lax.broadcasted_iota(jnp.int32, (TK, TQ), 0)
    qidx = lax.broadcasted_iota(jnp.int32, (TK, TQ), 1)

    wt = wi_ref[...].astype(F32).T

    def score_block(kb, diagonal):
        k0 = pl.multiple_of(kb * TK, TK)
        kblk = ki_ref[pl.ds(k0, TK), :]
        acc = jnp.zeros((TK, TQ), F32)
        for hd in range(IDX_HEADS):
            acc = acc + jnp.maximum(_nt_dot(kblk, qim_ref[hd]), 0.0) * wt[hd:hd + 1, :]
        hi_part = lo_part = _fold8(acc)
        if diagonal:
            adm = (kidx >> 6) <= (qidx >> 6)
            acc = jnp.where(adm, acc, -jnp.inf)
            hi_part = _fold8(acc)
            lo_part = _fold8(jnp.where(adm, acc, jnp.inf))
        score_ref[kb] = acc
        mn_ref[...] = jnp.minimum(mn_ref[...], jnp.min(lo_part, axis=0))
        mx_ref[...] = jnp.maximum(mx_ref[...], jnp.max(hi_part, axis=0))

    mn_ref[...] = jnp.full((SUBLANES, TQ), jnp.inf, F32)
    mx_ref[...] = jnp.full((SUBLANES, TQ), -jnp.inf, F32)

    def score_pair(j, carry):
        score_block(2 * j, False)
        score_block(2 * j + 1, False)
        return carry

    lax.fori_loop(0, i // 2, score_pair, 0)

    @pl.when(i % 2 == 1)
    def _():
        score_block(i - 1, False)

    score_block(i, True)

    @pl.when(nkb % 2 == 1)
    def _():
        score_ref[nkb] = jnp.full((TK, TQ), -jnp.inf, F32)

    qrow = q0 + lax.broadcasted_iota(jnp.int32, (1, TQ), 1)
    n_adm = ((qrow >> 6) + 1) << 6
    k_eff = jnp.minimum(n_adm, topk).astype(F32)

    lo = jnp.min(mn_ref[...], axis=0, keepdims=True)
    hi = jnp.max(mx_ref[...], axis=0, keepdims=True)

    def count_ge(t):
        def body(j, acc):
            a0, a1 = acc
            a0 = a0 + jnp.sum(jnp.where(_fold8(score_ref[2 * j]) >= t, 1.0, 0.0), axis=0)
            a1 = a1 + jnp.sum(jnp.where(_fold8(score_ref[2 * j + 1]) >= t, 1.0, 0.0), axis=0)
            return a0, a1

        z = jnp.zeros((SUBLANES, TQ), F32)
        a0, a1 = lax.fori_loop(0, npair, body, (z, z))
        return jnp.sum(a0 + a1, axis=0, keepdims=True)

    def bisect(_, c):
        lo, hi, c_lo = c
        mid = 0.5 * (lo + hi)
        cnt = count_ge(mid)
        ge = cnt >= k_eff
        return jnp.where(ge, mid, lo), jnp.where(ge, hi, mid), jnp.where(ge, cnt, c_lo)

    lo, hi, c_lo = lax.fori_loop(0, N_BISECT, bisect, (lo, hi, n_adm.astype(F32)))
    lo_ref[...] = lo

    for c in range(TQ // LANES):
        cs = slice(c * LANES, (c + 1) * LANES)

        @pl.when(jnp.max(c_lo[:, cs] - k_eff[:, cs]) > 0.0)
        def _():
            k_c = k_eff[:, cs]

            def counts(t):
                def body(kb, c2):
                    s = _fold8(score_ref[kb, :, cs])
                    return (c2[0] + jnp.sum(jnp.where(s >= t, 1.0, 0.0), axis=0),
                            c2[1] + jnp.sum(jnp.where(s > t, 1.0, 0.0), axis=0))

                z = jnp.zeros((SUBLANES, LANES), F32)
                ge, gt = lax.fori_loop(0, nkb, body, (z, z))
                return jnp.sum(ge, axis=0, keepdims=True), jnp.sum(gt, axis=0, keepdims=True)

            def largest(keep):
                def body(kb, acc):
                    s = _fold8(score_ref[kb, :, cs])
                    return jnp.maximum(acc, jnp.max(jnp.where(keep(s), s, -jnp.inf), axis=0))

                acc = lax.fori_loop(0, nkb, body, jnp.full((SUBLANES, LANES), -jnp.inf, F32))
                return jnp.max(acc, axis=0, keepdims=True)

            def pending(ge):
                return jnp.sum(jnp.where(ge >= k_c, 0.0, 1.0))

            def walk(c3):
                t, ge, _, _ = c3
                t = jnp.where(ge >= k_c, t, largest(lambda s: s < t))
                ge, gt = counts(t)
                return t, ge, gt, pending(ge)

            hi_c = hi[:, cs]
            t0 = largest(lambda s: s <= hi_c)
            ge0, gt0 = counts(t0)
            thr, _, c_gt, _ = lax.while_loop(lambda c3: c3[3] > 0.0, walk,
                                             (t0, ge0, gt0, pending(ge0)))
            need = k_c - c_gt

            def mark(kb, seen):
                s = score_ref[kb, :, cs]
                eqf = jnp.where(s == thr, 1.0, 0.0)
                rank = seen + jnp.dot(ltri_ref[...], eqf.astype(BF16), preferred_element_type=F32)
                tie = jnp.where(rank <= need, eqf, 0.0)
                score_ref[kb, :, cs] = jnp.where(s > thr, 1.0, tie) * 2.0 - 1.0
                return seen + jnp.sum(eqf, axis=0, keepdims=True)

            lax.fori_loop(0, nkb, mark, jnp.zeros((1, LANES), F32))
            lo_ref[:, cs] = jnp.zeros((1, LANES), F32)

    _flash_init(m_ref, acc_ref)
    lo_row = lo_ref[...]

    def logits(kb, buf, future=None):
        k0 = pl.multiple_of(kb * TK, TK)
        bias = jnp.where(score_ref[kb] >= lo_row, 0.0, NEG)
        for hd in range(A_HEADS):
            s = _nt_dot(ka_ref[hd, pl.ds(k0, TK), :], qa_ref[hd])
            s_ref[buf, hd] = s + (bias if future is None else bias - slopes_ref[hd] * future)

    def update(kb, buf):
        _flash_softmax_pv(A_HEADS, s_ref.at[buf],
                          lambda hd: vat_ref[kb, hd * V_ROWS:(hd + 1) * V_ROWS, :],
                          m_ref, acc_ref)

    _flash_loop(i, lambda buf: logits(i, buf, 2.0 * jnp.maximum(kidx - qidx, 0).astype(F32)),
                logits, update)
    _flash_store(out_ref, acc_ref, A_HEADS)


def _mla_kernel(qb_ref, kb_ref, vbt_ref, out_ref, s_ref, m_ref, acc_ref):
    i = pl.program_id(1)
    _flash_init(m_ref, acc_ref)

    def logits(kb, buf, bias=None):
        k0 = pl.multiple_of(kb * TK, TK)
        for hd in range(B_HEADS):
            s = _nt_dot(kb_ref[hd, pl.ds(k0, TK), :], qb_ref[hd])
            s_ref[buf, hd] = s if bias is None else s + bias

    def update(kb, buf):
        _flash_softmax_pv(B_HEADS, s_ref.at[buf],
                          lambda hd: vbt_ref[kb, hd * V_ROWS:(hd + 1) * V_ROWS, :],
                          m_ref, acc_ref)

    kidx = lax.broadcasted_iota(jnp.int32, (TK, TQ), 0)
    qidx = lax.broadcasted_iota(jnp.int32, (TK, TQ), 1)
    _flash_loop(i, lambda buf: logits(i, buf, jnp.where((kidx >> 6) <= (qidx >> 6), 0.0, NEG)),
                logits, update)
    _flash_store(out_ref, acc_ref, B_HEADS)


def _out_kernel(x_ref, ya_ref, yb_ref, z_ref, qm_ref, gmat_ref, gqm_ref, km_ref, vm_ref, wout_ref,
                out_ref):
    tm = x_ref.shape[0]
    low = lax.broadcasted_iota(jnp.int32, (tm, LANES), 1) < HEAD_DIM
    qm = _head64_norm(qm_ref[...].astype(F32), gmat_ref, gqm_ref[...])
    ym = []
    for g in range(M_HEADS // 2):
        pair = qm[:, g * LANES:(g + 1) * LANES]
        outs = []
        for q in (jnp.where(low, pair, 0.0), jnp.where(low, 0.0, pair)):
            s = _nt_dot(q.astype(BF16), km_ref[g])
            p = jnp.exp2(s - jnp.max(s, axis=1, keepdims=True))
            o = jnp.dot(p.astype(BF16), vm_ref[g], preferred_element_type=F32)
            outs.append(o / jnp.sum(p, axis=1, keepdims=True))
        ym.append(jnp.where(low, outs[0], outs[1]))
    y = jnp.concatenate([ya_ref[...].astype(F32), yb_ref[...].astype(F32)] + ym, axis=1)
    z = z_ref[...].astype(F32)
    y = (y * (z / (1.0 + jnp.exp(-z)))).astype(BF16)
    out_ref[...] = x_ref[...] + jnp.dot(y, wout_ref[...], preferred_element_type=F32)


def _full(shape):
    n = len(shape)
    return pl.BlockSpec(shape, lambda *_: (0,) * n)


def _params(*sem):
    return pltpu.CompilerParams(dimension_semantics=sem, vmem_limit_bytes=VMEM_LIMIT)


def _bf16_pieces(v, n):
    out, rest = [], v
    for _ in range(n):
        piece = rest.astype(BF16).astype(F32)
        out.append(piece)
        rest = rest - piece
    return jnp.stack(out, axis=-1)


def kernel(x, mem, g_in, w_in, w_uq, g_cq, w_ukv, g_ckv, g_qa, g_ka, g_qb, g_kb,
           g_mem, w_mk, w_mv, g_qm, g_km, w_out):
    B, S, D = x.shape
    n_mem = mem.shape[1]
    assert S % TM == 0 and TM % TK == 0 and TQ == TK and D == D_MIX and (S // TK) % 2 == 0
    assert _C_QI == 0 and _C_KI % LANES == 0 and _C_WI % LANES == 0
    assert _C_QM % M_WIDTH == 0 and _C_GATE % D_MIX == 0
    topk = min(TOPK_MAX, S // 4)
    nkb = S // TK
    l = 0
    half = B_ROPE // 2

    wi = w_in[l]
    o = [0]
    for n in (A_WIDTH, A_WIDTH, A_WIDTH, IDX_HEADS * IDX_DIM, IDX_DIM, IDX_HEADS, A_WIDTH,
              Q_LORA, KV_LORA, B_ROPE, B_WIDTH, M_WIDTH, M_WIDTH):
        o.append(o[-1] + n)
    (c_qa, c_ka, c_va, c_qi, c_ki, c_wi, c_za, c_cq, c_ckv, c_kr, c_zb, c_qm, c_zm) = [
        wi[:, o[j]:o[j + 1]] for j in range(13)]
    zeros = lambda n: jnp.zeros((D, n), wi.dtype)
    w_pack = jnp.concatenate([
        c_qi,
        c_ki, c_ki,
        c_wi, zeros(LANES - IDX_HEADS),
        c_qm,
        c_za, c_zb, c_zm,
        c_qa, c_ka, c_va,
        c_cq, c_ckv,
        zeros(B_NOPE), c_kr, zeros(LANES - B_QK),
        zeros(B_NOPE), c_kr[:, half:], c_kr[:, :half], zeros(LANES - B_QK),
    ], axis=1).astype(BF16)
    assert w_pack.shape[1] == _C_END

    def rope_partner(v):
        z = jnp.zeros_like(v[..., :B_NOPE])
        return jnp.concatenate([z, v[..., B_NOPE + half:B_QK], v[..., B_NOPE:B_NOPE + half],
                                jnp.zeros_like(v[..., B_QK:])], axis=-1)

    wuq_h = jnp.pad(w_uq[l].reshape(Q_LORA, B_HEADS, B_QK), ((0, 0), (0, 0), (0, LANES - B_QK)))
    wuq_p = jnp.concatenate([wuq_h.reshape(Q_LORA, B_HEADS * LANES),
                             rope_partner(wuq_h).reshape(Q_LORA, B_HEADS * LANES)],
                            axis=1).astype(BF16)
    wukv = w_ukv[l].reshape(KV_LORA, B_HEADS, B_NOPE + B_V)
    wukv_k = jnp.pad(wukv[:, :, :B_NOPE], ((0, 0), (0, 0), (0, LANES - B_NOPE))
                     ).reshape(KV_LORA, B_HEADS * LANES).astype(BF16)
    wukv_v = wukv[:, :, B_NOPE:].reshape(KV_LORA, B_WIDTH).astype(BF16)

    idx = jnp.arange(A_WIDTH)
    gmat = jnp.where((idx[:, None] // HEAD_DIM) == (idx[None, :] // HEAD_DIM),
                     1.0 / HEAD_DIM, 0.0).astype(BF16)
    row2 = lambda v: v.reshape(1, -1).astype(F32)
    gqa_t = row2(jnp.tile(g_qa[l], A_HEADS) * (HEAD_DIM ** -0.5 * LOG2E))
    gka_t = row2(jnp.tile(g_ka[l], A_HEADS))
    gqm_t = row2(jnp.tile(g_qm[l], M_HEADS) * (HEAD_DIM ** -0.5 * LOG2E))
    gkm_t = row2(jnp.tile(g_km[l], M_HEADS))
    gqb_p = row2(jnp.pad(g_qb[l], (0, LANES - B_QK)) * (B_QK ** -0.5 * LOG2E))
    gkb_p = row2(jnp.pad(g_kb[l], (0, LANES - B_QK)))

    pos = jnp.arange(S, dtype=jnp.int32)
    inv = 1.0 / (ROPE_THETA ** (jnp.arange(half, dtype=F32) / half))
    ang = pos.astype(F32)[:, None] * inv[None, :]
    cos, sin = jnp.cos(ang), jnp.sin(ang)
    zs = lambda n: jnp.zeros((S, n), F32)
    rope_c = jnp.concatenate([jnp.ones((S, B_NOPE), F32), cos, cos, jnp.ones((S, LANES - B_QK), F32)], 1)
    rope_s = jnp.concatenate([zs(B_NOPE), -sin, sin, zs(LANES - B_QK)], 1)
    qa_tab, qb_tab = rope_c * gqb_p, rope_s * rope_partner(gqb_p)
    ka_tab, kb_tab = rope_c * gkb_p, rope_s * rope_partner(gkb_p)
    src = jnp.arange(A_WIDTH)
    place = (src[:, None] // HEAD_DIM * LANES + src[:, None] % HEAD_DIM
             == jnp.arange(A_HEADS * LANES)[None, :]).astype(BF16)

    slopes = 2.0 ** (-ALIBI_MAX * jnp.arange(1, A_HEADS + 1, dtype=F32) / A_HEADS) * LOG2E
    pieces = _bf16_pieces(slopes, N_ALIBI_PIECES)
    n_feat = 2 * N_ALIBI_PIECES
    qfeat = jnp.concatenate([jnp.zeros((A_HEADS, HEAD_DIM), F32), pieces * float(CHUNK), pieces,
                             jnp.zeros((A_HEADS, LANES - HEAD_DIM - n_feat), F32)], axis=1)
    pos_hi = jnp.repeat((pos // CHUNK).astype(F32)[:, None], N_ALIBI_PIECES, axis=1)
    pos_lo = jnp.repeat((pos % CHUNK).astype(F32)[:, None], N_ALIBI_PIECES, axis=1)
    kfeat = jnp.concatenate([zs(HEAD_DIM), pos_hi, pos_lo, zs(LANES - HEAD_DIM - n_feat)], axis=1)

    kk = jnp.arange(TK)
    ltri = (kk[None, :] <= kk[:, None]).astype(BF16)

    sds = jax.ShapeDtypeStruct
    tok = lambda w, j=0: pl.BlockSpec((None, TM, w), lambda b, t: (b, t, j))
    u = pl.pallas_call(
        _inproj_kernel,
        grid=(B, S // TM),
        in_specs=[tok(D), _full((1, D)), _full((D, _C_END))],
        out_specs=tok(_C_END),
        out_shape=sds((B, S, _C_END), BF16),
        compiler_params=_params("parallel", "parallel"),
        name="inproj",
    )(x, row2(g_in[l]), w_pack)

    grp = lambda n: pl.BlockSpec((None, n, TM, LANES), lambda b, t: (b, 0, t, 0))
    vt_out = lambda n: pl.BlockSpec((None, TM // TK, n * V_ROWS, TK), lambda b, t: (b, t, 0, 0))
    tab = pl.BlockSpec((TM, LANES), lambda b, t: (t, 0))
    qa, ka, vat, qb, kb, vbt = pl.pallas_call(
        _tokprep_kernel,
        grid=(B, S // TM),
        in_specs=[tok(_C_END), _full((A_WIDTH, A_WIDTH)), _full((A_WIDTH, A_HEADS * LANES)),
                  _full((1, A_WIDTH)), _full((1, A_WIDTH)), _full((A_HEADS, LANES)), tab,
                  _full((1, Q_LORA)), _full((Q_LORA, 2 * B_HEADS * LANES)), _full((1, KV_LORA)),
                  _full((KV_LORA, B_HEADS * LANES)), _full((KV_LORA, B_WIDTH)),
                  tab, tab, tab, tab],
        out_specs=[grp(A_HEADS), grp(A_HEADS), vt_out(A_HEADS),
                   grp(B_HEADS), grp(B_HEADS), vt_out(B_HEADS)],
        out_shape=[sds((B, A_HEADS, S, LANES), BF16), sds((B, A_HEADS, S, LANES), BF16),
                   sds((B, nkb, A_HEADS * V_ROWS, TK), BF16),
                   sds((B, B_HEADS, S, LANES), BF16), sds((B, B_HEADS, S, LANES), BF16),
                   sds((B, nkb, B_HEADS * V_ROWS, TK), BF16)],
        compiler_params=_params("parallel", "parallel"),
        name="tokprep",
    )(u, gmat, place, gqa_t, gka_t, qfeat, kfeat, row2(g_cq[l]), wuq_p,
      row2(g_ckv[l]), wukv_k, wukv_v, qa_tab, qb_tab, ka_tab, kb_tab)

    km, vm = pl.pallas_call(
        _memkv_kernel,
        grid=(B,),
        in_specs=[pl.BlockSpec((None, n_mem, D), lambda b: (b, 0, 0)), _full((1, D)),
                  _full((D, M_WIDTH)), _full((D, M_WIDTH)), _full((A_WIDTH, A_WIDTH)),
                  _full((1, M_WIDTH))],
        out_specs=[pl.BlockSpec((None, 2, n_mem, LANES), lambda b: (b, 0, 0, 0))] * 2,
        out_shape=[sds((B, 2, n_mem, LANES), BF16)] * 2,
        compiler_params=_params("parallel"),
        name="memkv",
    )(mem, row2(g_mem[l]), w_mk[l].astype(BF16), w_mv[l].astype(BF16), gmat, gkm_t)

    qtile = lambda w, j=0: pl.BlockSpec((None, TQ, w), lambda b, t: (b, t, j))
    seq = lambda n: pl.BlockSpec((None, n, S, LANES), lambda b, t: (b, 0, 0, 0))
    vt_in = lambda n: pl.BlockSpec((None, nkb, n * V_ROWS, TK), lambda b, t: (b, 0, 0, 0))
    heads = lambda n: pl.BlockSpec((None, n, TQ, LANES), lambda b, t: (b, 0, t, 0))
    flash_scratch = lambda n: [pltpu.VMEM((2, n, TK, TQ), F32), pltpu.VMEM((n, 1, TQ), F32),
                               pltpu.VMEM((n, V_ROWS, TQ), F32)]
    ya = pl.pallas_call(
        functools.partial(_dsa_kernel, topk=topk),
        grid=(B, S // TQ),
        in_specs=[pl.BlockSpec(memory_space=pltpu.SMEM),
                  qtile(IDX_HEADS * IDX_DIM, _C_QI // (IDX_HEADS * IDX_DIM)),
                  qtile(LANES, _C_WI // LANES),
                  pl.BlockSpec((None, S, LANES), lambda b, t: (b, 0, _C_KI // LANES)),
                  heads(A_HEADS), seq(A_HEADS), vt_in(A_HEADS), _full((TK, TK))],
        out_specs=qtile(A_WIDTH),
        out_shape=sds((B, S, A_WIDTH), BF16),
        scratch_shapes=[pltpu.VMEM((nkb, TK, TQ), F32),
                        pltpu.VMEM((IDX_HEADS, TQ, LANES), BF16),
                        pltpu.VMEM((1, TQ), F32), pltpu.VMEM((SUBLANES, TQ), F32),
                        pltpu.VMEM((SUBLANES, TQ), F32)] + flash_scratch(A_HEADS),
        compiler_params=_params("parallel", "arbitrary"),
        name="dsa",
    )(slopes, u, u, u, qa, ka, vat, ltri)

    yb = pl.pallas_call(
        _mla_kernel,
        grid=(B, S // TQ),
        in_specs=[heads(B_HEADS), seq(B_HEADS), vt_in(B_HEADS)],
        out_specs=qtile(B_WIDTH),
        out_shape=sds((B, S, B_WIDTH), BF16),
        scratch_shapes=flash_scratch(B_HEADS),
        compiler_params=_params("parallel", "arbitrary"),
        name="mla",
    )(qb, kb, vbt)

    memkv = pl.BlockSpec((None, 2, n_mem, LANES), lambda b, t: (b, 0, 0, 0))
    return pl.pallas_call(
        _out_kernel,
        grid=(B, S // TM),
        in_specs=[tok(D), tok(A_WIDTH), tok(B_WIDTH), tok(D_MIX, _C_GATE // D_MIX),
                  tok(M_WIDTH, _C_QM // M_WIDTH), _full((A_WIDTH, A_WIDTH)), _full((1, M_WIDTH)),
                  memkv, memkv, _full((D, D))],
        out_specs=tok(D),
        out_shape=sds((B, S, D), x.dtype),
        compiler_params=_params("parallel", "parallel"),
        name="outproj",
    )(x, ya, yb, u, u, gmat, gqm_t, km, vm, w_out[l].astype(BF16))
```

```python
import functools
import math

import jax
import jax.numpy as jnp
from jax import lax
from jax.experimental import pallas as pl
from jax.experimental.pallas import tpu as pltpu

F32 = jnp.float32
BF16 = jnp.bfloat16

CHUNK = 64
EPS = 1e-6
HEAD_DIM = 64
A_HEADS = 6
IDX_HEADS = 8
IDX_DIM = 64
TOPK_MAX = 256
ALIBI_MAX = 8.0
B_HEADS = 6
B_NOPE = 64
B_ROPE = 32
B_V = 64
B_QK = B_NOPE + B_ROPE
Q_LORA = 256
KV_LORA = 128
ROPE_THETA = 10000.0
M_HEADS = 4
A_WIDTH = A_HEADS * HEAD_DIM
B_WIDTH = B_HEADS * B_V
M_WIDTH = M_HEADS * HEAD_DIM
D_MIX = A_WIDTH + B_WIDTH + M_WIDTH
LOG2E = math.log2(math.e)
N_ALIBI_PIECES = 3

LANES = 128
SUBLANES = 8
VMEM_LIMIT = 52 * 1024 * 1024

TM = 512
TN = 512
TQ = 256
TK = 256
N_BISECT = 20
V_ROWS = 80
NEG = -1e30

_C_QI = 0
_C_KI = _C_QI + IDX_HEADS * IDX_DIM
_C_WI = _C_KI + LANES
_C_QM = _C_WI + LANES
_C_GATE = _C_QM + M_WIDTH
_C_QA = _C_GATE + D_MIX
_C_KA = _C_QA + A_WIDTH
_C_VA = _C_KA + A_WIDTH
_C_CQ = _C_VA + A_WIDTH
_C_CKV = _C_CQ + Q_LORA
_C_KR = _C_CKV + KV_LORA
_C_KRP = _C_KR + LANES
_C_END = _C_KRP + LANES


def _nt_dot(a, b):
    return lax.dot_general(a, b, (((1,), (1,)), ((), ())), preferred_element_type=F32)


def _head64_norm(u, gmat_ref, g):
    n = u.shape[1]
    msq = jnp.dot((u * u).astype(BF16), gmat_ref[:n, :n], preferred_element_type=F32)
    return u * lax.rsqrt(msq + EPS) * g


def _inproj_kernel(x_ref, gin_ref, w_ref, u_ref):
    x = x_ref[...]
    ms = jnp.mean(x * x, axis=-1, keepdims=True)
    h = (x * lax.rsqrt(ms + EPS) * gin_ref[...]).astype(BF16)
    for a in range(0, _C_END, TN):
        b = min(a + TN, _C_END)
        u_ref[:, a:b] = jnp.dot(h, w_ref[:, a:b], preferred_element_type=F32).astype(BF16)


def _tokprep_kernel(u_ref, gmat_ref, place_ref, gqa_ref, gka_ref, qfeat_ref, kfeat_ref, gcq_ref,
                    wuq_ref, gckv_ref, wukvk_ref, wukvv_ref, qa_tab_ref, qb_tab_ref, ka_tab_ref,
                    kb_tab_ref, qa_ref, ka_ref, vat_ref, qb_ref, kb_ref, vbt_ref):
    def cols(a, b):
        return u_ref[:, a:b].astype(F32)

    ones_rows = (lax.broadcasted_iota(jnp.int32, (V_ROWS - HEAD_DIM, TK), 0) == 0).astype(BF16)

    def store_transposed(dst_ref, v):
        vt = v.T.astype(BF16)
        for j in range(TM // TK):
            for hd in range(v.shape[1] // HEAD_DIM):
                r = hd * V_ROWS
                dst_ref[j, r:r + HEAD_DIM, :] = vt[hd * HEAD_DIM:(hd + 1) * HEAD_DIM,
                                                   j * TK:(j + 1) * TK]
                dst_ref[j, r + HEAD_DIM:r + V_ROWS, :] = ones_rows

    def store_heads(dst_ref, v, feat_fn):
        placed = jnp.dot(v.astype(BF16), place_ref[...], preferred_element_type=F32)
        for hd in range(A_HEADS):
            dst_ref[hd] = (placed[:, hd * LANES:(hd + 1) * LANES] + feat_fn(hd)).astype(BF16)

    store_heads(qa_ref, _head64_norm(cols(_C_QA, _C_KA), gmat_ref, gqa_ref[...]),
                lambda hd: qfeat_ref[hd:hd + 1, :])
    kfeat = kfeat_ref[...]
    store_heads(ka_ref, _head64_norm(cols(_C_KA, _C_VA), gmat_ref, gka_ref[...]), lambda hd: kfeat)
    store_transposed(vat_ref, cols(_C_VA, _C_CQ))

    def head96(uh, partner, tab_a, tab_b):
        ss = jnp.sum(uh * uh, axis=-1, keepdims=True) * (1.0 / B_QK)
        return ((uh * tab_a + partner * tab_b) * lax.rsqrt(ss + EPS)).astype(BF16)

    cq = cols(_C_CQ, _C_CKV)
    cq = cq * lax.rsqrt(jnp.mean(cq * cq, axis=-1, keepdims=True) + EPS) * gcq_ref[...]
    q = jnp.dot(cq.astype(BF16), wuq_ref[...], preferred_element_type=F32)
    qa_tab = qa_tab_ref[...]
    qb_tab = qb_tab_ref[...]
    for hd in range(B_HEADS):
        qb_ref[hd] = head96(q[:, hd * LANES:(hd + 1) * LANES],
                            q[:, (B_HEADS + hd) * LANES:(B_HEADS + hd + 1) * LANES], qa_tab, qb_tab)

    ckv = cols(_C_CKV, _C_KR)
    ckv = ckv * lax.rsqrt(jnp.mean(ckv * ckv, axis=-1, keepdims=True) + EPS) * gckv_ref[...]
    ckv = ckv.astype(BF16)
    kk = jnp.dot(ckv, wukvk_ref[...], preferred_element_type=F32)
    store_transposed(vbt_ref, jnp.dot(ckv, wukvv_ref[...], preferred_element_type=F32))
    krope = cols(_C_KR, _C_KRP)
    krope_partner = cols(_C_KRP, _C_END)
    ka_tab = ka_tab_ref[...]
    kb_tab = kb_tab_ref[...]
    for hd in range(B_HEADS):
        kb_ref[hd] = head96(kk[:, hd * LANES:(hd + 1) * LANES] + krope, krope_partner, ka_tab, kb_tab)


def _memkv_kernel(mem_ref, gmem_ref, wmk_ref, wmv_ref, gmat_ref, gkm_ref, km_ref, vm_ref):
    x = mem_ref[...]
    ms = jnp.mean(x * x, axis=-1, keepdims=True)
    m = (x * lax.rsqrt(ms + EPS) * gmem_ref[...]).astype(BF16)
    k = _head64_norm(jnp.dot(m, wmk_ref[...], preferred_element_type=F32), gmat_ref,
                     gkm_ref[...]).astype(BF16)
    v = jnp.dot(m, wmv_ref[...], preferred_element_type=F32).astype(BF16)
    for g in range(M_WIDTH // LANES):
        km_ref[g] = k[:, g * LANES:(g + 1) * LANES]
        vm_ref[g] = v[:, g * LANES:(g + 1) * LANES]


def _flash_init(m_ref, acc_ref):
    m_ref[...] = jnp.full(m_ref.shape, NEG, F32)
    acc_ref[...] = jnp.zeros(acc_ref.shape, F32)


def _flash_softmax_pv(n_heads, s_ref, vt_fn, m_ref, acc_ref):
    for hd in range(n_heads):
        s = s_ref[hd]
        m_prev = m_ref[hd]
        m_new = jnp.maximum(m_prev, jnp.max(s, axis=0, keepdims=True))
        p = jnp.exp2(s - m_new).astype(BF16)
        alpha = jnp.exp2(m_prev - m_new)
        m_ref[hd] = m_new
        acc_ref[hd] = alpha * acc_ref[hd] + jnp.dot(vt_fn(hd), p, preferred_element_type=F32)


def _flash_loop(i, diag_logits_fn, logits_fn, update_fn):
    def block(t):
        return jnp.minimum(t, i) - 1

    def consumed(t):
        return jnp.where(t == 0, i, t - 1)

    diag_logits_fn(0)

    def pair(j, carry):
        t = 2 * j
        logits_fn(block(t + 1), 1)
        update_fn(consumed(t), 0)
        logits_fn(block(t + 2), 0)
        update_fn(t, 1)
        return carry

    lax.fori_loop(0, (i + 1) // 2, pair, 0)

    @pl.when(i % 2 == 0)
    def _():
        update_fn(consumed(i), 0)


def _flash_store(out_ref, acc_ref, n_heads):
    def head(hd):
        return acc_ref[hd, :HEAD_DIM, :] / acc_ref[hd, HEAD_DIM:HEAD_DIM + 1, :]

    for g in range(n_heads // 2):
        o = jnp.concatenate([head(2 * g), head(2 * g + 1)], axis=0)
        out_ref[:, g * LANES:(g + 1) * LANES] = o.T.astype(out_ref.dtype)


def _split_head_pairs(src_ref, dst_ref, n_pairs):
    low = lax.broadcasted_iota(jnp.int32, (src_ref.shape[0], LANES), 1) < HEAD_DIM
    for g in range(n_pairs):
        pair = src_ref[:, g * LANES:(g + 1) * LANES].astype(F32)
        dst_ref[2 * g] = jnp.where(low, pair, 0.0).astype(dst_ref.dtype)
        dst_ref[2 * g + 1] = jnp.where(low, 0.0, pair).astype(dst_ref.dtype)


def _fold8(x):
    return x.reshape(x.shape[0] // SUBLANES, SUBLANES, x.shape[1])


def _dsa_kernel(slopes_ref, qi_ref, wi_ref, ki_ref, qa_ref, ka_ref, vat_ref, ltri_ref,
                out_ref,
                score_ref, qim_ref, lo_ref, mn_ref, mx_ref, s_ref, m_ref, acc_ref,
                *, topk):
    i = pl.program_id(1)
    nkb = i + 1
    npair = (nkb + 1) // 2
    q0 = i * TQ

    _split_head_pairs(qi_ref, qim_ref, IDX_HEADS // 2)

    kidx = lax.broadcasted_iota(jnp.int32, (TK, TQ), 0)
    qidx = lax.broadcasted_iota(jnp.int32, (TK, TQ), 1)

    wt = wi_ref[...].astype(F32).T

    def score_block(kb, diagonal):
        k0 = pl.multiple_of(kb * TK, TK)
        kblk = ki_ref[pl.ds(k0, TK), :]
        acc = jnp.zeros((TK, TQ), F32)
        for hd in range(IDX_HEADS):
            acc = acc + jnp.maximum(_nt_dot(kblk, qim_ref[hd]), 0.0) * wt[hd:hd + 1, :]
        hi_part = lo_part = _fold8(acc)
        if diagonal:
            adm = (kidx >> 6) <= (qidx >> 6)
            acc = jnp.where(adm, acc, -jnp.inf)
            hi_part = _fold8(acc)
            lo_part = _fold8(jnp.where(adm, acc, jnp.inf))
        score_ref[kb] = acc
        mn_ref[...] = jnp.minimum(mn_ref[...], jnp.min(lo_part, axis=0))
        mx_ref[...] = jnp.maximum(mx_ref[...], jnp.max(hi_part, axis=0))

    mn_ref[...] = jnp.full((SUBLANES, TQ), jnp.inf, F32)
    mx_ref[...] = jnp.full((SUBLANES, TQ), -jnp.inf, F32)

    def score_pair(j, carry):
        score_block(2 * j, False)
        score_block(2 * j + 1, False)
        return carry

    lax.fori_loop(0, i // 2, score_pair, 0)

    @pl.when(i % 2 == 1)
    def _():
        score_block(i - 1, False)

    score_block(i, True)

    @pl.when(nkb % 2 == 1)
    def _():
        score_ref[nkb] = jnp.full((TK, TQ), -jnp.inf, F32)

    qrow = q0 + lax.broadcasted_iota(jnp.int32, (1, TQ), 1)
    n_adm = ((qrow >> 6) + 1) << 6
    k_eff = jnp.minimum(n_adm, topk).astype(F32)

    lo = jnp.min(mn_ref[...], axis=0, keepdims=True)
    hi = jnp.max(mx_ref[...], axis=0, keepdims=True)

    def count_ge(t):
        def body(j, acc):
            a0, a1 = acc
            a0 = a0 + jnp.sum(jnp.where(_fold8(score_ref[2 * j]) >= t, 1.0, 0.0), axis=0)
            a1 = a1 + jnp.sum(jnp.where(_fold8(score_ref[2 * j + 1]) >= t, 1.0, 0.0), axis=0)
            return a0, a1

        z = jnp.zeros((SUBLANES, TQ), F32)
        a0, a1 = lax.fori_loop(0, npair, body, (z, z))
        return jnp.sum(a0 + a1, axis=0, keepdims=True)

    def bisect(_, c):
        lo, hi, c_lo = c
        mid = 0.5 * (lo + hi)
        cnt = count_ge(mid)
        ge = cnt >= k_eff
        return jnp.where(ge, mid, lo), jnp.where(ge, hi, mid), jnp.where(ge, cnt, c_lo)

    lo, hi, c_lo = lax.fori_loop(0, N_BISECT, bisect, (lo, hi, n_adm.astype(F32)))
    lo_ref[...] = lo

    for c in range(TQ // LANES):
        cs = slice(c * LANES, (c + 1) * LANES)

        @pl.when(jnp.max(c_lo[:, cs] - k_eff[:, cs]) > 0.0)
        def _():
            k_c = k_eff[:, cs]

            def counts(t):
                def body(j, c2):
                    ge, gt = c2
                    for kb in (2 * j, 2 * j + 1):
                        s = _fold8(score_ref[kb, :, cs])
                        ge = ge + jnp.sum(jnp.where(s >= t, 1.0, 0.0), axis=0)
                        gt = gt + jnp.sum(jnp.where(s > t, 1.0, 0.0), axis=0)
                    return ge, gt

                z = jnp.zeros((SUBLANES, LANES), F32)
                ge, gt = lax.fori_loop(0, npair, body, (z, z))
                return jnp.sum(ge, axis=0, keepdims=True), jnp.sum(gt, axis=0, keepdims=True)

            def largest(keep):
                def body(j, acc):
                    for kb in (2 * j, 2 * j + 1):
                        s = _fold8(score_ref[kb, :, cs])
                        acc = jnp.maximum(acc, jnp.max(jnp.where(keep(s), s, -jnp.inf), axis=0))
                    return acc

                acc = lax.fori_loop(0, npair, body, jnp.full((SUBLANES, LANES), -jnp.inf, F32))
                return jnp.max(acc, axis=0, keepdims=True)

            def pending(ge):
                return jnp.sum(jnp.where(ge >= k_c, 0.0, 1.0))

            def walk(c3):
                t, ge, _, _ = c3
                t = jnp.where(ge >= k_c, t, largest(lambda s: s < t))
                ge, gt = counts(t)
                return t, ge, gt, pending(ge)

            hi_c = hi[:, cs]
            t0 = largest(lambda s: s <= hi_c)
            ge0, gt0 = counts(t0)
            thr, _, c_gt, _ = lax.while_loop(lambda c3: c3[3] > 0.0, walk,
                                             (t0, ge0, gt0, pending(ge0)))
            need = k_c - c_gt

            def mark(j, seen):
                for kb in (2 * j, 2 * j + 1):
                    s = score_ref[kb, :, cs]
                    eqf = jnp.where(s == thr, 1.0, 0.0)
                    rank = seen + jnp.dot(ltri_ref[...], eqf.astype(BF16),
                                          preferred_element_type=F32)
                    tie = jnp.where(rank <= need, eqf, 0.0)
                    score_ref[kb, :, cs] = jnp.where(s > thr, 1.0, tie) * 2.0 - 1.0
                    seen = seen + jnp.sum(eqf, axis=0, keepdims=True)
                return seen

            lax.fori_loop(0, npair, mark, jnp.zeros((1, LANES), F32))
            lo_ref[:, cs] = jnp.zeros((1, LANES), F32)

    _flash_init(m_ref, acc_ref)
    lo_row = lo_ref[...]

    def logits(kb, buf, future=None):
        k0 = pl.multiple_of(kb * TK, TK)
        bias = jnp.where(score_ref[kb] >= lo_row, 0.0, NEG)
        for hd in range(A_HEADS):
            s = _nt_dot(ka_ref[hd, pl.ds(k0, TK), :], qa_ref[hd])
            s_ref[buf, hd] = s + (bias if future is None else bias - slopes_ref[hd] * future)

    def update(kb, buf):
        _flash_softmax_pv(A_HEADS, s_ref.at[buf],
                          lambda hd: vat_ref[kb, hd * V_ROWS:(hd + 1) * V_ROWS, :],
                          m_ref, acc_ref)

    _flash_loop(i, lambda buf: logits(i, buf, 2.0 * jnp.maximum(kidx - qidx, 0).astype(F32)),
                logits, update)
    _flash_store(out_ref, acc_ref, A_HEADS)


def _mla_kernel(qb_ref, kb_ref, vbt_ref, out_ref, s_ref, m_ref, acc_ref):
    i = pl.program_id(1)
    _flash_init(m_ref, acc_ref)

    def logits(kb, buf, bias=None):
        k0 = pl.multiple_of(kb * TK, TK)
        for hd in range(B_HEADS):
            s = _nt_dot(kb_ref[hd, pl.ds(k0, TK), :], qb_ref[hd])
            s_ref[buf, hd] = s if bias is None else s + bias

    def update(kb, buf):
        _flash_softmax_pv(B_HEADS, s_ref.at[buf],
                          lambda hd: vbt_ref[kb, hd * V_ROWS:(hd + 1) * V_ROWS, :],
                          m_ref, acc_ref)

    kidx = lax.broadcasted_iota(jnp.int32, (TK, TQ), 0)
    qidx = lax.broadcasted_iota(jnp.int32, (TK, TQ), 1)
    _flash_loop(i, lambda buf: logits(i, buf, jnp.where((kidx >> 6) <= (qidx >> 6), 0.0, NEG)),
                logits, update)
    _flash_store(out_ref, acc_ref, B_HEADS)


def _out_kernel(x_ref, ya_ref, yb_ref, z_ref, qm_ref, gmat_ref, gqm_ref, km_ref, vm_ref, wout_ref,
                out_ref):
    tm = x_ref.shape[0]
    low = lax.broadcasted_iota(jnp.int32, (tm, LANES), 1) < HEAD_DIM
    qm = _head64_norm(qm_ref[...].astype(F32), gmat_ref, gqm_ref[...])
    ym = []
    for g in range(M_HEADS // 2):
        pair = qm[:, g * LANES:(g + 1) * LANES]
        outs = []
        for q in (jnp.where(low, pair, 0.0), jnp.where(low, 0.0, pair)):
            s = _nt_dot(q.astype(BF16), km_ref[g])
            p = jnp.exp2(s - jnp.max(s, axis=1, keepdims=True))
            o = jnp.dot(p.astype(BF16), vm_ref[g], preferred_element_type=F32)
            outs.append(o / jnp.sum(p, axis=1, keepdims=True))
        ym.append(jnp.where(low, outs[0], outs[1]))
    y = jnp.concatenate([ya_ref[...].astype(F32), yb_ref[...].astype(F32)] + ym, axis=1)
    z = z_ref[...].astype(F32)
    y = (y * (z / (1.0 + jnp.exp(-z)))).astype(BF16)
    out_ref[...] = x_ref[...] + jnp.dot(y, wout_ref[...], preferred_element_type=F32)


def _full(shape):
    n = len(shape)
    return pl.BlockSpec(shape, lambda *_: (0,) * n)


def _params(*sem):
    return pltpu.CompilerParams(dimension_semantics=sem, vmem_limit_bytes=VMEM_LIMIT)


def _bf16_pieces(v, n):
    out, rest = [], v
    for _ in range(n):
        piece = rest.astype(BF16).astype(F32)
        out.append(piece)
        rest = rest - piece
    return jnp.stack(out, axis=-1)


def kernel(x, mem, g_in, w_in, w_uq, g_cq, w_ukv, g_ckv, g_qa, g_ka, g_qb, g_kb,
           g_mem, w_mk, w_mv, g_qm, g_km, w_out):
    B, S, D = x.shape
    n_mem = mem.shape[1]
    assert S % TM == 0 and TM % TK == 0 and TQ == TK and D == D_MIX and (S // TK) % 2 == 0
    assert _C_QI == 0 and _C_KI % LANES == 0 and _C_WI % LANES == 0
    assert _C_QM % M_WIDTH == 0 and _C_GATE % D_MIX == 0
    topk = min(TOPK_MAX, S // 4)
    nkb = S // TK
    l = 0
    half = B_ROPE // 2

    wi = w_in[l]
    o = [0]
    for n in (A_WIDTH, A_WIDTH, A_WIDTH, IDX_HEADS * IDX_DIM, IDX_DIM, IDX_HEADS, A_WIDTH,
              Q_LORA, KV_LORA, B_ROPE, B_WIDTH, M_WIDTH, M_WIDTH):
        o.append(o[-1] + n)
    (c_qa, c_ka, c_va, c_qi, c_ki, c_wi, c_za, c_cq, c_ckv, c_kr, c_zb, c_qm, c_zm) = [
        wi[:, o[j]:o[j + 1]] for j in range(13)]
    zeros = lambda n: jnp.zeros((D, n), wi.dtype)
    w_pack = jnp.concatenate([
        c_qi,
        c_ki, c_ki,
        c_wi, zeros(LANES - IDX_HEADS),
        c_qm,
        c_za, c_zb, c_zm,
        c_qa, c_ka, c_va,
        c_cq, c_ckv,
        zeros(B_NOPE), c_kr, zeros(LANES - B_QK),
        zeros(B_NOPE), c_kr[:, half:], c_kr[:, :half], zeros(LANES - B_QK),
    ], axis=1).astype(BF16)
    assert w_pack.shape[1] == _C_END

    def rope_partner(v):
        z = jnp.zeros_like(v[..., :B_NOPE])
        return jnp.concatenate([z, v[..., B_NOPE + half:B_QK], v[..., B_NOPE:B_NOPE + half],
                                jnp.zeros_like(v[..., B_QK:])], axis=-1)

    wuq_h = jnp.pad(w_uq[l].reshape(Q_LORA, B_HEADS, B_QK), ((0, 0), (0, 0), (0, LANES - B_QK)))
    wuq_p = jnp.concatenate([wuq_h.reshape(Q_LORA, B_HEADS * LANES),
                             rope_partner(wuq_h).reshape(Q_LORA, B_HEADS * LANES)],
                            axis=1).astype(BF16)
    wukv = w_ukv[l].reshape(KV_LORA, B_HEADS, B_NOPE + B_V)
    wukv_k = jnp.pad(wukv[:, :, :B_NOPE], ((0, 0), (0, 0), (0, LANES - B_NOPE))
                     ).reshape(KV_LORA, B_HEADS * LANES).astype(BF16)
    wukv_v = wukv[:, :, B_NOPE:].reshape(KV_LORA, B_WIDTH).astype(BF16)

    idx = jnp.arange(A_WIDTH)
    gmat = jnp.where((idx[:, None] // HEAD_DIM) == (idx[None, :] // HEAD_DIM),
                     1.0 / HEAD_DIM, 0.0).astype(BF16)
    row2 = lambda v: v.reshape(1, -1).astype(F32)
    gqa_t = row2(jnp.tile(g_qa[l], A_HEADS) * (HEAD_DIM ** -0.5 * LOG2E))
    gka_t = row2(jnp.tile(g_ka[l], A_HEADS))
    gqm_t = row2(jnp.tile(g_qm[l], M_HEADS) * (HEAD_DIM ** -0.5 * LOG2E))
    gkm_t = row2(jnp.tile(g_km[l], M_HEADS))
    gqb_p = row2(jnp.pad(g_qb[l], (0, LANES - B_QK)) * (B_QK ** -0.5 * LOG2E))
    gkb_p = row2(jnp.pad(g_kb[l], (0, LANES - B_QK)))

    pos = jnp.arange(S, dtype=jnp.int32)
    inv = 1.0 / (ROPE_THETA ** (jnp.arange(half, dtype=F32) / half))
    ang = pos.astype(F32)[:, None] * inv[None, :]
    cos, sin = jnp.cos(ang), jnp.sin(ang)
    zs = lambda n: jnp.zeros((S, n), F32)
    rope_c = jnp.concatenate([jnp.ones((S, B_NOPE), F32), cos, cos, jnp.ones((S, LANES - B_QK), F32)], 1)
    rope_s = jnp.concatenate([zs(B_NOPE), -sin, sin, zs(LANES - B_QK)], 1)
    qa_tab, qb_tab = rope_c * gqb_p, rope_s * rope_partner(gqb_p)
    ka_tab, kb_tab = rope_c * gkb_p, rope_s * rope_partner(gkb_p)
    src = jnp.arange(A_WIDTH)
    place = (src[:, None] // HEAD_DIM * LANES + src[:, None] % HEAD_DIM
             == jnp.arange(A_HEADS * LANES)[None, :]).astype(BF16)

    slopes = 2.0 ** (-ALIBI_MAX * jnp.arange(1, A_HEADS + 1, dtype=F32) / A_HEADS) * LOG2E
    pieces = _bf16_pieces(slopes, N_ALIBI_PIECES)
    n_feat = 2 * N_ALIBI_PIECES
    qfeat = jnp.concatenate([jnp.zeros((A_HEADS, HEAD_DIM), F32), pieces * float(CHUNK), pieces,
                             jnp.zeros((A_HEADS, LANES - HEAD_DIM - n_feat), F32)], axis=1)
    pos_hi = jnp.repeat((pos // CHUNK).astype(F32)[:, None], N_ALIBI_PIECES, axis=1)
    pos_lo = jnp.repeat((pos % CHUNK).astype(F32)[:, None], N_ALIBI_PIECES, axis=1)
    kfeat = jnp.concatenate([zs(HEAD_DIM), pos_hi, pos_lo, zs(LANES - HEAD_DIM - n_feat)], axis=1)

    kk = jnp.arange(TK)
    ltri = (kk[None, :] <= kk[:, None]).astype(BF16)

    sds = jax.ShapeDtypeStruct
    tok = lambda w, j=0: pl.BlockSpec((None, TM, w), lambda b, t: (b, t, j))
    u = pl.pallas_call(
        _inproj_kernel,
        grid=(B, S // TM),
        in_specs=[tok(D), _full((1, D)), _full((D, _C_END))],
        out_specs=tok(_C_END),
        out_shape=sds((B, S, _C_END), BF16),
        compiler_params=_params("parallel", "parallel"),
        name="inproj",
    )(x, row2(g_in[l]), w_pack)

    grp = lambda n: pl.BlockSpec((None, n, TM, LANES), lambda b, t: (b, 0, t, 0))
    vt_out = lambda n: pl.BlockSpec((None, TM // TK, n * V_ROWS, TK), lambda b, t: (b, t, 0, 0))
    tab = pl.BlockSpec((TM, LANES), lambda b, t: (t, 0))
    qa, ka, vat, qb, kb, vbt = pl.pallas_call(
        _tokprep_kernel,
        grid=(B, S // TM),
        in_specs=[tok(_C_END), _full((A_WIDTH, A_WIDTH)), _full((A_WIDTH, A_HEADS * LANES)),
                  _full((1, A_WIDTH)), _full((1, A_WIDTH)), _full((A_HEADS, LANES)), tab,
                  _full((1, Q_LORA)), _full((Q_LORA, 2 * B_HEADS * LANES)), _full((1, KV_LORA)),
                  _full((KV_LORA, B_HEADS * LANES)), _full((KV_LORA, B_WIDTH)),
                  tab, tab, tab, tab],
        out_specs=[grp(A_HEADS), grp(A_HEADS), vt_out(A_HEADS),
                   grp(B_HEADS), grp(B_HEADS), vt_out(B_HEADS)],
        out_shape=[sds((B, A_HEADS, S, LANES), BF16), sds((B, A_HEADS, S, LANES), BF16),
                   sds((B, nkb, A_HEADS * V_ROWS, TK), BF16),
                   sds((B, B_HEADS, S, LANES), BF16), sds((B, B_HEADS, S, LANES), BF16),
                   sds((B, nkb, B_HEADS * V_ROWS, TK), BF16)],
        compiler_params=_params("parallel", "parallel"),
        name="tokprep",
    )(u, gmat, place, gqa_t, gka_t, qfeat, kfeat, row2(g_cq[l]), wuq_p,
      row2(g_ckv[l]), wukv_k, wukv_v, qa_tab, qb_tab, ka_tab, kb_tab)

    km, vm = pl.pallas_call(
        _memkv_kernel,
        grid=(B,),
        in_specs=[pl.BlockSpec((None, n_mem, D), lambda b: (b, 0, 0)), _full((1, D)),
                  _full((D, M_WIDTH)), _full((D, M_WIDTH)), _full((A_WIDTH, A_WIDTH)),
                  _full((1, M_WIDTH))],
        out_specs=[pl.BlockSpec((None, 2, n_mem, LANES), lambda b: (b, 0, 0, 0))] * 2,
        out_shape=[sds((B, 2, n_mem, LANES), BF16)] * 2,
        compiler_params=_params("parallel"),
        name="memkv",
    )(mem, row2(g_mem[l]), w_mk[l].astype(BF16), w_mv[l].astype(BF16), gmat, gkm_t)

    qtile = lambda w, j=0: pl.BlockSpec((None, TQ, w), lambda b, t: (b, t, j))
    seq = lambda n: pl.BlockSpec((None, n, S, LANES), lambda b, t: (b, 0, 0, 0))
    vt_in = lambda n: pl.BlockSpec((None, nkb, n * V_ROWS, TK), lambda b, t: (b, 0, 0, 0))
    heads = lambda n: pl.BlockSpec((None, n, TQ, LANES), lambda b, t: (b, 0, t, 0))
    flash_scratch = lambda n: [pltpu.VMEM((2, n, TK, TQ), F32), pltpu.VMEM((n, 1, TQ), F32),
                               pltpu.VMEM((n, V_ROWS, TQ), F32)]
    ya = pl.pallas_call(
        functools.partial(_dsa_kernel, topk=topk),
        grid=(B, S // TQ),
        in_specs=[pl.BlockSpec(memory_space=pltpu.SMEM),
                  qtile(IDX_HEADS * IDX_DIM, _C_QI // (IDX_HEADS * IDX_DIM)),
                  qtile(LANES, _C_WI // LANES),
                  pl.BlockSpec((None, S, LANES), lambda b, t: (b, 0, _C_KI // LANES)),
                  heads(A_HEADS), seq(A_HEADS), vt_in(A_HEADS), _full((TK, TK))],
        out_specs=qtile(A_WIDTH),
        out_shape=sds((B, S, A_WIDTH), BF16),
        scratch_shapes=[pltpu.VMEM((nkb, TK, TQ), F32),
                        pltpu.VMEM((IDX_HEADS, TQ, LANES), BF16),
                        pltpu.VMEM((1, TQ), F32), pltpu.VMEM((SUBLANES, TQ), F32),
                        pltpu.VMEM((SUBLANES, TQ), F32)] + flash_scratch(A_HEADS),
        compiler_params=_params("parallel", "arbitrary"),
        name="dsa",
    )(slopes, u, u, u, qa, ka, vat, ltri)

    yb = pl.pallas_call(
        _mla_kernel,
        grid=(B, S // TQ),
        in_specs=[heads(B_HEADS), seq(B_HEADS), vt_in(B_HEADS)],
        out_specs=qtile(B_WIDTH),
        out_shape=sds((B, S, B_WIDTH), BF16),
        scratch_shapes=flash_scratch(B_HEADS),
        compiler_params=_params("parallel", "arbitrary"),
        name="mla",
    )(qb, kb, vbt)

    memkv = pl.BlockSpec((None, 2, n_mem, LANES), lambda b, t: (b, 0, 0, 0))
    return pl.pallas_call(
        _out_kernel,
        grid=(B, S // TM),
        in_specs=[tok(D), tok(A_WIDTH), tok(B_WIDTH), tok(D_MIX, _C_GATE // D_MIX),
                  tok(M_WIDTH, _C_QM // M_WIDTH), _full((A_WIDTH, A_WIDTH)), _full((1, M_WIDTH)),
                  memkv, memkv, _full((D, D))],
        out_specs=tok(D),
        out_shape=sds((B, S, D), x.dtype),
        compiler_params=_params("parallel", "parallel"),
        name="outproj",
    )(x, ya, yb, u, u, gmat, gqm_t, km, vm, w_out[l].astype(BF16))
```

```python
import functools
import math

import jax
import jax.numpy as jnp
from jax import lax
from jax.experimental import pallas as pl
from jax.experimental.pallas import tpu as pltpu

F32 = jnp.float32
BF16 = jnp.bfloat16

CHUNK = 64
EPS = 1e-6
HEAD_DIM = 64
A_HEADS = 6
IDX_HEADS = 8
IDX_DIM = 64
TOPK_MAX = 256
ALIBI_MAX = 8.0
B_HEADS = 6
B_NOPE = 64
B_ROPE = 32
B_V = 64
B_QK = B_NOPE + B_ROPE
Q_LORA = 256
KV_LORA = 128
ROPE_THETA = 10000.0
M_HEADS = 4
A_WIDTH = A_HEADS * HEAD_DIM
B_WIDTH = B_HEADS * B_V
M_WIDTH = M_HEADS * HEAD_DIM
D_MIX = A_WIDTH + B_WIDTH + M_WIDTH
LOG2E = math.log2(math.e)
N_ALIBI_PIECES = 3

LANES = 128
SUBLANES = 8
VMEM_LIMIT = 52 * 1024 * 1024

TM = 512
TN = 512
TQ = 256
TK = 256
N_COARSE = 8
N_FINE = 12
PACK = 2 * SUBLANES
TINY = 1e-37
V_ROWS = 80
NEG = -1e30

_C_QI = 0
_C_KI = _C_QI + IDX_HEADS * IDX_DIM
_C_WI = _C_KI + LANES
_C_QM = _C_WI + LANES
_C_GATE = _C_QM + M_WIDTH
_C_QA = _C_GATE + D_MIX
_C_KA = _C_QA + A_WIDTH
_C_VA = _C_KA + A_WIDTH
_C_CQ = _C_VA + A_WIDTH
_C_CKV = _C_CQ + Q_LORA
_C_KR = _C_CKV + KV_LORA
_C_KRP = _C_KR + LANES
_C_END = _C_KRP + LANES


def _nt_dot(a, b):
    return lax.dot_general(a, b, (((1,), (1,)), ((), ())), preferred_element_type=F32)


def _head64_norm(u, gmat_ref, g):
    n = u.shape[1]
    msq = jnp.dot((u * u).astype(BF16), gmat_ref[:n, :n], preferred_element_type=F32)
    return u * lax.rsqrt(msq + EPS) * g


def _inproj_kernel(x_ref, gin_ref, w_ref, u_ref):
    x = x_ref[...]
    ms = jnp.mean(x * x, axis=-1, keepdims=True)
    h = (x * lax.rsqrt(ms + EPS) * gin_ref[...]).astype(BF16)
    for a in range(0, _C_END, TN):
        b = min(a + TN, _C_END)
        u_ref[:, a:b] = jnp.dot(h, w_ref[:, a:b], preferred_element_type=F32).astype(BF16)


def _tokprep_kernel(u_ref, gmat_ref, place_ref, gqa_ref, gka_ref, qfeat_ref, kfeat_ref, gcq_ref,
                    wuq_ref, gckv_ref, wukvk_ref, wukvv_ref, qa_tab_ref, qb_tab_ref, ka_tab_ref,
                    kb_tab_ref, qa_ref, ka_ref, vat_ref, qb_ref, kb_ref, vbt_ref):
    def cols(a, b):
        return u_ref[:, a:b].astype(F32)

    ones_rows = (lax.broadcasted_iota(jnp.int32, (V_ROWS - HEAD_DIM, TK), 0) == 0).astype(BF16)

    def store_transposed(dst_ref, v):
        vt = v.T.astype(BF16)
        for j in range(TM // TK):
            for hd in range(v.shape[1] // HEAD_DIM):
                r = hd * V_ROWS
                dst_ref[j, r:r + HEAD_DIM, :] = vt[hd * HEAD_DIM:(hd + 1) * HEAD_DIM,
                                                   j * TK:(j + 1) * TK]
                dst_ref[j, r + HEAD_DIM:r + V_ROWS, :] = ones_rows

    def store_heads(dst_ref, v, feat_fn):
        placed = jnp.dot(v.astype(BF16), place_ref[...], preferred_element_type=F32)
        for hd in range(A_HEADS):
            dst_ref[hd] = (placed[:, hd * LANES:(hd + 1) * LANES] + feat_fn(hd)).astype(BF16)

    store_heads(qa_ref, _head64_norm(cols(_C_QA, _C_KA), gmat_ref, gqa_ref[...]),
                lambda hd: qfeat_ref[hd:hd + 1, :])
    kfeat = kfeat_ref[...]
    store_heads(ka_ref, _head64_norm(cols(_C_KA, _C_VA), gmat_ref, gka_ref[...]), lambda hd: kfeat)
    store_transposed(vat_ref, cols(_C_VA, _C_CQ))

    def head96(uh, partner, tab_a, tab_b):
        ss = jnp.sum(uh * uh, axis=-1, keepdims=True) * (1.0 / B_QK)
        return ((uh * tab_a + partner * tab_b) * lax.rsqrt(ss + EPS)).astype(BF16)

    cq = cols(_C_CQ, _C_CKV)
    cq = cq * lax.rsqrt(jnp.mean(cq * cq, axis=-1, keepdims=True) + EPS) * gcq_ref[...]
    q = jnp.dot(cq.astype(BF16), wuq_ref[...], preferred_element_type=F32)
    qa_tab = qa_tab_ref[...]
    qb_tab = qb_tab_ref[...]
    for hd in range(B_HEADS):
        qb_ref[hd] = head96(q[:, hd * LANES:(hd + 1) * LANES],
                            q[:, (B_HEADS + hd) * LANES:(B_HEADS + hd + 1) * LANES], qa_tab, qb_tab)

    ckv = cols(_C_CKV, _C_KR)
    ckv = ckv * lax.rsqrt(jnp.mean(ckv * ckv, axis=-1, keepdims=True) + EPS) * gckv_ref[...]
    ckv = ckv.astype(BF16)
    kk = jnp.dot(ckv, wukvk_ref[...], preferred_element_type=F32)
    store_transposed(vbt_ref, jnp.dot(ckv, wukvv_ref[...], preferred_element_type=F32))
    krope = cols(_C_KR, _C_KRP)
    krope_partner = cols(_C_KRP, _C_END)
    ka_tab = ka_tab_ref[...]
    kb_tab = kb_tab_ref[...]
    for hd in range(B_HEADS):
        kb_ref[hd] = head96(kk[:, hd * LANES:(hd + 1) * LANES] + krope, krope_partner, ka_tab, kb_tab)


def _memkv_kernel(mem_ref, gmem_ref, wmk_ref, wmv_ref, gmat_ref, gkm_ref, km_ref, vm_ref):
    x = mem_ref[...]
    ms = jnp.mean(x * x, axis=-1, keepdims=True)
    m = (x * lax.rsqrt(ms + EPS) * gmem_ref[...]).astype(BF16)
    k = _head64_norm(jnp.dot(m, wmk_ref[...], preferred_element_type=F32), gmat_ref,
                     gkm_ref[...]).astype(BF16)
    v = jnp.dot(m, wmv_ref[...], preferred_element_type=F32).astype(BF16)
    for g in range(M_WIDTH // LANES):
        km_ref[g] = k[:, g * LANES:(g + 1) * LANES]
        vm_ref[g] = v[:, g * LANES:(g + 1) * LANES]


def _flash_init(m_ref, acc_ref):
    m_ref[...] = jnp.full(m_ref.shape, NEG, F32)
    acc_ref[...] = jnp.zeros(acc_ref.shape, F32)


def _flash_softmax_pv(n_heads, s_ref, vt_fn, m_ref, acc_ref):
    for hd in range(n_heads):
        s = s_ref[hd]
        m_prev = m_ref[hd]
        m_new = jnp.maximum(m_prev, jnp.max(s, axis=0, keepdims=True))
        p = jnp.exp2(s - m_new).astype(BF16)
        alpha = jnp.exp2(m_prev - m_new)
        m_ref[hd] = m_new
        acc_ref[hd] = alpha * acc_ref[hd] + jnp.dot(vt_fn(hd), p, preferred_element_type=F32)


def _flash_loop(i, diag_logits_fn, logits_fn, update_fn):
    def block(t):
        return jnp.minimum(t, i) - 1

    def consumed(t):
        return jnp.where(t == 0, i, t - 1)

    diag_logits_fn(0)

    def pair(j, carry):
        t = 2 * j
        logits_fn(block(t + 1), 1)
        update_fn(consumed(t), 0)
        logits_fn(block(t + 2), 0)
        update_fn(t, 1)
        return carry

    lax.fori_loop(0, (i + 1) // 2, pair, 0)

    @pl.when(i % 2 == 0)
    def _():
        update_fn(consumed(i), 0)


def _flash_store(out_ref, acc_ref, n_heads):
    def head(hd):
        return acc_ref[hd, :HEAD_DIM, :] / acc_ref[hd, HEAD_DIM:HEAD_DIM + 1, :]

    for g in range(n_heads // 2):
        o = jnp.concatenate([head(2 * g), head(2 * g + 1)], axis=0)
        out_ref[:, g * LANES:(g + 1) * LANES] = o.T.astype(out_ref.dtype)


def _split_head_pairs(src_ref, dst_ref, n_pairs):
    low = lax.broadcasted_iota(jnp.int32, (src_ref.shape[0], LANES), 1) < HEAD_DIM
    for g in range(n_pairs):
        pair = src_ref[:, g * LANES:(g + 1) * LANES].astype(F32)
        dst_ref[2 * g] = jnp.where(low, pair, 0.0).astype(dst_ref.dtype)
        dst_ref[2 * g + 1] = jnp.where(low, 0.0, pair).astype(dst_ref.dtype)


def _fold8(x):
    return x.reshape(x.shape[0] // SUBLANES, SUBLANES, x.shape[1])


def _dsa_kernel(slopes_ref, qi_ref, wi_ref, ki_ref, qa_ref, ka_ref, vat_ref, ltri_ref,
                out_ref,
                score_ref, sb_ref, qim_ref, lo_ref, mn_ref, mx_ref, s_ref, m_ref, acc_ref,
                *, topk):
    i = pl.program_id(1)
    nkb = i + 1
    npair = (nkb + 1) // 2
    q0 = i * TQ

    _split_head_pairs(qi_ref, qim_ref, IDX_HEADS // 2)

    kidx = lax.broadcasted_iota(jnp.int32, (TK, TQ), 0)
    qidx = lax.broadcasted_iota(jnp.int32, (TK, TQ), 1)

    wt = wi_ref[...].astype(F32).T

    def score_block(kb, diagonal):
        k0 = pl.multiple_of(kb * TK, TK)
        kblk = ki_ref[pl.ds(k0, TK), :]
        acc = jnp.zeros((TK, TQ), F32)
        for hd in range(IDX_HEADS):
            acc = acc + jnp.maximum(_nt_dot(kblk, qim_ref[hd]), 0.0) * wt[hd:hd + 1, :]
        hi_part = lo_part = _fold8(acc)
        if diagonal:
            adm = (kidx >> 6) <= (qidx >> 6)
            acc = jnp.where(adm, acc, -jnp.inf)
            hi_part = _fold8(acc)
            lo_part = _fold8(jnp.where(adm, acc, jnp.inf))
        score_ref[kb] = acc
        sb_ref[kb] = acc.astype(BF16)
        mn_ref[...] = jnp.minimum(mn_ref[...], jnp.min(lo_part, axis=0))
        mx_ref[...] = jnp.maximum(mx_ref[...], jnp.max(hi_part, axis=0))

    mn_ref[...] = jnp.full((SUBLANES, TQ), jnp.inf, F32)
    mx_ref[...] = jnp.full((SUBLANES, TQ), -jnp.inf, F32)

    def score_pair(j, carry):
        score_block(2 * j, False)
        score_block(2 * j + 1, False)
        return carry

    lax.fori_loop(0, i // 2, score_pair, 0)

    @pl.when(i % 2 == 1)
    def _():
        score_block(i - 1, False)

    score_block(i, True)

    @pl.when(nkb % 2 == 1)
    def _():
        score_ref[nkb] = jnp.full((TK, TQ), -jnp.inf, F32)
        sb_ref[nkb] = jnp.full((TK, TQ), -jnp.inf, BF16)

    qrow = q0 + lax.broadcasted_iota(jnp.int32, (1, TQ), 1)
    n_adm = ((qrow >> 6) + 1) << 6
    k_eff = jnp.minimum(n_adm, topk).astype(F32)

    lo = jnp.min(mn_ref[...], axis=0, keepdims=True)
    hi = jnp.max(mx_ref[...], axis=0, keepdims=True)

    def count_ge_coarse(t):
        tb = jnp.broadcast_to(t.astype(BF16), (PACK, TQ))
        one = jnp.ones((PACK, TQ), BF16)
        zero = jnp.zeros((PACK, TQ), BF16)

        def body(j, acc):
            for kb in (2 * j, 2 * j + 1):
                sb = sb_ref[kb]
                for r in range(TK // PACK):
                    acc = acc + jnp.where(sb[r * PACK:(r + 1) * PACK, :] >= tb, one, zero)
            return acc

        acc = lax.fori_loop(0, npair, body, zero)
        return jnp.sum(acc.astype(F32), axis=0, keepdims=True)

    def coarse(_, c):
        lo, hi = c
        mid = (0.5 * (lo + hi)).astype(BF16).astype(F32)
        ge = count_ge_coarse(mid) >= k_eff
        return jnp.where(ge, mid, lo), jnp.where(ge, hi, mid)

    lo, hi = lax.fori_loop(0, N_COARSE, coarse,
                           (lo.astype(BF16).astype(F32), hi.astype(BF16).astype(F32)))
    lo = lo - jnp.abs(lo) * 2.0 ** -8 - TINY
    hi = hi + jnp.abs(hi) * 2.0 ** -8 + TINY

    def count_ge(t):
        def body(j, acc):
            a0, a1 = acc
            a0 = a0 + jnp.sum(jnp.where(_fold8(score_ref[2 * j]) >= t, 1.0, 0.0), axis=0)
            a1 = a1 + jnp.sum(jnp.where(_fold8(score_ref[2 * j + 1]) >= t, 1.0, 0.0), axis=0)
            return a0, a1

        z = jnp.zeros((SUBLANES, TQ), F32)
        a0, a1 = lax.fori_loop(0, npair, body, (z, z))
        return jnp.sum(a0 + a1, axis=0, keepdims=True)

    def bisect(_, c):
        lo, hi, c_lo = c
        mid = 0.5 * (lo + hi)
        cnt = count_ge(mid)
        ge = cnt >= k_eff
        return jnp.where(ge, mid, lo), jnp.where(ge, hi, mid), jnp.where(ge, cnt, c_lo)

    lo, hi, c_lo = lax.fori_loop(0, N_FINE, bisect, (lo, hi, k_eff + 1.0))
    lo_ref[...] = lo

    for c in range(TQ // LANES):
        cs = slice(c * LANES, (c + 1) * LANES)

        @pl.when(jnp.max(c_lo[:, cs] - k_eff[:, cs]) > 0.0)
        def _():
            k_c = k_eff[:, cs]

            def counts(t):
                def body(j, c2):
                    ge, gt = c2
                    for kb in (2 * j, 2 * j + 1):
                        s = _fold8(score_ref[kb, :, cs])
                        ge = ge + jnp.sum(jnp.where(s >= t, 1.0, 0.0), axis=0)
                        gt = gt + jnp.sum(jnp.where(s > t, 1.0, 0.0), axis=0)
                    return ge, gt

                z = jnp.zeros((SUBLANES, LANES), F32)
                ge, gt = lax.fori_loop(0, npair, body, (z, z))
                return jnp.sum(ge, axis=0, keepdims=True), jnp.sum(gt, axis=0, keepdims=True)

            def largest(keep):
                def body(j, acc):
                    for kb in (2 * j, 2 * j + 1):
                        s = _fold8(score_ref[kb, :, cs])
                        acc = jnp.maximum(acc, jnp.max(jnp.where(keep(s), s, -jnp.inf), axis=0))
                    return acc

                acc = lax.fori_loop(0, npair, body, jnp.full((SUBLANES, LANES), -jnp.inf, F32))
                return jnp.max(acc, axis=0, keepdims=True)

            def pending(ge):
                return jnp.sum(jnp.where(ge >= k_c, 0.0, 1.0))

            def walk(c3):
                t, ge, _, _ = c3
                t = jnp.where(ge >= k_c, t, largest(lambda s: s < t))
                ge, gt = counts(t)
                return t, ge, gt, pending(ge)

            hi_c = hi[:, cs]
            t0 = largest(lambda s: s <= hi_c)
            ge0, gt0 = counts(t0)
            thr, _, c_gt, _ = lax.while_loop(lambda c3: c3[3] > 0.0, walk,
                                             (t0, ge0, gt0, pending(ge0)))
            need = k_c - c_gt

            def mark(j, seen):
                for kb in (2 * j, 2 * j + 1):
                    s = score_ref[kb, :, cs]
                    eqf = jnp.where(s == thr, 1.0, 0.0)
                    rank = seen + jnp.dot(ltri_ref[...], eqf.astype(BF16),
                                          preferred_element_type=F32)
                    tie = jnp.where(rank <= need, eqf, 0.0)
                    score_ref[kb, :, cs] = jnp.where(s > thr, 1.0, tie) * 2.0 - 1.0
                    seen = seen + jnp.sum(eqf, axis=0, keepdims=True)
                return seen

            lax.fori_loop(0, npair, mark, jnp.zeros((1, LANES), F32))
            lo_ref[:, cs] = jnp.zeros((1, LANES), F32)

    _flash_init(m_ref, acc_ref)
    lo_row = lo_ref[...]

    def logits(kb, buf, future=None):
        k0 = pl.multiple_of(kb * TK, TK)
        bias = jnp.where(score_ref[kb] >= lo_row, 0.0, NEG)
        for hd in range(A_HEADS):
            s = _nt_dot(ka_ref[hd, pl.ds(k0, TK), :], qa_ref[hd])
            s_ref[buf, hd] = s + (bias if future is None else bias - slopes_ref[hd] * future)

    def update(kb, buf):
        _flash_softmax_pv(A_HEADS, s_ref.at[buf],
                          lambda hd: vat_ref[kb, hd * V_ROWS:(hd + 1) * V_ROWS, :],
                          m_ref, acc_ref)

    _flash_loop(i, lambda buf: logits(i, buf, 2.0 * jnp.maximum(kidx - qidx, 0).astype(F32)),
                logits, update)
    _flash_store(out_ref, acc_ref, A_HEADS)


def _mla_kernel(qb_ref, kb_ref, vbt_ref, out_ref, s_ref, m_ref, acc_ref):
    i = pl.program_id(1)
    _flash_init(m_ref, acc_ref)

    def logits(kb, buf, bias=None):
        k0 = pl.multiple_of(kb * TK, TK)
        for hd in range(B_HEADS):
            s = _nt_dot(kb_ref[hd, pl.ds(k0, TK), :], qb_ref[hd])
            s_ref[buf, hd] = s if bias is None else s + bias

    def update(kb, buf):
        _flash_softmax_pv(B_HEADS, s_ref.at[buf],
                          lambda hd: vbt_ref[kb, hd * V_ROWS:(hd + 1) * V_ROWS, :],
                          m_ref, acc_ref)

    kidx = lax.broadcasted_iota(jnp.int32, (TK, TQ), 0)
    qidx = lax.broadcasted_iota(jnp.int32, (TK, TQ), 1)
    _flash_loop(i, lambda buf: logits(i, buf, jnp.where((kidx >> 6) <= (qidx >> 6), 0.0, NEG)),
                logits, update)
    _flash_store(out_ref, acc_ref, B_HEADS)


def _out_kernel(x_ref, ya_ref, yb_ref, z_ref, qm_ref, gmat_ref, gqm_ref, km_ref, vm_ref, wout_ref,
                out_ref):
    tm = x_ref.shape[0]
    low = lax.broadcasted_iota(jnp.int32, (tm, LANES), 1) < HEAD_DIM
    qm = _head64_norm(qm_ref[...].astype(F32), gmat_ref, gqm_ref[...])
    ym = []
    for g in range(M_HEADS // 2):
        pair = qm[:, g * LANES:(g + 1) * LANES]
        outs = []
        for q in (jnp.where(low, pair, 0.0), jnp.where(low, 0.0, pair)):
            s = _nt_dot(q.astype(BF16), km_ref[g])
            p = jnp.exp2(s - jnp.max(s, axis=1, keepdims=True))
            o = jnp.dot(p.astype(BF16), vm_ref[g], preferred_element_type=F32)
            outs.append(o / jnp.sum(p, axis=1, keepdims=True))
        ym.append(jnp.where(low, outs[0], outs[1]))
    y = jnp.concatenate([ya_ref[...].astype(F32), yb_ref[...].astype(F32)] + ym, axis=1)
    z = z_ref[...].astype(F32)
    y = (y * (z / (1.0 + jnp.exp(-z)))).astype(BF16)
    out_ref[...] = x_ref[...] + jnp.dot(y, wout_ref[...], preferred_element_type=F32)


def _full(shape):
    n = len(shape)
    return pl.BlockSpec(shape, lambda *_: (0,) * n)


def _params(*sem):
    return pltpu.CompilerParams(dimension_semantics=sem, vmem_limit_bytes=VMEM_LIMIT)


def _bf16_pieces(v, n):
    out, rest = [], v
    for _ in range(n):
        piece = rest.astype(BF16).astype(F32)
        out.append(piece)
        rest = rest - piece
    return jnp.stack(out, axis=-1)


def kernel(x, mem, g_in, w_in, w_uq, g_cq, w_ukv, g_ckv, g_qa, g_ka, g_qb, g_kb,
           g_mem, w_mk, w_mv, g_qm, g_km, w_out):
    B, S, D = x.shape
    n_mem = mem.shape[1]
    assert S % TM == 0 and TM % TK == 0 and TQ == TK and D == D_MIX and (S // TK) % 2 == 0
    assert S // PACK <= 256
    assert _C_QI == 0 and _C_KI % LANES == 0 and _C_WI % LANES == 0
    assert _C_QM % M_WIDTH == 0 and _C_GATE % D_MIX == 0
    topk = min(TOPK_MAX, S // 4)
    nkb = S // TK
    l = 0
    half = B_ROPE // 2

    wi = w_in[l]
    o = [0]
    for n in (A_WIDTH, A_WIDTH, A_WIDTH, IDX_HEADS * IDX_DIM, IDX_DIM, IDX_HEADS, A_WIDTH,
              Q_LORA, KV_LORA, B_ROPE, B_WIDTH, M_WIDTH, M_WIDTH):
        o.append(o[-1] + n)
    (c_qa, c_ka, c_va, c_qi, c_ki, c_wi, c_za, c_cq, c_ckv, c_kr, c_zb, c_qm, c_zm) = [
        wi[:, o[j]:o[j + 1]] for j in range(13)]
    zeros = lambda n: jnp.zeros((D, n), wi.dtype)
    w_pack = jnp.concatenate([
        c_qi,
        c_ki, c_ki,
        c_wi, zeros(LANES - IDX_HEADS),
        c_qm,
        c_za, c_zb, c_zm,
        c_qa, c_ka, c_va,
        c_cq, c_ckv,
        zeros(B_NOPE), c_kr, zeros(LANES - B_QK),
        zeros(B_NOPE), c_kr[:, half:], c_kr[:, :half], zeros(LANES - B_QK),
    ], axis=1).astype(BF16)
    assert w_pack.shape[1] == _C_END

    def rope_partner(v):
        z = jnp.zeros_like(v[..., :B_NOPE])
        return jnp.concatenate([z, v[..., B_NOPE + half:B_QK], v[..., B_NOPE:B_NOPE + half],
                                jnp.zeros_like(v[..., B_QK:])], axis=-1)

    wuq_h = jnp.pad(w_uq[l].reshape(Q_LORA, B_HEADS, B_QK), ((0, 0), (0, 0), (0, LANES - B_QK)))
    wuq_p = jnp.concatenate([wuq_h.reshape(Q_LORA, B_HEADS * LANES),
                             rope_partner(wuq_h).reshape(Q_LORA, B_HEADS * LANES)],
                            axis=1).astype(BF16)
    wukv = w_ukv[l].reshape(KV_LORA, B_HEADS, B_NOPE + B_V)
    wukv_k = jnp.pad(wukv[:, :, :B_NOPE], ((0, 0), (0, 0), (0, LANES - B_NOPE))
                     ).reshape(KV_LORA, B_HEADS * LANES).astype(BF16)
    wukv_v = wukv[:, :, B_NOPE:].reshape(KV_LORA, B_WIDTH).astype(BF16)

    idx = jnp.arange(A_WIDTH)
    gmat = jnp.where((idx[:, None] // HEAD_DIM) == (idx[None, :] // HEAD_DIM),
                     1.0 / HEAD_DIM, 0.0).astype(BF16)
    row2 = lambda v: v.reshape(1, -1).astype(F32)
    gqa_t = row2(jnp.tile(g_qa[l], A_HEADS) * (HEAD_DIM ** -0.5 * LOG2E))
    gka_t = row2(jnp.tile(g_ka[l], A_HEADS))
    gqm_t = row2(jnp.tile(g_qm[l], M_HEADS) * (HEAD_DIM ** -0.5 * LOG2E))
    gkm_t = row2(jnp.tile(g_km[l], M_HEADS))
    gqb_p = row2(jnp.pad(g_qb[l], (0, LANES - B_QK)) * (B_QK ** -0.5 * LOG2E))
    gkb_p = row2(jnp.pad(g_kb[l], (0, LANES - B_QK)))

    pos = jnp.arange(S, dtype=jnp.int32)
    inv = 1.0 / (ROPE_THETA ** (jnp.arange(half, dtype=F32) / half))
    ang = pos.astype(F32)[:, None] * inv[None, :]
    cos, sin = jnp.cos(ang), jnp.sin(ang)
    zs = lambda n: jnp.zeros((S, n), F32)
    rope_c = jnp.concatenate([jnp.ones((S, B_NOPE), F32), cos, cos, jnp.ones((S, LANES - B_QK), F32)], 1)
    rope_s = jnp.concatenate([zs(B_NOPE), -sin, sin, zs(LANES - B_QK)], 1)
    qa_tab, qb_tab = rope_c * gqb_p, rope_s * rope_partner(gqb_p)
    ka_tab, kb_tab = rope_c * gkb_p, rope_s * rope_partner(gkb_p)
    src = jnp.arange(A_WIDTH)
    place = (src[:, None] // HEAD_DIM * LANES + src[:, None] % HEAD_DIM
             == jnp.arange(A_HEADS * LANES)[None, :]).astype(BF16)

    slopes = 2.0 ** (-ALIBI_MAX * jnp.arange(1, A_HEADS + 1, dtype=F32) / A_HEADS) * LOG2E
    pieces = _bf16_pieces(slopes, N_ALIBI_PIECES)
    n_feat = 2 * N_ALIBI_PIECES
    qfeat = jnp.concatenate([jnp.zeros((A_HEADS, HEAD_DIM), F32), pieces * float(CHUNK), pieces,
                             jnp.zeros((A_HEADS, LANES - HEAD_DIM - n_feat), F32)], axis=1)
    pos_hi = jnp.repeat((pos // CHUNK).astype(F32)[:, None], N_ALIBI_PIECES, axis=1)
    pos_lo = jnp.repeat((pos % CHUNK).astype(F32)[:, None], N_ALIBI_PIECES, axis=1)
    kfeat = jnp.concatenate([zs(HEAD_DIM), pos_hi, pos_lo, zs(LANES - HEAD_DIM - n_feat)], axis=1)

    kk = jnp.arange(TK)
    ltri = (kk[None, :] <= kk[:, None]).astype(BF16)

    sds = jax.ShapeDtypeStruct
    tok = lambda w, j=0: pl.BlockSpec((None, TM, w), lambda b, t: (b, t, j))
    u = pl.pallas_call(
        _inproj_kernel,
        grid=(B, S // TM),
        in_specs=[tok(D), _full((1, D)), _full((D, _C_END))],
        out_specs=tok(_C_END),
        out_shape=sds((B, S, _C_END), BF16),
        compiler_params=_params("parallel", "parallel"),
        name="inproj",
    )(x, row2(g_in[l]), w_pack)

    grp = lambda n: pl.BlockSpec((None, n, TM, LANES), lambda b, t: (b, 0, t, 0))
    vt_out = lambda n: pl.BlockSpec((None, TM // TK, n * V_ROWS, TK), lambda b, t: (b, t, 0, 0))
    tab = pl.BlockSpec((TM, LANES), lambda b, t: (t, 0))
    qa, ka, vat, qb, kb, vbt = pl.pallas_call(
        _tokprep_kernel,
        grid=(B, S // TM),
        in_specs=[tok(_C_END), _full((A_WIDTH, A_WIDTH)), _full((A_WIDTH, A_HEADS * LANES)),
                  _full((1, A_WIDTH)), _full((1, A_WIDTH)), _full((A_HEADS, LANES)), tab,
                  _full((1, Q_LORA)), _full((Q_LORA, 2 * B_HEADS * LANES)), _full((1, KV_LORA)),
                  _full((KV_LORA, B_HEADS * LANES)), _full((KV_LORA, B_WIDTH)),
                  tab, tab, tab, tab],
        out_specs=[grp(A_HEADS), grp(A_HEADS), vt_out(A_HEADS),
                   grp(B_HEADS), grp(B_HEADS), vt_out(B_HEADS)],
        out_shape=[sds((B, A_HEADS, S, LANES), BF16), sds((B, A_HEADS, S, LANES), BF16),
                   sds((B, nkb, A_HEADS * V_ROWS, TK), BF16),
                   sds((B, B_HEADS, S, LANES), BF16), sds((B, B_HEADS, S, LANES), BF16),
                   sds((B, nkb, B_HEADS * V_ROWS, TK), BF16)],
        compiler_params=_params("parallel", "parallel"),
        name="tokprep",
    )(u, gmat, place, gqa_t, gka_t, qfeat, kfeat, row2(g_cq[l]), wuq_p,
      row2(g_ckv[l]), wukv_k, wukv_v, qa_tab, qb_tab, ka_tab, kb_tab)

    km, vm = pl.pallas_call(
        _memkv_kernel,
        grid=(B,),
        in_specs=[pl.BlockSpec((None, n_mem, D), lambda b: (b, 0, 0)), _full((1, D)),
                  _full((D, M_WIDTH)), _full((D, M_WIDTH)), _full((A_WIDTH, A_WIDTH)),
                  _full((1, M_WIDTH))],
        out_specs=[pl.BlockSpec((None, 2, n_mem, LANES), lambda b: (b, 0, 0, 0))] * 2,
        out_shape=[sds((B, 2, n_mem, LANES), BF16)] * 2,
        compiler_params=_params("parallel"),
        name="memkv",
    )(mem, row2(g_mem[l]), w_mk[l].astype(BF16), w_mv[l].astype(BF16), gmat, gkm_t)

    qtile = lambda w, j=0: pl.BlockSpec((None, TQ, w), lambda b, t: (b, t, j))
    seq = lambda n: pl.BlockSpec((None, n, S, LANES), lambda b, t: (b, 0, 0, 0))
    vt_in = lambda n: pl.BlockSpec((None, nkb, n * V_ROWS, TK), lambda b, t: (b, 0, 0, 0))
    heads = lambda n: pl.BlockSpec((None, n, TQ, LANES), lambda b, t: (b, 0, t, 0))
    flash_scratch = lambda n: [pltpu.VMEM((2, n, TK, TQ), F32), pltpu.VMEM((n, 1, TQ), F32),
                               pltpu.VMEM((n, V_ROWS, TQ), F32)]
    ya = pl.pallas_call(
        functools.partial(_dsa_kernel, topk=topk),
        grid=(B, S // TQ),
        in_specs=[pl.BlockSpec(memory_space=pltpu.SMEM),
                  qtile(IDX_HEADS * IDX_DIM, _C_QI // (IDX_HEADS * IDX_DIM)),
                  qtile(LANES, _C_WI // LANES),
                  pl.BlockSpec((None, S, LANES), lambda b, t: (b, 0, _C_KI // LANES)),
                  heads(A_HEADS), seq(A_HEADS), vt_in(A_HEADS), _full((TK, TK))],
        out_specs=qtile(A_WIDTH),
        out_shape=sds((B, S, A_WIDTH), BF16),
        scratch_shapes=[pltpu.VMEM((nkb, TK, TQ), F32), pltpu.VMEM((nkb, TK, TQ), BF16),
                        pltpu.VMEM((IDX_HEADS, TQ, LANES), BF16),
                        pltpu.VMEM((1, TQ), F32), pltpu.VMEM((SUBLANES, TQ), F32),
                        pltpu.VMEM((SUBLANES, TQ), F32)] + flash_scratch(A_HEADS),
        compiler_params=_params("parallel", "arbitrary"),
        name="dsa",
    )(slopes, u, u, u, qa, ka, vat, ltri)

    yb = pl.pallas_call(
        _mla_kernel,
        grid=(B, S // TQ),
        in_specs=[heads(B_HEADS), seq(B_HEADS), vt_in(B_HEADS)],
        out_specs=qtile(B_WIDTH),
        out_shape=sds((B, S, B_WIDTH), BF16),
        scratch_shapes=flash_scratch(B_HEADS),
        compiler_params=_params("parallel", "arbitrary"),
        name="mla",
    )(qb, kb, vbt)

    memkv = pl.BlockSpec((None, 2, n_mem, LANES), lambda b, t: (b, 0, 0, 0))
    return pl.pallas_call(
        _out_kernel,
        grid=(B, S // TM),
        in_specs=[tok(D), tok(A_WIDTH), tok(B_WIDTH), tok(D_MIX, _C_GATE // D_MIX),
                  tok(M_WIDTH, _C_QM // M_WIDTH), _full((A_WIDTH, A_WIDTH)), _full((1, M_WIDTH)),
                  memkv, memkv, _full((D, D))],
        out_specs=tok(D),
        out_shape=sds((B, S, D), x.dtype),
        compiler_params=_params("parallel", "parallel"),
        name="outproj",
    )(x, ya, yb, u, u, gmat, gqm_t, km, vm, w_out[l].astype(BF16))
```

```python
import functools
import math

import jax
import jax.numpy as jnp
from jax import lax
from jax.experimental import pallas as pl
from jax.experimental.pallas import tpu as pltpu

F32 = jnp.float32
BF16 = jnp.bfloat16

CHUNK = 64
EPS = 1e-6
HEAD_DIM = 64
A_HEADS = 6
IDX_HEADS = 8
IDX_DIM = 64
TOPK_MAX = 256
ALIBI_MAX = 8.0
B_HEADS = 6
B_NOPE = 64
B_ROPE = 32
B_V = 64
B_QK = B_NOPE + B_ROPE
Q_LORA = 256
KV_LORA = 128
ROPE_THETA = 10000.0
M_HEADS = 4
A_WIDTH = A_HEADS * HEAD_DIM
B_WIDTH = B_HEADS * B_V
M_WIDTH = M_HEADS * HEAD_DIM
D_MIX = A_WIDTH + B_WIDTH + M_WIDTH
LOG2E = math.log2(math.e)
N_ALIBI_PIECES = 3

LANES = 128
SUBLANES = 8
VMEM_LIMIT = 52 * 1024 * 1024

TM = 512
TN = 512
TQ = 256
TK = 256
N_COARSE = 8
N_FINE = 12
PACK = 2 * SUBLANES
N_ACC = 4
TINY = 1e-37
V_ROWS = 80
NEG = -1e30

_C_QI = 0
_C_KI = _C_QI + IDX_HEADS * IDX_DIM
_C_WI = _C_KI + LANES
_C_QM = _C_WI + LANES
_C_GATE = _C_QM + M_WIDTH
_C_QA = _C_GATE + D_MIX
_C_KA = _C_QA + A_WIDTH
_C_VA = _C_KA + A_WIDTH
_C_CQ = _C_VA + A_WIDTH
_C_CKV = _C_CQ + Q_LORA
_C_KR = _C_CKV + KV_LORA
_C_KRP = _C_KR + LANES
_C_END = _C_KRP + LANES


def _nt_dot(a, b):
    return lax.dot_general(a, b, (((1,), (1,)), ((), ())), preferred_element_type=F32)


def _head64_norm(u, gmat_ref, g):
    n = u.shape[1]
    msq = jnp.dot((u * u).astype(BF16), gmat_ref[:n, :n], preferred_element_type=F32)
    return u * lax.rsqrt(msq + EPS) * g


def _inproj_kernel(x_ref, gin_ref, w_ref, u_ref):
    x = x_ref[...]
    ms = jnp.mean(x * x, axis=-1, keepdims=True)
    h = (x * lax.rsqrt(ms + EPS) * gin_ref[...]).astype(BF16)
    for a in range(0, _C_END, TN):
        b = min(a + TN, _C_END)
        u_ref[:, a:b] = jnp.dot(h, w_ref[:, a:b], preferred_element_type=F32).astype(BF16)


def _tokprep_kernel(u_ref, gmat_ref, place_ref, gqa_ref, gka_ref, qfeat_ref, kfeat_ref, gcq_ref,
                    wuq_ref, gckv_ref, wukvk_ref, wukvv_ref, qa_tab_ref, qb_tab_ref, ka_tab_ref,
                    kb_tab_ref, qa_ref, ka_ref, vat_ref, qb_ref, kb_ref, vbt_ref):
    def cols(a, b):
        return u_ref[:, a:b].astype(F32)

    ones_rows = (lax.broadcasted_iota(jnp.int32, (V_ROWS - HEAD_DIM, TK), 0) == 0).astype(BF16)

    def store_transposed(dst_ref, v):
        vt = v.T.astype(BF16)
        for j in range(TM // TK):
            for hd in range(v.shape[1] // HEAD_DIM):
                r = hd * V_ROWS
                dst_ref[j, r:r + HEAD_DIM, :] = vt[hd * HEAD_DIM:(hd + 1) * HEAD_DIM,
                                                   j * TK:(j + 1) * TK]
                dst_ref[j, r + HEAD_DIM:r + V_ROWS, :] = ones_rows

    def store_heads(dst_ref, v, feat_fn):
        placed = jnp.dot(v.astype(BF16), place_ref[...], preferred_element_type=F32)
        for hd in range(A_HEADS):
            dst_ref[hd] = (placed[:, hd * LANES:(hd + 1) * LANES] + feat_fn(hd)).astype(BF16)

    store_heads(qa_ref, _head64_norm(cols(_C_QA, _C_KA), gmat_ref, gqa_ref[...]),
                lambda hd: qfeat_ref[hd:hd + 1, :])
    kfeat = kfeat_ref[...]
    store_heads(ka_ref, _head64_norm(cols(_C_KA, _C_VA), gmat_ref, gka_ref[...]), lambda hd: kfeat)
    store_transposed(vat_ref, cols(_C_VA, _C_CQ))

    def head96(uh, partner, tab_a, tab_b):
        ss = jnp.sum(uh * uh, axis=-1, keepdims=True) * (1.0 / B_QK)
        return ((uh * tab_a + partner * tab_b) * lax.rsqrt(ss + EPS)).astype(BF16)

    cq = cols(_C_CQ, _C_CKV)
    cq = cq * lax.rsqrt(jnp.mean(cq * cq, axis=-1, keepdims=True) + EPS) * gcq_ref[...]
    q = jnp.dot(cq.astype(BF16), wuq_ref[...], preferred_element_type=F32)
    qa_tab = qa_tab_ref[...]
    qb_tab = qb_tab_ref[...]
    for hd in range(B_HEADS):
        qb_ref[hd] = head96(q[:, hd * LANES:(hd + 1) * LANES],
                            q[:, (B_HEADS + hd) * LANES:(B_HEADS + hd + 1) * LANES], qa_tab, qb_tab)

    ckv = cols(_C_CKV, _C_KR)
    ckv = ckv * lax.rsqrt(jnp.mean(ckv * ckv, axis=-1, keepdims=True) + EPS) * gckv_ref[...]
    ckv = ckv.astype(BF16)
    kk = jnp.dot(ckv, wukvk_ref[...], preferred_element_type=F32)
    store_transposed(vbt_ref, jnp.dot(ckv, wukvv_ref[...], preferred_element_type=F32))
    krope = cols(_C_KR, _C_KRP)
    krope_partner = cols(_C_KRP, _C_END)
    ka_tab = ka_tab_ref[...]
    kb_tab = kb_tab_ref[...]
    for hd in range(B_HEADS):
        kb_ref[hd] = head96(kk[:, hd * LANES:(hd + 1) * LANES] + krope, krope_partner, ka_tab, kb_tab)


def _memkv_kernel(mem_ref, gmem_ref, wmk_ref, wmv_ref, gmat_ref, gkm_ref, km_ref, vm_ref):
    x = mem_ref[...]
    ms = jnp.mean(x * x, axis=-1, keepdims=True)
    m = (x * lax.rsqrt(ms + EPS) * gmem_ref[...]).astype(BF16)
    k = _head64_norm(jnp.dot(m, wmk_ref[...], preferred_element_type=F32), gmat_ref,
                     gkm_ref[...]).astype(BF16)
    v = jnp.dot(m, wmv_ref[...], preferred_element_type=F32).astype(BF16)
    for g in range(M_WIDTH // LANES):
        km_ref[g] = k[:, g * LANES:(g + 1) * LANES]
        vm_ref[g] = v[:, g * LANES:(g + 1) * LANES]


def _flash_init(m_ref, acc_ref):
    m_ref[...] = jnp.full(m_ref.shape, NEG, F32)
    acc_ref[...] = jnp.zeros(acc_ref.shape, F32)


def _flash_softmax_pv(n_heads, s_ref, vt_fn, m_ref, acc_ref):
    for hd in range(n_heads):
        s = s_ref[hd]
        m_prev = m_ref[hd]
        m_new = jnp.maximum(m_prev, jnp.max(s, axis=0, keepdims=True))
        p = jnp.exp2(s - m_new).astype(BF16)
        alpha = jnp.exp2(m_prev - m_new)
        m_ref[hd] = m_new
        acc_ref[hd] = alpha * acc_ref[hd] + jnp.dot(vt_fn(hd), p, preferred_element_type=F32)


def _flash_loop(i, diag_logits_fn, logits_fn, update_fn):
    def block(t):
        return jnp.minimum(t, i) - 1

    def consumed(t):
        return jnp.where(t == 0, i, t - 1)

    diag_logits_fn(0)

    def pair(j, carry):
        t = 2 * j
        logits_fn(block(t + 1), 1)
        update_fn(consumed(t), 0)
        logits_fn(block(t + 2), 0)
        update_fn(t, 1)
        return carry

    lax.fori_loop(0, (i + 1) // 2, pair, 0)

    @pl.when(i % 2 == 0)
    def _():
        update_fn(consumed(i), 0)


def _flash_store(out_ref, acc_ref, n_heads):
    def head(hd):
        return acc_ref[hd, :HEAD_DIM, :] / acc_ref[hd, HEAD_DIM:HEAD_DIM + 1, :]

    for g in range(n_heads // 2):
        o = jnp.concatenate([head(2 * g), head(2 * g + 1)], axis=0)
        out_ref[:, g * LANES:(g + 1) * LANES] = o.T.astype(out_ref.dtype)


def _split_head_pairs(src_ref, dst_ref, n_pairs):
    low = lax.broadcasted_iota(jnp.int32, (src_ref.shape[0], LANES), 1) < HEAD_DIM
    for g in range(n_pairs):
        pair = src_ref[:, g * LANES:(g + 1) * LANES].astype(F32)
        dst_ref[2 * g] = jnp.where(low, pair, 0.0).astype(dst_ref.dtype)
        dst_ref[2 * g + 1] = jnp.where(low, 0.0, pair).astype(dst_ref.dtype)


def _fold8(x):
    return x.reshape(x.shape[0] // SUBLANES, SUBLANES, x.shape[1])


def _dsa_kernel(slopes_ref, qi_ref, wi_ref, ki_ref, qa_ref, ka_ref, vat_ref, ltri_ref,
                out_ref,
                score_ref, sb_ref, qim_ref, lo_ref, mn_ref, mx_ref, s_ref, m_ref, acc_ref,
                *, topk):
    i = pl.program_id(1)
    nkb = i + 1
    npair = (nkb + 1) // 2
    q0 = i * TQ

    _split_head_pairs(qi_ref, qim_ref, IDX_HEADS // 2)

    kidx = lax.broadcasted_iota(jnp.int32, (TK, TQ), 0)
    qidx = lax.broadcasted_iota(jnp.int32, (TK, TQ), 1)

    wt = wi_ref[...].astype(F32).T

    def score_block(kb, diagonal):
        k0 = pl.multiple_of(kb * TK, TK)
        kblk = ki_ref[pl.ds(k0, TK), :]
        acc = jnp.zeros((TK, TQ), F32)
        for hd in range(IDX_HEADS):
            acc = acc + jnp.maximum(_nt_dot(kblk, qim_ref[hd]), 0.0) * wt[hd:hd + 1, :]
        hi_part = lo_part = _fold8(acc)
        if diagonal:
            adm = (kidx >> 6) <= (qidx >> 6)
            acc = jnp.where(adm, acc, -jnp.inf)
            hi_part = _fold8(acc)
            lo_part = _fold8(jnp.where(adm, acc, jnp.inf))
        score_ref[kb] = acc
        sb_ref[kb] = acc.astype(BF16)
        mn_ref[...] = jnp.minimum(mn_ref[...], jnp.min(lo_part, axis=0))
        mx_ref[...] = jnp.maximum(mx_ref[...], jnp.max(hi_part, axis=0))

    mn_ref[...] = jnp.full((SUBLANES, TQ), jnp.inf, F32)
    mx_ref[...] = jnp.full((SUBLANES, TQ), -jnp.inf, F32)

    def score_pair(j, carry):
        score_block(2 * j, False)
        score_block(2 * j + 1, False)
        return carry

    lax.fori_loop(0, i // 2, score_pair, 0)

    @pl.when(i % 2 == 1)
    def _():
        score_block(i - 1, False)

    score_block(i, True)

    @pl.when(nkb % 2 == 1)
    def _():
        score_ref[nkb] = jnp.full((TK, TQ), -jnp.inf, F32)
        sb_ref[nkb] = jnp.full((TK, TQ), -jnp.inf, BF16)

    qrow = q0 + lax.broadcasted_iota(jnp.int32, (1, TQ), 1)
    n_adm = ((qrow >> 6) + 1) << 6
    k_eff = jnp.minimum(n_adm, topk).astype(F32)

    lo = jnp.min(mn_ref[...], axis=0, keepdims=True)
    hi = jnp.max(mx_ref[...], axis=0, keepdims=True)

    def count_ge_coarse(t):
        tb = jnp.broadcast_to(t.astype(BF16), (PACK, TQ))
        one = jnp.ones((PACK, TQ), BF16)
        zero = jnp.zeros((PACK, TQ), BF16)

        def body(j, accs):
            accs = list(accs)
            for kb in (2 * j, 2 * j + 1):
                sb = sb_ref[kb]
                for r in range(TK // PACK):
                    hit = jnp.where(sb[r * PACK:(r + 1) * PACK, :] >= tb, one, zero)
                    accs[r % N_ACC] = accs[r % N_ACC] + hit
            return tuple(accs)

        accs = lax.fori_loop(0, npair, body, (zero,) * N_ACC)
        return jnp.sum(sum(a.astype(F32) for a in accs), axis=0, keepdims=True)

    def coarse(_, c):
        lo, hi = c
        mid = (0.5 * (lo + hi)).astype(BF16).astype(F32)
        ge = count_ge_coarse(mid) >= k_eff
        return jnp.where(ge, mid, lo), jnp.where(ge, hi, mid)

    lo, hi = lax.fori_loop(0, N_COARSE, coarse,
                           (lo.astype(BF16).astype(F32), hi.astype(BF16).astype(F32)))
    lo = lo - jnp.abs(lo) * 2.0 ** -8 - TINY
    hi = hi + jnp.abs(hi) * 2.0 ** -8 + TINY

    def count_ge(t):
        tb = jnp.broadcast_to(t, (SUBLANES, TQ))

        def body(j, accs):
            accs = list(accs)
            for kb in (2 * j, 2 * j + 1):
                s = score_ref[kb]
                for r in range(TK // SUBLANES):
                    hit = jnp.where(s[r * SUBLANES:(r + 1) * SUBLANES, :] >= tb, 1.0, 0.0)
                    accs[r % N_ACC] = accs[r % N_ACC] + hit
            return tuple(accs)

        z = jnp.zeros((SUBLANES, TQ), F32)
        accs = lax.fori_loop(0, npair, body, (z,) * N_ACC)
        return jnp.sum(sum(accs), axis=0, keepdims=True)

    def bisect(_, c):
        lo, hi, c_lo = c
        mid = 0.5 * (lo + hi)
        cnt = count_ge(mid)
        ge = cnt >= k_eff
        return jnp.where(ge, mid, lo), jnp.where(ge, hi, mid), jnp.where(ge, cnt, c_lo)

    lo, hi, c_lo = lax.fori_loop(0, N_FINE, bisect, (lo, hi, k_eff + 1.0))
    lo_ref[...] = lo

    for c in range(TQ // LANES):
        cs = slice(c * LANES, (c + 1) * LANES)

        @pl.when(jnp.max(c_lo[:, cs] - k_eff[:, cs]) > 0.0)
        def _():
            k_c = k_eff[:, cs]

            def counts(t):
                def body(j, c2):
                    ge, gt = c2
                    for kb in (2 * j, 2 * j + 1):
                        s = _fold8(score_ref[kb, :, cs])
                        ge = ge + jnp.sum(jnp.where(s >= t, 1.0, 0.0), axis=0)
                        gt = gt + jnp.sum(jnp.where(s > t, 1.0, 0.0), axis=0)
                    return ge, gt

                z = jnp.zeros((SUBLANES, LANES), F32)
                ge, gt = lax.fori_loop(0, npair, body, (z, z))
                return jnp.sum(ge, axis=0, keepdims=True), jnp.sum(gt, axis=0, keepdims=True)

            def largest(keep):
                def body(j, acc):
                    for kb in (2 * j, 2 * j + 1):
                        s = _fold8(score_ref[kb, :, cs])
                        acc = jnp.maximum(acc, jnp.max(jnp.where(keep(s), s, -jnp.inf), axis=0))
                    return acc

                acc = lax.fori_loop(0, npair, body, jnp.full((SUBLANES, LANES), -jnp.inf, F32))
                return jnp.max(acc, axis=0, keepdims=True)

            def pending(ge):
                return jnp.sum(jnp.where(ge >= k_c, 0.0, 1.0))

            def walk(c3):
                t, ge, _, _ = c3
                t = jnp.where(ge >= k_c, t, largest(lambda s: s < t))
                ge, gt = counts(t)
                return t, ge, gt, pending(ge)

            hi_c = hi[:, cs]
            t0 = largest(lambda s: s <= hi_c)
            ge0, gt0 = counts(t0)
            thr, _, c_gt, _ = lax.while_loop(lambda c3: c3[3] > 0.0, walk,
                                             (t0, ge0, gt0, pending(ge0)))
            need = k_c - c_gt

            def mark(j, seen):
                for kb in (2 * j, 2 * j + 1):
                    s = score_ref[kb, :, cs]
                    eqf = jnp.where(s == thr, 1.0, 0.0)
                    rank = seen + jnp.dot(ltri_ref[...], eqf.astype(BF16),
                                          preferred_element_type=F32)
                    tie = jnp.where(rank <= need, eqf, 0.0)
                    score_ref[kb, :, cs] = jnp.where(s > thr, 1.0, tie) * 2.0 - 1.0
                    seen = seen + jnp.sum(eqf, axis=0, keepdims=True)
                return seen

            lax.fori_loop(0, npair, mark, jnp.zeros((1, LANES), F32))
            lo_ref[:, cs] = jnp.zeros((1, LANES), F32)

    _flash_init(m_ref, acc_ref)
    lo_row = lo_ref[...]

    def logits(kb, buf, future=None):
        k0 = pl.multiple_of(kb * TK, TK)
        bias = jnp.where(score_ref[kb] >= lo_row, 0.0, NEG)
        for hd in range(A_HEADS):
            s = _nt_dot(ka_ref[hd, pl.ds(k0, TK), :], qa_ref[hd])
            s_ref[buf, hd] = s + (bias if future is None else bias - slopes_ref[hd] * future)

    def update(kb, buf):
        _flash_softmax_pv(A_HEADS, s_ref.at[buf],
                          lambda hd: vat_ref[kb, hd * V_ROWS:(hd + 1) * V_ROWS, :],
                          m_ref, acc_ref)

    _flash_loop(i, lambda buf: logits(i, buf, 2.0 * jnp.maximum(kidx - qidx, 0).astype(F32)),
                logits, update)
    _flash_store(out_ref, acc_ref, A_HEADS)


def _mla_kernel(qb_ref, kb_ref, vbt_ref, out_ref, s_ref, m_ref, acc_ref):
    i = pl.program_id(1)
    _flash_init(m_ref, acc_ref)

    def logits(kb, buf, bias=None):
        k0 = pl.multiple_of(kb * TK, TK)
        for hd in range(B_HEADS):
            s = _nt_dot(kb_ref[hd, pl.ds(k0, TK), :], qb_ref[hd])
            s_ref[buf, hd] = s if bias is None else s + bias

    def update(kb, buf):
        _flash_softmax_pv(B_HEADS, s_ref.at[buf],
                          lambda hd: vbt_ref[kb, hd * V_ROWS:(hd + 1) * V_ROWS, :],
                          m_ref, acc_ref)

    kidx = lax.broadcasted_iota(jnp.int32, (TK, TQ), 0)
    qidx = lax.broadcasted_iota(jnp.int32, (TK, TQ), 1)
    _flash_loop(i, lambda buf: logits(i, buf, jnp.where((kidx >> 6) <= (qidx >> 6), 0.0, NEG)),
                logits, update)
    _flash_store(out_ref, acc_ref, B_HEADS)


def _out_kernel(x_ref, ya_ref, yb_ref, z_ref, qm_ref, gmat_ref, gqm_ref, km_ref, vm_ref, wout_ref,
                out_ref):
    tm = x_ref.shape[0]
    low = lax.broadcasted_iota(jnp.int32, (tm, LANES), 1) < HEAD_DIM
    qm = _head64_norm(qm_ref[...].astype(F32), gmat_ref, gqm_ref[...])
    ym = []
    for g in range(M_HEADS // 2):
        pair = qm[:, g * LANES:(g + 1) * LANES]
        outs = []
        for q in (jnp.where(low, pair, 0.0), jnp.where(low, 0.0, pair)):
            s = _nt_dot(q.astype(BF16), km_ref[g])
            p = jnp.exp2(s - jnp.max(s, axis=1, keepdims=True))
            o = jnp.dot(p.astype(BF16), vm_ref[g], preferred_element_type=F32)
            outs.append(o / jnp.sum(p, axis=1, keepdims=True))
        ym.append(jnp.where(low, outs[0], outs[1]))
    y = jnp.concatenate([ya_ref[...].astype(F32), yb_ref[...].astype(F32)] + ym, axis=1)
    z = z_ref[...].astype(F32)
    y = (y * (z / (1.0 + jnp.exp(-z)))).astype(BF16)
    out_ref[...] = x_ref[...] + jnp.dot(y, wout_ref[...], preferred_element_type=F32)


def _full(shape):
    n = len(shape)
    return pl.BlockSpec(shape, lambda *_: (0,) * n)


def _params(*sem):
    return pltpu.CompilerParams(dimension_semantics=sem, vmem_limit_bytes=VMEM_LIMIT)


def _bf16_pieces(v, n):
    out, rest = [], v
    for _ in range(n):
        piece = rest.astype(BF16).astype(F32)
        out.append(piece)
        rest = rest - piece
    return jnp.stack(out, axis=-1)


def kernel(x, mem, g_in, w_in, w_uq, g_cq, w_ukv, g_ckv, g_qa, g_ka, g_qb, g_kb,
           g_mem, w_mk, w_mv, g_qm, g_km, w_out):
    B, S, D = x.shape
    n_mem = mem.shape[1]
    assert S % TM == 0 and TM % TK == 0 and TQ == TK and D == D_MIX and (S // TK) % 2 == 0
    assert S // PACK <= 256
    assert _C_QI == 0 and _C_KI % LANES == 0 and _C_WI % LANES == 0
    assert _C_QM % M_WIDTH == 0 and _C_GATE % D_MIX == 0
    topk = min(TOPK_MAX, S // 4)
    nkb = S // TK
    l = 0
    half = B_ROPE // 2

    wi = w_in[l]
    o = [0]
    for n in (A_WIDTH, A_WIDTH, A_WIDTH, IDX_HEADS * IDX_DIM, IDX_DIM, IDX_HEADS, A_WIDTH,
              Q_LORA, KV_LORA, B_ROPE, B_WIDTH, M_WIDTH, M_WIDTH):
        o.append(o[-1] + n)
    (c_qa, c_ka, c_va, c_qi, c_ki, c_wi, c_za, c_cq, c_ckv, c_kr, c_zb, c_qm, c_zm) = [
        wi[:, o[j]:o[j + 1]] for j in range(13)]
    zeros = lambda n: jnp.zeros((D, n), wi.dtype)
    w_pack = jnp.concatenate([
        c_qi,
        c_ki, c_ki,
        c_wi, zeros(LANES - IDX_HEADS),
        c_qm,
        c_za, c_zb, c_zm,
        c_qa, c_ka, c_va,
        c_cq, c_ckv,
        zeros(B_NOPE), c_kr, zeros(LANES - B_QK),
        zeros(B_NOPE), c_kr[:, half:], c_kr[:, :half], zeros(LANES - B_QK),
    ], axis=1).astype(BF16)
    assert w_pack.shape[1] == _C_END

    def rope_partner(v):
        z = jnp.zeros_like(v[..., :B_NOPE])
        return jnp.concatenate([z, v[..., B_NOPE + half:B_QK], v[..., B_NOPE:B_NOPE + half],
                                jnp.zeros_like(v[..., B_QK:])], axis=-1)

    wuq_h = jnp.pad(w_uq[l].reshape(Q_LORA, B_HEADS, B_QK), ((0, 0), (0, 0), (0, LANES - B_QK)))
    wuq_p = jnp.concatenate([wuq_h.reshape(Q_LORA, B_HEADS * LANES),
                             rope_partner(wuq_h).reshape(Q_LORA, B_HEADS * LANES)],
                            axis=1).astype(BF16)
    wukv = w_ukv[l].reshape(KV_LORA, B_HEADS, B_NOPE + B_V)
    wukv_k = jnp.pad(wukv[:, :, :B_NOPE], ((0, 0), (0, 0), (0, LANES - B_NOPE))
                     ).reshape(KV_LORA, B_HEADS * LANES).astype(BF16)
    wukv_v = wukv[:, :, B_NOPE:].reshape(KV_LORA, B_WIDTH).astype(BF16)

    idx = jnp.arange(A_WIDTH)
    gmat = jnp.where((idx[:, None] // HEAD_DIM) == (idx[None, :] // HEAD_DIM),
                     1.0 / HEAD_DIM, 0.0).astype(BF16)
    row2 = lambda v: v.reshape(1, -1).astype(F32)
    gqa_t = row2(jnp.tile(g_qa[l], A_HEADS) * (HEAD_DIM ** -0.5 * LOG2E))
    gka_t = row2(jnp.tile(g_ka[l], A_HEADS))
    gqm_t = row2(jnp.tile(g_qm[l], M_HEADS) * (HEAD_DIM ** -0.5 * LOG2E))
    gkm_t = row2(jnp.tile(g_km[l], M_HEADS))
    gqb_p = row2(jnp.pad(g_qb[l], (0, LANES - B_QK)) * (B_QK ** -0.5 * LOG2E))
    gkb_p = row2(jnp.pad(g_kb[l], (0, LANES - B_QK)))

    pos = jnp.arange(S, dtype=jnp.int32)
    inv = 1.0 / (ROPE_THETA ** (jnp.arange(half, dtype=F32) / half))
    ang = pos.astype(F32)[:, None] * inv[None, :]
    cos, sin = jnp.cos(ang), jnp.sin(ang)
    zs = lambda n: jnp.zeros((S, n), F32)
    rope_c = jnp.concatenate([jnp.ones((S, B_NOPE), F32), cos, cos, jnp.ones((S, LANES - B_QK), F32)], 1)
    rope_s = jnp.concatenate([zs(B_NOPE), -sin, sin, zs(LANES - B_QK)], 1)
    qa_tab, qb_tab = rope_c * gqb_p, rope_s * rope_partner(gqb_p)
    ka_tab, kb_tab = rope_c * gkb_p, rope_s * rope_partner(gkb_p)
    src = jnp.arange(A_WIDTH)
    place = (src[:, None] // HEAD_DIM * LANES + src[:, None] % HEAD_DIM
             == jnp.arange(A_HEADS * LANES)[None, :]).astype(BF16)

    slopes = 2.0 ** (-ALIBI_MAX * jnp.arange(1, A_HEADS + 1, dtype=F32) / A_HEADS) * LOG2E
    pieces = _bf16_pieces(slopes, N_ALIBI_PIECES)
    n_feat = 2 * N_ALIBI_PIECES
    qfeat = jnp.concatenate([jnp.zeros((A_HEADS, HEAD_DIM), F32), pieces * float(CHUNK), pieces,
                             jnp.zeros((A_HEADS, LANES - HEAD_DIM - n_feat), F32)], axis=1)
    pos_hi = jnp.repeat((pos // CHUNK).astype(F32)[:, None], N_ALIBI_PIECES, axis=1)
    pos_lo = jnp.repeat((pos % CHUNK).astype(F32)[:, None], N_ALIBI_PIECES, axis=1)
    kfeat = jnp.concatenate([zs(HEAD_DIM), pos_hi, pos_lo, zs(LANES - HEAD_DIM - n_feat)], axis=1)

    kk = jnp.arange(TK)
    ltri = (kk[None, :] <= kk[:, None]).astype(BF16)

    sds = jax.ShapeDtypeStruct
    tok = lambda w, j=0: pl.BlockSpec((None, TM, w), lambda b, t: (b, t, j))
    u = pl.pallas_call(
        _inproj_kernel,
        grid=(B, S // TM),
        in_specs=[tok(D), _full((1, D)), _full((D, _C_END))],
        out_specs=tok(_C_END),
        out_shape=sds((B, S, _C_END), BF16),
        compiler_params=_params("parallel", "parallel"),
        name="inproj",
    )(x, row2(g_in[l]), w_pack)

    grp = lambda n: pl.BlockSpec((None, n, TM, LANES), lambda b, t: (b, 0, t, 0))
    vt_out = lambda n: pl.BlockSpec((None, TM // TK, n * V_ROWS, TK), lambda b, t: (b, t, 0, 0))
    tab = pl.BlockSpec((TM, LANES), lambda b, t: (t, 0))
    qa, ka, vat, qb, kb, vbt = pl.pallas_call(
        _tokprep_kernel,
        grid=(B, S // TM),
        in_specs=[tok(_C_END), _full((A_WIDTH, A_WIDTH)), _full((A_WIDTH, A_HEADS * LANES)),
                  _full((1, A_WIDTH)), _full((1, A_WIDTH)), _full((A_HEADS, LANES)), tab,
                  _full((1, Q_LORA)), _full((Q_LORA, 2 * B_HEADS * LANES)), _full((1, KV_LORA)),
                  _full((KV_LORA, B_HEADS * LANES)), _full((KV_LORA, B_WIDTH)),
                  tab, tab, tab, tab],
        out_specs=[grp(A_HEADS), grp(A_HEADS), vt_out(A_HEADS),
                   grp(B_HEADS), grp(B_HEADS), vt_out(B_HEADS)],
        out_shape=[sds((B, A_HEADS, S, LANES), BF16), sds((B, A_HEADS, S, LANES), BF16),
                   sds((B, nkb, A_HEADS * V_ROWS, TK), BF16),
                   sds((B, B_HEADS, S, LANES), BF16), sds((B, B_HEADS, S, LANES), BF16),
                   sds((B, nkb, B_HEADS * V_ROWS, TK), BF16)],
        compiler_params=_params("parallel", "parallel"),
        name="tokprep",
    )(u, gmat, place, gqa_t, gka_t, qfeat, kfeat, row2(g_cq[l]), wuq_p,
      row2(g_ckv[l]), wukv_k, wukv_v, qa_tab, qb_tab, ka_tab, kb_tab)

    km, vm = pl.pallas_call(
        _memkv_kernel,
        grid=(B,),
        in_specs=[pl.BlockSpec((None, n_mem, D), lambda b: (b, 0, 0)), _full((1, D)),
                  _full((D, M_WIDTH)), _full((D, M_WIDTH)), _full((A_WIDTH, A_WIDTH)),
                  _full((1, M_WIDTH))],
        out_specs=[pl.BlockSpec((None, 2, n_mem, LANES), lambda b: (b, 0, 0, 0))] * 2,
        out_shape=[sds((B, 2, n_mem, LANES), BF16)] * 2,
        compiler_params=_params("parallel"),
        name="memkv",
    )(mem, row2(g_mem[l]), w_mk[l].astype(BF16), w_mv[l].astype(BF16), gmat, gkm_t)

    qtile = lambda w, j=0: pl.BlockSpec((None, TQ, w), lambda b, t: (b, t, j))
    seq = lambda n: pl.BlockSpec((None, n, S, LANES), lambda b, t: (b, 0, 0, 0))
    vt_in = lambda n: pl.BlockSpec((None, nkb, n * V_ROWS, TK), lambda b, t: (b, 0, 0, 0))
    heads = lambda n: pl.BlockSpec((None, n, TQ, LANES), lambda b, t: (b, 0, t, 0))
    flash_scratch = lambda n: [pltpu.VMEM((2, n, TK, TQ), F32), pltpu.VMEM((n, 1, TQ), F32),
                               pltpu.VMEM((n, V_ROWS, TQ), F32)]
    ya = pl.pallas_call(
        functools.partial(_dsa_kernel, topk=topk),
        grid=(B, S // TQ),
        in_specs=[pl.BlockSpec(memory_space=pltpu.SMEM),
                  qtile(IDX_HEADS * IDX_DIM, _C_QI // (IDX_HEADS * IDX_DIM)),
                  qtile(LANES, _C_WI // LANES),
                  pl.BlockSpec((None, S, LANES), lambda b, t: (b, 0, _C_KI // LANES)),
                  heads(A_HEADS), seq(A_HEADS), vt_in(A_HEADS), _full((TK, TK))],
        out_specs=qtile(A_WIDTH),
        out_shape=sds((B, S, A_WIDTH), BF16),
        scratch_shapes=[pltpu.VMEM((nkb, TK, TQ), F32), pltpu.VMEM((nkb, TK, TQ), BF16),
                        pltpu.VMEM((IDX_HEADS, TQ, LANES), BF16),
                        pltpu.VMEM((1, TQ), F32), pltpu.VMEM((SUBLANES, TQ), F32),
                        pltpu.VMEM((SUBLANES, TQ), F32)] + flash_scratch(A_HEADS),
        compiler_params=_params("parallel", "arbitrary"),
        name="dsa",
    )(slopes, u, u, u, qa, ka, vat, ltri)

    yb = pl.pallas_call(
        _mla_kernel,
        grid=(B, S // TQ),
        in_specs=[heads(B_HEADS), seq(B_HEADS), vt_in(B_HEADS)],
        out_specs=qtile(B_WIDTH),
        out_shape=sds((B, S, B_WIDTH), BF16),
        scratch_shapes=flash_scratch(B_HEADS),
        compiler_params=_params("parallel", "arbitrary"),
        name="mla",
    )(qb, kb, vbt)

    memkv = pl.BlockSpec((None, 2, n_mem, LANES), lambda b, t: (b, 0, 0, 0))
    return pl.pallas_call(
        _out_kernel,
        grid=(B, S // TM),
        in_specs=[tok(D), tok(A_WIDTH), tok(B_WIDTH), tok(D_MIX, _C_GATE // D_MIX),
                  tok(M_WIDTH, _C_QM // M_WIDTH), _full((A_WIDTH, A_WIDTH)), _full((1, M_WIDTH)),
                  memkv, memkv, _full((D, D))],
        out_specs=tok(D),
        out_shape=sds((B, S, D), x.dtype),
        compiler_params=_params("parallel", "parallel"),
        name="outproj",
    )(x, ya, yb, u, u, gmat, gqm_t, km, vm, w_out[l].astype(BF16))
```

```python
import functools
import math

import jax
import jax.numpy as jnp
from jax import lax
from jax.experimental import pallas as pl
from jax.experimental.pallas import tpu as pltpu

F32 = jnp.float32
BF16 = jnp.bfloat16

CHUNK = 64
EPS = 1e-6
HEAD_DIM = 64
A_HEADS = 6
IDX_HEADS = 8
IDX_DIM = 64
TOPK_MAX = 256
ALIBI_MAX = 8.0
B_HEADS = 6
B_NOPE = 64
B_ROPE = 32
B_V = 64
B_QK = B_NOPE + B_ROPE
Q_LORA = 256
KV_LORA = 128
ROPE_THETA = 10000.0
M_HEADS = 4
A_WIDTH = A_HEADS * HEAD_DIM
B_WIDTH = B_HEADS * B_V
M_WIDTH = M_HEADS * HEAD_DIM
D_MIX = A_WIDTH + B_WIDTH + M_WIDTH
LOG2E = math.log2(math.e)
N_ALIBI_PIECES = 3

LANES = 128
SUBLANES = 8
VMEM_LIMIT = 52 * 1024 * 1024

TM = 1024
TN = 512
TQ = 256
TK = 256
N_COARSE = 8
N_FINE = 12
PACK = 2 * SUBLANES
N_ACC = 4
TINY = 1e-37
V_ROWS = 80
NEG = -1e30

_C_QI = 0
_C_KI = _C_QI + IDX_HEADS * IDX_DIM
_C_WI = _C_KI + LANES
_C_QM = _C_WI + LANES
_C_GATE = _C_QM + M_WIDTH
_C_QA = _C_GATE + D_MIX
_C_KA = _C_QA + A_WIDTH
_C_VA = _C_KA + A_WIDTH
_C_CQ = _C_VA + A_WIDTH
_C_CKV = _C_CQ + Q_LORA
_C_KR = _C_CKV + KV_LORA
_C_KRP = _C_KR + LANES
_C_END = _C_KRP + LANES


def _nt_dot(a, b):
    return lax.dot_general(a, b, (((1,), (1,)), ((), ())), preferred_element_type=F32)


def _head64_norm(u, gmat_ref, g):
    n = u.shape[1]
    msq = jnp.dot((u * u).astype(BF16), gmat_ref[:n, :n], preferred_element_type=F32)
    return u * lax.rsqrt(msq + EPS) * g


def _inproj_kernel(x_ref, gin_ref, w_ref, u_ref):
    x = x_ref[...]
    ms = jnp.mean(x * x, axis=-1, keepdims=True)
    h = (x * lax.rsqrt(ms + EPS) * gin_ref[...]).astype(BF16)
    for a in range(0, _C_END, TN):
        b = min(a + TN, _C_END)
        u_ref[:, a:b] = jnp.dot(h, w_ref[:, a:b], preferred_element_type=F32).astype(BF16)


def _tokprep_kernel(u_ref, gmat_ref, place_ref, gqa_ref, gka_ref, qfeat_ref, kfeat_ref, gcq_ref,
                    wuq_ref, gckv_ref, wukvk_ref, wukvv_ref, qa_tab_ref, qb_tab_ref, ka_tab_ref,
                    kb_tab_ref, qa_ref, ka_ref, vat_ref, qb_ref, kb_ref, vbt_ref):
    def cols(a, b):
        return u_ref[:, a:b].astype(F32)

    ones_rows = (lax.broadcasted_iota(jnp.int32, (V_ROWS - HEAD_DIM, TK), 0) == 0).astype(BF16)

    def store_transposed(dst_ref, v):
        vt = v.T.astype(BF16)
        for j in range(TM // TK):
            for hd in range(v.shape[1] // HEAD_DIM):
                r = hd * V_ROWS
                dst_ref[j, r:r + HEAD_DIM, :] = vt[hd * HEAD_DIM:(hd + 1) * HEAD_DIM,
                                                   j * TK:(j + 1) * TK]
                dst_ref[j, r + HEAD_DIM:r + V_ROWS, :] = ones_rows

    def store_heads(dst_ref, v, feat_fn):
        placed = jnp.dot(v.astype(BF16), place_ref[...], preferred_element_type=F32)
        for hd in range(A_HEADS):
            dst_ref[hd] = (placed[:, hd * LANES:(hd + 1) * LANES] + feat_fn(hd)).astype(BF16)

    store_heads(qa_ref, _head64_norm(cols(_C_QA, _C_KA), gmat_ref, gqa_ref[...]),
                lambda hd: qfeat_ref[hd:hd + 1, :])
    kfeat = kfeat_ref[...]
    store_heads(ka_ref, _head64_norm(cols(_C_KA, _C_VA), gmat_ref, gka_ref[...]), lambda hd: kfeat)
    store_transposed(vat_ref, cols(_C_VA, _C_CQ))

    def head96(uh, partner, tab_a, tab_b):
        ss = jnp.sum(uh * uh, axis=-1, keepdims=True) * (1.0 / B_QK)
        return ((uh * tab_a + partner * tab_b) * lax.rsqrt(ss + EPS)).astype(BF16)

    cq = cols(_C_CQ, _C_CKV)
    cq = cq * lax.rsqrt(jnp.mean(cq * cq, axis=-1, keepdims=True) + EPS) * gcq_ref[...]
    q = jnp.dot(cq.astype(BF16), wuq_ref[...], preferred_element_type=F32)
    qa_tab = qa_tab_ref[...]
    qb_tab = qb_tab_ref[...]
    for hd in range(B_HEADS):
        qb_ref[hd] = head96(q[:, hd * LANES:(hd + 1) * LANES],
                            q[:, (B_HEADS + hd) * LANES:(B_HEADS + hd + 1) * LANES], qa_tab, qb_tab)

    ckv = cols(_C_CKV, _C_KR)
    ckv = ckv * lax.rsqrt(jnp.mean(ckv * ckv, axis=-1, keepdims=True) + EPS) * gckv_ref[...]
    ckv = ckv.astype(BF16)
    kk = jnp.dot(ckv, wukvk_ref[...], preferred_element_type=F32)
    store_transposed(vbt_ref, jnp.dot(ckv, wukvv_ref[...], preferred_element_type=F32))
    krope = cols(_C_KR, _C_KRP)
    krope_partner = cols(_C_KRP, _C_END)
    ka_tab = ka_tab_ref[...]
    kb_tab = kb_tab_ref[...]
    for hd in range(B_HEADS):
        kb_ref[hd] = head96(kk[:, hd * LANES:(hd + 1) * LANES] + krope, krope_partner, ka_tab, kb_tab)


def _memkv_kernel(mem_ref, gmem_ref, wmk_ref, wmv_ref, gmat_ref, gkm_ref, km_ref, vm_ref):
    x = mem_ref[...]
    ms = jnp.mean(x * x, axis=-1, keepdims=True)
    m = (x * lax.rsqrt(ms + EPS) * gmem_ref[...]).astype(BF16)
    k = _head64_norm(jnp.dot(m, wmk_ref[...], preferred_element_type=F32), gmat_ref,
                     gkm_ref[...]).astype(BF16)
    v = jnp.dot(m, wmv_ref[...], preferred_element_type=F32).astype(BF16)
    for g in range(M_WIDTH // LANES):
        km_ref[g] = k[:, g * LANES:(g + 1) * LANES]
        vm_ref[g] = v[:, g * LANES:(g + 1) * LANES]


def _flash_init(m_ref, acc_ref):
    m_ref[...] = jnp.full(m_ref.shape, NEG, F32)
    acc_ref[...] = jnp.zeros(acc_ref.shape, F32)


def _flash_softmax_pv(n_heads, s_ref, vt_fn, m_ref, acc_ref):
    for hd in range(n_heads):
        s = s_ref[hd]
        m_prev = m_ref[hd]
        m_new = jnp.maximum(m_prev, jnp.max(s, axis=0, keepdims=True))
        p = jnp.exp2(s - m_new).astype(BF16)
        alpha = jnp.exp2(m_prev - m_new)
        m_ref[hd] = m_new
        acc_ref[hd] = alpha * acc_ref[hd] + jnp.dot(vt_fn(hd), p, preferred_element_type=F32)


def _flash_loop(i, diag_logits_fn, logits_fn, update_fn):
    def block(t):
        return jnp.minimum(t, i) - 1

    def consumed(t):
        return jnp.where(t == 0, i, t - 1)

    diag_logits_fn(0)

    def pair(j, carry):
        t = 2 * j
        logits_fn(block(t + 1), 1)
        update_fn(consumed(t), 0)
        logits_fn(block(t + 2), 0)
        update_fn(t, 1)
        return carry

    lax.fori_loop(0, (i + 1) // 2, pair, 0)

    @pl.when(i % 2 == 0)
    def _():
        update_fn(consumed(i), 0)


def _flash_store(out_ref, acc_ref, n_heads):
    def head(hd):
        return acc_ref[hd, :HEAD_DIM, :] / acc_ref[hd, HEAD_DIM:HEAD_DIM + 1, :]

    for g in range(n_heads // 2):
        o = jnp.concatenate([head(2 * g), head(2 * g + 1)], axis=0)
        out_ref[:, g * LANES:(g + 1) * LANES] = o.T.astype(out_ref.dtype)


def _split_head_pairs(src_ref, dst_ref, n_pairs):
    low = lax.broadcasted_iota(jnp.int32, (src_ref.shape[0], LANES), 1) < HEAD_DIM
    for g in range(n_pairs):
        pair = src_ref[:, g * LANES:(g + 1) * LANES].astype(F32)
        dst_ref[2 * g] = jnp.where(low, pair, 0.0).astype(dst_ref.dtype)
        dst_ref[2 * g + 1] = jnp.where(low, 0.0, pair).astype(dst_ref.dtype)


def _fold8(x):
    return x.reshape(x.shape[0] // SUBLANES, SUBLANES, x.shape[1])


def _dsa_kernel(slopes_ref, qi_ref, wi_ref, ki_ref, qa_ref, ka_ref, vat_ref, ltri_ref,
                out_ref,
                score_ref, sb_ref, qim_ref, lo_ref, mn_ref, mx_ref, s_ref, m_ref, acc_ref,
                *, topk):
    i = pl.program_id(1)
    nkb = i + 1
    npair = (nkb + 1) // 2
    q0 = i * TQ

    _split_head_pairs(qi_ref, qim_ref, IDX_HEADS // 2)

    kidx = lax.broadcasted_iota(jnp.int32, (TK, TQ), 0)
    qidx = lax.broadcasted_iota(jnp.int32, (TK, TQ), 1)

    wt = wi_ref[...].astype(F32).T

    def score_block(kb, diagonal):
        k0 = pl.multiple_of(kb * TK, TK)
        kblk = ki_ref[pl.ds(k0, TK), :]
        acc = jnp.zeros((TK, TQ), F32)
        for hd in range(IDX_HEADS):
            acc = acc + jnp.maximum(_nt_dot(kblk, qim_ref[hd]), 0.0) * wt[hd:hd + 1, :]
        hi_part = lo_part = _fold8(acc)
        if diagonal:
            adm = (kidx >> 6) <= (qidx >> 6)
            acc = jnp.where(adm, acc, -jnp.inf)
            hi_part = _fold8(acc)
            lo_part = _fold8(jnp.where(adm, acc, jnp.inf))
        score_ref[kb] = acc
        sb_ref[kb] = acc.astype(BF16)
        mn_ref[...] = jnp.minimum(mn_ref[...], jnp.min(lo_part, axis=0))
        mx_ref[...] = jnp.maximum(mx_ref[...], jnp.max(hi_part, axis=0))

    mn_ref[...] = jnp.full((SUBLANES, TQ), jnp.inf, F32)
    mx_ref[...] = jnp.full((SUBLANES, TQ), -jnp.inf, F32)

    def score_pair(j, carry):
        score_block(2 * j, False)
        score_block(2 * j + 1, False)
        return carry

    lax.fori_loop(0, i // 2, score_pair, 0)

    @pl.when(i % 2 == 1)
    def _():
        score_block(i - 1, False)

    score_block(i, True)

    @pl.when(nkb % 2 == 1)
    def _():
        score_ref[nkb] = jnp.full((TK, TQ), -jnp.inf, F32)
        sb_ref[nkb] = jnp.full((TK, TQ), -jnp.inf, BF16)

    qrow = q0 + lax.broadcasted_iota(jnp.int32, (1, TQ), 1)
    n_adm = ((qrow >> 6) + 1) << 6
    k_eff = jnp.minimum(n_adm, topk).astype(F32)

    lo = jnp.min(mn_ref[...], axis=0, keepdims=True)
    hi = jnp.max(mx_ref[...], axis=0, keepdims=True)

    def count_ge_coarse(t):
        tb = jnp.broadcast_to(t.astype(BF16), (PACK, TQ))
        one = jnp.ones((PACK, TQ), BF16)
        zero = jnp.zeros((PACK, TQ), BF16)

        def body(j, accs):
            accs = list(accs)
            for kb in (2 * j, 2 * j + 1):
                sb = sb_ref[kb]
                for r in range(TK // PACK):
                    hit = jnp.where(sb[r * PACK:(r + 1) * PACK, :] >= tb, one, zero)
                    accs[r % N_ACC] = accs[r % N_ACC] + hit
            return tuple(accs)

        accs = lax.fori_loop(0, npair, body, (zero,) * N_ACC)
        return jnp.sum(sum(a.astype(F32) for a in accs), axis=0, keepdims=True)

    def coarse(_, c):
        lo, hi = c
        mid = (0.5 * (lo + hi)).astype(BF16).astype(F32)
        ge = count_ge_coarse(mid) >= k_eff
        return jnp.where(ge, mid, lo), jnp.where(ge, hi, mid)

    lo, hi = lax.fori_loop(0, N_COARSE, coarse,
                           (lo.astype(BF16).astype(F32), hi.astype(BF16).astype(F32)))
    lo = lo - jnp.abs(lo) * 2.0 ** -8 - TINY
    hi = hi + jnp.abs(hi) * 2.0 ** -8 + TINY

    def count_ge(t):
        tb = jnp.broadcast_to(t, (SUBLANES, TQ))

        def body(j, accs):
            accs = list(accs)
            for kb in (2 * j, 2 * j + 1):
                s = score_ref[kb]
                for r in range(TK // SUBLANES):
                    hit = jnp.where(s[r * SUBLANES:(r + 1) * SUBLANES, :] >= tb, 1.0, 0.0)
                    accs[r % N_ACC] = accs[r % N_ACC] + hit
            return tuple(accs)

        z = jnp.zeros((SUBLANES, TQ), F32)
        accs = lax.fori_loop(0, npair, body, (z,) * N_ACC)
        return jnp.sum(sum(accs), axis=0, keepdims=True)

    def bisect(_, c):
        lo, hi, c_lo = c
        mid = 0.5 * (lo + hi)
        cnt = count_ge(mid)
        ge = cnt >= k_eff
        return jnp.where(ge, mid, lo), jnp.where(ge, hi, mid), jnp.where(ge, cnt, c_lo)

    c_lo = jnp.where(n_adm <= topk, k_eff, k_eff + 1.0)
    lo, hi, c_lo = lax.fori_loop(0, N_FINE, bisect, (lo, hi, c_lo))
    lo_ref[...] = lo

    for c in range(TQ // LANES):
        cs = slice(c * LANES, (c + 1) * LANES)

        @pl.when(jnp.max(c_lo[:, cs] - k_eff[:, cs]) > 0.0)
        def _():
            k_c = k_eff[:, cs]

            def counts(t):
                def body(j, c2):
                    ge, gt = c2
                    for kb in (2 * j, 2 * j + 1):
                        s = _fold8(score_ref[kb, :, cs])
                        ge = ge + jnp.sum(jnp.where(s >= t, 1.0, 0.0), axis=0)
                        gt = gt + jnp.sum(jnp.where(s > t, 1.0, 0.0), axis=0)
                    return ge, gt

                z = jnp.zeros((SUBLANES, LANES), F32)
                ge, gt = lax.fori_loop(0, npair, body, (z, z))
                return jnp.sum(ge, axis=0, keepdims=True), jnp.sum(gt, axis=0, keepdims=True)

            def largest(keep):
                def body(j, acc):
                    for kb in (2 * j, 2 * j + 1):
                        s = _fold8(score_ref[kb, :, cs])
                        acc = jnp.maximum(acc, jnp.max(jnp.where(keep(s), s, -jnp.inf), axis=0))
                    return acc

                acc = lax.fori_loop(0, npair, body, jnp.full((SUBLANES, LANES), -jnp.inf, F32))
                return jnp.max(acc, axis=0, keepdims=True)

            def pending(ge):
                return jnp.sum(jnp.where(ge >= k_c, 0.0, 1.0))

            def walk(c3):
                t, ge, _, _ = c3
                t = jnp.where(ge >= k_c, t, largest(lambda s: s < t))
                ge, gt = counts(t)
                return t, ge, gt, pending(ge)

            hi_c = hi[:, cs]
            t0 = largest(lambda s: s <= hi_c)
            ge0, gt0 = counts(t0)
            thr, _, c_gt, _ = lax.while_loop(lambda c3: c3[3] > 0.0, walk,
                                             (t0, ge0, gt0, pending(ge0)))
            need = k_c - c_gt

            def mark(j, seen):
                for kb in (2 * j, 2 * j + 1):
                    s = score_ref[kb, :, cs]
                    eqf = jnp.where(s == thr, 1.0, 0.0)
                    rank = seen + jnp.dot(ltri_ref[...], eqf.astype(BF16),
                                          preferred_element_type=F32)
                    tie = jnp.where(rank <= need, eqf, 0.0)
                    score_ref[kb, :, cs] = jnp.where(s > thr, 1.0, tie) * 2.0 - 1.0
                    seen = seen + jnp.sum(eqf, axis=0, keepdims=True)
                return seen

            lax.fori_loop(0, npair, mark, jnp.zeros((1, LANES), F32))
            lo_ref[:, cs] = jnp.zeros((1, LANES), F32)

    _flash_init(m_ref, acc_ref)
    lo_row = lo_ref[...]

    def logits(kb, buf, future=None):
        k0 = pl.multiple_of(kb * TK, TK)
        bias = jnp.where(score_ref[kb] >= lo_row, 0.0, NEG)
        for hd in range(A_HEADS):
            s = _nt_dot(ka_ref[hd, pl.ds(k0, TK), :], qa_ref[hd])
            s_ref[buf, hd] = s + (bias if future is None else bias - slopes_ref[hd] * future)

    def update(kb, buf):
        _flash_softmax_pv(A_HEADS, s_ref.at[buf],
                          lambda hd: vat_ref[kb, hd * V_ROWS:(hd + 1) * V_ROWS, :],
                          m_ref, acc_ref)

    _flash_loop(i, lambda buf: logits(i, buf, 2.0 * jnp.maximum(kidx - qidx, 0).astype(F32)),
                logits, update)
    _flash_store(out_ref, acc_ref, A_HEADS)


def _mla_kernel(qb_ref, kb_ref, vbt_ref, out_ref, s_ref, m_ref, acc_ref):
    i = pl.program_id(1)
    _flash_init(m_ref, acc_ref)

    def logits(kb, buf, bias=None):
        k0 = pl.multiple_of(kb * TK, TK)
        for hd in range(B_HEADS):
            s = _nt_dot(kb_ref[hd, pl.ds(k0, TK), :], qb_ref[hd])
            s_ref[buf, hd] = s if bias is None else s + bias

    def update(kb, buf):
        _flash_softmax_pv(B_HEADS, s_ref.at[buf],
                          lambda hd: vbt_ref[kb, hd * V_ROWS:(hd + 1) * V_ROWS, :],
                          m_ref, acc_ref)

    kidx = lax.broadcasted_iota(jnp.int32, (TK, TQ), 0)
    qidx = lax.broadcasted_iota(jnp.int32, (TK, TQ), 1)
    _flash_loop(i, lambda buf: logits(i, buf, jnp.where((kidx >> 6) <= (qidx >> 6), 0.0, NEG)),
                logits, update)
    _flash_store(out_ref, acc_ref, B_HEADS)


def _out_kernel(x_ref, ya_ref, yb_ref, z_ref, qm_ref, gmat_ref, gqm_ref, km_ref, vm_ref, wout_ref,
                out_ref):
    tm = x_ref.shape[0]
    low = lax.broadcasted_iota(jnp.int32, (tm, LANES), 1) < HEAD_DIM
    qm = _head64_norm(qm_ref[...].astype(F32), gmat_ref, gqm_ref[...])
    ym = []
    for g in range(M_HEADS // 2):
        pair = qm[:, g * LANES:(g + 1) * LANES]
        outs = []
        for q in (jnp.where(low, pair, 0.0), jnp.where(low, 0.0, pair)):
            s = _nt_dot(q.astype(BF16), km_ref[g])
            p = jnp.exp2(s - jnp.max(s, axis=1, keepdims=True))
            o = jnp.dot(p.astype(BF16), vm_ref[g], preferred_element_type=F32)
            outs.append(o / jnp.sum(p, axis=1, keepdims=True))
        ym.append(jnp.where(low, outs[0], outs[1]))
    y = jnp.concatenate([ya_ref[...].astype(F32), yb_ref[...].astype(F32)] + ym, axis=1)
    z = z_ref[...].astype(F32)
    y = (y * (z / (1.0 + jnp.exp(-z)))).astype(BF16)
    out_ref[...] = x_ref[...] + jnp.dot(y, wout_ref[...], preferred_element_type=F32)


def _full(shape):
    n = len(shape)
    return pl.BlockSpec(shape, lambda *_: (0,) * n)


def _params(*sem):
    return pltpu.CompilerParams(dimension_semantics=sem, vmem_limit_bytes=VMEM_LIMIT)


def _bf16_pieces(v, n):
    out, rest = [], v
    for _ in range(n):
        piece = rest.astype(BF16).astype(F32)
        out.append(piece)
        rest = rest - piece
    return jnp.stack(out, axis=-1)


def kernel(x, mem, g_in, w_in, w_uq, g_cq, w_ukv, g_ckv, g_qa, g_ka, g_qb, g_kb,
           g_mem, w_mk, w_mv, g_qm, g_km, w_out):
    B, S, D = x.shape
    n_mem = mem.shape[1]
    assert S % TM == 0 and TM % TK == 0 and TQ == TK and D == D_MIX and (S // TK) % 2 == 0
    assert S // PACK <= 256
    assert _C_QI == 0 and _C_KI % LANES == 0 and _C_WI % LANES == 0
    assert _C_QM % M_WIDTH == 0 and _C_GATE % D_MIX == 0
    topk = min(TOPK_MAX, S // 4)
    nkb = S // TK
    l = 0
    half = B_ROPE // 2

    wi = w_in[l]
    o = [0]
    for n in (A_WIDTH, A_WIDTH, A_WIDTH, IDX_HEADS * IDX_DIM, IDX_DIM, IDX_HEADS, A_WIDTH,
              Q_LORA, KV_LORA, B_ROPE, B_WIDTH, M_WIDTH, M_WIDTH):
        o.append(o[-1] + n)
    (c_qa, c_ka, c_va, c_qi, c_ki, c_wi, c_za, c_cq, c_ckv, c_kr, c_zb, c_qm, c_zm) = [
        wi[:, o[j]:o[j + 1]] for j in range(13)]
    zeros = lambda n: jnp.zeros((D, n), wi.dtype)
    w_pack = jnp.concatenate([
        c_qi,
        c_ki, c_ki,
        c_wi, zeros(LANES - IDX_HEADS),
        c_qm,
        c_za, c_zb, c_zm,
        c_qa, c_ka, c_va,
        c_cq, c_ckv,
        zeros(B_NOPE), c_kr, zeros(LANES - B_QK),
        zeros(B_NOPE), c_kr[:, half:], c_kr[:, :half], zeros(LANES - B_QK),
    ], axis=1).astype(BF16)
    assert w_pack.shape[1] == _C_END

    def rope_partner(v):
        z = jnp.zeros_like(v[..., :B_NOPE])
        return jnp.concatenate([z, v[..., B_NOPE + half:B_QK], v[..., B_NOPE:B_NOPE + half],
                                jnp.zeros_like(v[..., B_QK:])], axis=-1)

    wuq_h = jnp.pad(w_uq[l].reshape(Q_LORA, B_HEADS, B_QK), ((0, 0), (0, 0), (0, LANES - B_QK)))
    wuq_p = jnp.concatenate([wuq_h.reshape(Q_LORA, B_HEADS * LANES),
                             rope_partner(wuq_h).reshape(Q_LORA, B_HEADS * LANES)],
                            axis=1).astype(BF16)
    wukv = w_ukv[l].reshape(KV_LORA, B_HEADS, B_NOPE + B_V)
    wukv_k = jnp.pad(wukv[:, :, :B_NOPE], ((0, 0), (0, 0), (0, LANES - B_NOPE))
                     ).reshape(KV_LORA, B_HEADS * LANES).astype(BF16)
    wukv_v = wukv[:, :, B_NOPE:].reshape(KV_LORA, B_WIDTH).astype(BF16)

    idx = jnp.arange(A_WIDTH)
    gmat = jnp.where((idx[:, None] // HEAD_DIM) == (idx[None, :] // HEAD_DIM),
                     1.0 / HEAD_DIM, 0.0).astype(BF16)
    row2 = lambda v: v.reshape(1, -1).astype(F32)
    gqa_t = row2(jnp.tile(g_qa[l], A_HEADS) * (HEAD_DIM ** -0.5 * LOG2E))
    gka_t = row2(jnp.tile(g_ka[l], A_HEADS))
    gqm_t = row2(jnp.tile(g_qm[l], M_HEADS) * (HEAD_DIM ** -0.5 * LOG2E))
    gkm_t = row2(jnp.tile(g_km[l], M_HEADS))
    gqb_p = row2(jnp.pad(g_qb[l], (0, LANES - B_QK)) * (B_QK ** -0.5 * LOG2E))
    gkb_p = row2(jnp.pad(g_kb[l], (0, LANES - B_QK)))

    pos = jnp.arange(S, dtype=jnp.int32)
    inv = 1.0 / (ROPE_THETA ** (jnp.arange(half, dtype=F32) / half))
    ang = pos.astype(F32)[:, None] * inv[None, :]
    cos, sin = jnp.cos(ang), jnp.sin(ang)
    zs = lambda n: jnp.zeros((S, n), F32)
    rope_c = jnp.concatenate([jnp.ones((S, B_NOPE), F32), cos, cos, jnp.ones((S, LANES - B_QK), F32)], 1)
    rope_s = jnp.concatenate([zs(B_NOPE), -sin, sin, zs(LANES - B_QK)], 1)
    qa_tab, qb_tab = rope_c * gqb_p, rope_s * rope_partner(gqb_p)
    ka_tab, kb_tab = rope_c * gkb_p, rope_s * rope_partner(gkb_p)
    src = jnp.arange(A_WIDTH)
    place = (src[:, None] // HEAD_DIM * LANES + src[:, None] % HEAD_DIM
             == jnp.arange(A_HEADS * LANES)[None, :]).astype(BF16)

    slopes = 2.0 ** (-ALIBI_MAX * jnp.arange(1, A_HEADS + 1, dtype=F32) / A_HEADS) * LOG2E
    pieces = _bf16_pieces(slopes, N_ALIBI_PIECES)
    n_feat = 2 * N_ALIBI_PIECES
    qfeat = jnp.concatenate([jnp.zeros((A_HEADS, HEAD_DIM), F32), pieces * float(CHUNK), pieces,
                             jnp.zeros((A_HEADS, LANES - HEAD_DIM - n_feat), F32)], axis=1)
    pos_hi = jnp.repeat((pos // CHUNK).astype(F32)[:, None], N_ALIBI_PIECES, axis=1)
    pos_lo = jnp.repeat((pos % CHUNK).astype(F32)[:, None], N_ALIBI_PIECES, axis=1)
    kfeat = jnp.concatenate([zs(HEAD_DIM), pos_hi, pos_lo, zs(LANES - HEAD_DIM - n_feat)], axis=1)

    kk = jnp.arange(TK)
    ltri = (kk[None, :] <= kk[:, None]).astype(BF16)

    sds = jax.ShapeDtypeStruct
    tok = lambda w, j=0: pl.BlockSpec((None, TM, w), lambda b, t: (b, t, j))
    u = pl.pallas_call(
        _inproj_kernel,
        grid=(B, S // TM),
        in_specs=[tok(D), _full((1, D)), _full((D, _C_END))],
        out_specs=tok(_C_END),
        out_shape=sds((B, S, _C_END), BF16),
        compiler_params=_params("parallel", "parallel"),
        name="inproj",
    )(x, row2(g_in[l]), w_pack)

    grp = lambda n: pl.BlockSpec((None, n, TM, LANES), lambda b, t: (b, 0, t, 0))
    vt_out = lambda n: pl.BlockSpec((None, TM // TK, n * V_ROWS, TK), lambda b, t: (b, t, 0, 0))
    tab = pl.BlockSpec((TM, LANES), lambda b, t: (t, 0))
    qa, ka, vat, qb, kb, vbt = pl.pallas_call(
        _tokprep_kernel,
        grid=(B, S // TM),
        in_specs=[tok(_C_END), _full((A_WIDTH, A_WIDTH)), _full((A_WIDTH, A_HEADS * LANES)),
                  _full((1, A_WIDTH)), _full((1, A_WIDTH)), _full((A_HEADS, LANES)), tab,
                  _full((1, Q_LORA)), _full((Q_LORA, 2 * B_HEADS * LANES)), _full((1, KV_LORA)),
                  _full((KV_LORA, B_HEADS * LANES)), _full((KV_LORA, B_WIDTH)),
                  tab, tab, tab, tab],
        out_specs=[grp(A_HEADS), grp(A_HEADS), vt_out(A_HEADS),
                   grp(B_HEADS), grp(B_HEADS), vt_out(B_HEADS)],
        out_shape=[sds((B, A_HEADS, S, LANES), BF16), sds((B, A_HEADS, S, LANES), BF16),
                   sds((B, nkb, A_HEADS * V_ROWS, TK), BF16),
                   sds((B, B_HEADS, S, LANES), BF16), sds((B, B_HEADS, S, LANES), BF16),
                   sds((B, nkb, B_HEADS * V_ROWS, TK), BF16)],
        compiler_params=_params("parallel", "parallel"),
        name="tokprep",
    )(u, gmat, place, gqa_t, gka_t, qfeat, kfeat, row2(g_cq[l]), wuq_p,
      row2(g_ckv[l]), wukv_k, wukv_v, qa_tab, qb_tab, ka_tab, kb_tab)

    km, vm = pl.pallas_call(
        _memkv_kernel,
        grid=(B,),
        in_specs=[pl.BlockSpec((None, n_mem, D), lambda b: (b, 0, 0)), _full((1, D)),
                  _full((D, M_WIDTH)), _full((D, M_WIDTH)), _full((A_WIDTH, A_WIDTH)),
                  _full((1, M_WIDTH))],
        out_specs=[pl.BlockSpec((None, 2, n_mem, LANES), lambda b: (b, 0, 0, 0))] * 2,
        out_shape=[sds((B, 2, n_mem, LANES), BF16)] * 2,
        compiler_params=_params("parallel"),
        name="memkv",
    )(mem, row2(g_mem[l]), w_mk[l].astype(BF16), w_mv[l].astype(BF16), gmat, gkm_t)

    qtile = lambda w, j=0: pl.BlockSpec((None, TQ, w), lambda b, t: (b, t, j))
    seq = lambda n: pl.BlockSpec((None, n, S, LANES), lambda b, t: (b, 0, 0, 0))
    vt_in = lambda n: pl.BlockSpec((None, nkb, n * V_ROWS, TK), lambda b, t: (b, 0, 0, 0))
    heads = lambda n: pl.BlockSpec((None, n, TQ, LANES), lambda b, t: (b, 0, t, 0))
    flash_scratch = lambda n: [pltpu.VMEM((2, n, TK, TQ), F32), pltpu.VMEM((n, 1, TQ), F32),
                               pltpu.VMEM((n, V_ROWS, TQ), F32)]
    ya = pl.pallas_call(
        functools.partial(_dsa_kernel, topk=topk),
        grid=(B, S // TQ),
        in_specs=[pl.BlockSpec(memory_space=pltpu.SMEM),
                  qtile(IDX_HEADS * IDX_DIM, _C_QI // (IDX_HEADS * IDX_DIM)),
                  qtile(LANES, _C_WI // LANES),
                  pl.BlockSpec((None, S, LANES), lambda b, t: (b, 0, _C_KI // LANES)),
                  heads(A_HEADS), seq(A_HEADS), vt_in(A_HEADS), _full((TK, TK))],
        out_specs=qtile(A_WIDTH),
        out_shape=sds((B, S, A_WIDTH), BF16),
        scratch_shapes=[pltpu.VMEM((nkb, TK, TQ), F32), pltpu.VMEM((nkb, TK, TQ), BF16),
                        pltpu.VMEM((IDX_HEADS, TQ, LANES), BF16),
                        pltpu.VMEM((1, TQ), F32), pltpu.VMEM((SUBLANES, TQ), F32),
                        pltpu.VMEM((SUBLANES, TQ), F32)] + flash_scratch(A_HEADS),
        compiler_params=_params("parallel", "arbitrary"),
        name="dsa",
    )(slopes, u, u, u, qa, ka, vat, ltri)

    yb = pl.pallas_call(
        _mla_kernel,
        grid=(B, S // TQ),
        in_specs=[heads(B_HEADS), seq(B_HEADS), vt_in(B_HEADS)],
        out_specs=qtile(B_WIDTH),
        out_shape=sds((B, S, B_WIDTH), BF16),
        scratch_shapes=flash_scratch(B_HEADS),
        compiler_params=_params("parallel", "arbitrary"),
        name="mla",
    )(qb, kb, vbt)

    memkv = pl.BlockSpec((None, 2, n_mem, LANES), lambda b, t: (b, 0, 0, 0))
    return pl.pallas_call(
        _out_kernel,
        grid=(B, S // TM),
        in_specs=[tok(D), tok(A_WIDTH), tok(B_WIDTH), tok(D_MIX, _C_GATE // D_MIX),
                  tok(M_WIDTH, _C_QM // M_WIDTH), _full((A_WIDTH, A_WIDTH)), _full((1, M_WIDTH)),
                  memkv, memkv, _full((D, D))],
        out_specs=tok(D),
        out_shape=sds((B, S, D), x.dtype),
        compiler_params=_params("parallel", "parallel"),
        name="outproj",
    )(x, ya, yb, u, u, gmat, gqm_t, km, vm, w_out[l].astype(BF16))
```

```python
import functools
import math

import jax
import jax.numpy as jnp
from jax import lax
from jax.experimental import pallas as pl
from jax.experimental.pallas import tpu as pltpu

F32 = jnp.float32
BF16 = jnp.bfloat16

CHUNK = 64
EPS = 1e-6
HEAD_DIM = 64
A_HEADS = 6
IDX_HEADS = 8
IDX_DIM = 64
TOPK_MAX = 256
ALIBI_MAX = 8.0
B_HEADS = 6
B_NOPE = 64
B_ROPE = 32
B_V = 64
B_QK = B_NOPE + B_ROPE
Q_LORA = 256
KV_LORA = 128
ROPE_THETA = 10000.0
M_HEADS = 4
A_WIDTH = A_HEADS * HEAD_DIM
B_WIDTH = B_HEADS * B_V
M_WIDTH = M_HEADS * HEAD_DIM
D_MIX = A_WIDTH + B_WIDTH + M_WIDTH
LOG2E = math.log2(math.e)
N_ALIBI_PIECES = 3

LANES = 128
SUBLANES = 8
VMEM_LIMIT = 52 * 1024 * 1024

TM = 1024
TN = 512
TQ = 256
TK = 256
N_COARSE = 8
N_FINE = 12
PACK = 2 * SUBLANES
N_ACC = 4
TINY = 1e-37
V_ROWS = 80
NEG = -1e30

_C_QI = 0
_C_KI = _C_QI + IDX_HEADS * IDX_DIM
_C_WI = _C_KI + LANES
_C_QM = _C_WI + LANES
_C_GATE = _C_QM + M_WIDTH
_C_QA = _C_GATE + D_MIX
_C_KA = _C_QA + A_WIDTH
_C_VA = _C_KA + A_WIDTH
_C_CQ = _C_VA + A_WIDTH
_C_CKV = _C_CQ + Q_LORA
_C_KR = _C_CKV + KV_LORA
_C_KRP = _C_KR + LANES
_C_END = _C_KRP + LANES


def _nt_dot(a, b):
    return lax.dot_general(a, b, (((1,), (1,)), ((), ())), preferred_element_type=F32)


def _head64_norm(u, gmat_ref, g):
    n = u.shape[1]
    msq = jnp.dot((u * u).astype(BF16), gmat_ref[:n, :n], preferred_element_type=F32)
    return u * lax.rsqrt(msq + EPS) * g


def _inproj_kernel(x_ref, gin_ref, w_ref, u_ref):
    x = x_ref[...]
    ms = jnp.mean(x * x, axis=-1, keepdims=True)
    h = (x * lax.rsqrt(ms + EPS) * gin_ref[...]).astype(BF16)
    for a in range(0, _C_END, TN):
        b = min(a + TN, _C_END)
        u_ref[:, a:b] = jnp.dot(h, w_ref[:, a:b], preferred_element_type=F32).astype(BF16)


def _tokprep_kernel(u_ref, gmat_ref, place_ref, gqa_ref, gka_ref, qfeat_ref, kfeat_ref, gcq_ref,
                    wuq_ref, gckv_ref, wukvk_ref, wukvv_ref, qa_tab_ref, qb_tab_ref, ka_tab_ref,
                    kb_tab_ref, qa_ref, ka_ref, vat_ref, qb_ref, kb_ref, vbt_ref):
    def cols(a, b):
        return u_ref[:, a:b].astype(F32)

    ones_rows = (lax.broadcasted_iota(jnp.int32, (V_ROWS - HEAD_DIM, TK), 0) == 0).astype(BF16)

    def store_transposed(dst_ref, v):
        vt = v.T.astype(BF16)
        for j in range(TM // TK):
            for hd in range(v.shape[1] // HEAD_DIM):
                r = hd * V_ROWS
                dst_ref[j, r:r + HEAD_DIM, :] = vt[hd * HEAD_DIM:(hd + 1) * HEAD_DIM,
                                                   j * TK:(j + 1) * TK]
                dst_ref[j, r + HEAD_DIM:r + V_ROWS, :] = ones_rows

    def store_heads(dst_ref, v, feat_fn):
        placed = jnp.dot(v.astype(BF16), place_ref[...], preferred_element_type=F32)
        for hd in range(A_HEADS):
            dst_ref[hd] = (placed[:, hd * LANES:(hd + 1) * LANES] + feat_fn(hd)).astype(BF16)

    store_heads(qa_ref, _head64_norm(cols(_C_QA, _C_KA), gmat_ref, gqa_ref[...]),
                lambda hd: qfeat_ref[hd:hd + 1, :])
    kfeat = kfeat_ref[...]
    store_heads(ka_ref, _head64_norm(cols(_C_KA, _C_VA), gmat_ref, gka_ref[...]), lambda hd: kfeat)
    store_transposed(vat_ref, cols(_C_VA, _C_CQ))

    def head96(uh, partner, tab_a, tab_b):
        ss = jnp.sum(uh * uh, axis=-1, keepdims=True) * (1.0 / B_QK)
        return ((uh * tab_a + partner * tab_b) * lax.rsqrt(ss + EPS)).astype(BF16)

    cq = cols(_C_CQ, _C_CKV)
    cq = cq * lax.rsqrt(jnp.mean(cq * cq, axis=-1, keepdims=True) + EPS) * gcq_ref[...]
    q = jnp.dot(cq.astype(BF16), wuq_ref[...], preferred_element_type=F32)
    qa_tab = qa_tab_ref[...]
    qb_tab = qb_tab_ref[...]
    for hd in range(B_HEADS):
        qb_ref[hd] = head96(q[:, hd * LANES:(hd + 1) * LANES],
                            q[:, (B_HEADS + hd) * LANES:(B_HEADS + hd + 1) * LANES], qa_tab, qb_tab)

    ckv = cols(_C_CKV, _C_KR)
    ckv = ckv * lax.rsqrt(jnp.mean(ckv * ckv, axis=-1, keepdims=True) + EPS) * gckv_ref[...]
    ckv = ckv.astype(BF16)
    kk = jnp.dot(ckv, wukvk_ref[...], preferred_element_type=F32)
    store_transposed(vbt_ref, jnp.dot(ckv, wukvv_ref[...], preferred_element_type=F32))
    krope = cols(_C_KR, _C_KRP)
    krope_partner = cols(_C_KRP, _C_END)
    ka_tab = ka_tab_ref[...]
    kb_tab = kb_tab_ref[...]
    for hd in range(B_HEADS):
        kb_ref[hd] = head96(kk[:, hd * LANES:(hd + 1) * LANES] + krope, krope_partner, ka_tab, kb_tab)


def _memkv_kernel(mem_ref, gmem_ref, wmk_ref, wmv_ref, gmat_ref, gkm_ref, km_ref, vm_ref):
    x = mem_ref[...]
    ms = jnp.mean(x * x, axis=-1, keepdims=True)
    m = (x * lax.rsqrt(ms + EPS) * gmem_ref[...]).astype(BF16)
    k = _head64_norm(jnp.dot(m, wmk_ref[...], preferred_element_type=F32), gmat_ref,
                     gkm_ref[...]).astype(BF16)
    v = jnp.dot(m, wmv_ref[...], preferred_element_type=F32).astype(BF16)
    for g in range(M_WIDTH // LANES):
        km_ref[g] = k[:, g * LANES:(g + 1) * LANES]
        vm_ref[g] = v[:, g * LANES:(g + 1) * LANES]


def _flash_init(m_ref, acc_ref):
    m_ref[...] = jnp.full(m_ref.shape, NEG, F32)
    acc_ref[...] = jnp.zeros(acc_ref.shape, F32)


def _flash_softmax_pv(n_heads, s_ref, vt_fn, m_ref, acc_ref):
    for hd in range(n_heads):
        s = s_ref[hd]
        m_prev = m_ref[hd]
        m_new = jnp.maximum(m_prev, jnp.max(s, axis=0, keepdims=True))
        p = jnp.exp2(s - m_new).astype(BF16)
        alpha = jnp.exp2(m_prev - m_new)
        m_ref[hd] = m_new
        acc_ref[hd] = alpha * acc_ref[hd] + jnp.dot(vt_fn(hd), p, preferred_element_type=F32)


def _flash_loop(i, diag_logits_fn, logits_fn, update_fn):
    def block(t):
        return jnp.minimum(t, i) - 1

    def consumed(t):
        return jnp.where(t == 0, i, t - 1)

    diag_logits_fn(0)
    n = i + 1
    n_quad = n // 4

    def quad(j, carry):
        t = 4 * j
        logits_fn(block(t + 1), 1)
        update_fn(consumed(t), 0)
        logits_fn(block(t + 2), 0)
        update_fn(t, 1)
        logits_fn(block(t + 3), 1)
        update_fn(t + 1, 0)
        logits_fn(block(t + 4), 0)
        update_fn(t + 2, 1)
        return carry

    lax.fori_loop(0, n_quad, quad, 0)
    t0 = 4 * n_quad
    rem = n - t0

    @pl.when(rem >= 2)
    def _():
        logits_fn(block(t0 + 1), 1)
        update_fn(consumed(t0), 0)
        logits_fn(block(t0 + 2), 0)
        update_fn(t0, 1)

    @pl.when(rem % 2 == 1)
    def _():
        update_fn(consumed(n - 1), 0)


def _flash_store(out_ref, acc_ref, n_heads):
    def head(hd):
        return acc_ref[hd, :HEAD_DIM, :] / acc_ref[hd, HEAD_DIM:HEAD_DIM + 1, :]

    for g in range(n_heads // 2):
        o = jnp.concatenate([head(2 * g), head(2 * g + 1)], axis=0)
        out_ref[:, g * LANES:(g + 1) * LANES] = o.T.astype(out_ref.dtype)


def _split_head_pairs(src_ref, dst_ref, n_pairs):
    low = lax.broadcasted_iota(jnp.int32, (src_ref.shape[0], LANES), 1) < HEAD_DIM
    for g in range(n_pairs):
        pair = src_ref[:, g * LANES:(g + 1) * LANES].astype(F32)
        dst_ref[2 * g] = jnp.where(low, pair, 0.0).astype(dst_ref.dtype)
        dst_ref[2 * g + 1] = jnp.where(low, 0.0, pair).astype(dst_ref.dtype)


def _fold8(x):
    return x.reshape(x.shape[0] // SUBLANES, SUBLANES, x.shape[1])


def _dsa_kernel(slopes_ref, qi_ref, wi_ref, ki_ref, qa_ref, ka_ref, vat_ref, ltri_ref,
                out_ref,
                score_ref, sb_ref, qim_ref, lo_ref, mn_ref, mx_ref, s_ref, m_ref, acc_ref,
                *, topk):
    i = pl.program_id(1)
    nkb = i + 1
    npair = (nkb + 1) // 2
    q0 = i * TQ

    _split_head_pairs(qi_ref, qim_ref, IDX_HEADS // 2)

    kidx = lax.broadcasted_iota(jnp.int32, (TK, TQ), 0)
    qidx = lax.broadcasted_iota(jnp.int32, (TK, TQ), 1)

    wt = wi_ref[...].astype(F32).T

    def score_block(kb, diagonal):
        k0 = pl.multiple_of(kb * TK, TK)
        kblk = ki_ref[pl.ds(k0, TK), :]
        acc = jnp.zeros((TK, TQ), F32)
        for hd in range(IDX_HEADS):
            acc = acc + jnp.maximum(_nt_dot(kblk, qim_ref[hd]), 0.0) * wt[hd:hd + 1, :]
        hi_part = lo_part = _fold8(acc)
        if diagonal:
            adm = (kidx >> 6) <= (qidx >> 6)
            acc = jnp.where(adm, acc, -jnp.inf)
            hi_part = _fold8(acc)
            lo_part = _fold8(jnp.where(adm, acc, jnp.inf))
        score_ref[kb] = acc
        sb_ref[kb] = acc.astype(BF16)
        mn_ref[...] = jnp.minimum(mn_ref[...], jnp.min(lo_part, axis=0))
        mx_ref[...] = jnp.maximum(mx_ref[...], jnp.max(hi_part, axis=0))

    mn_ref[...] = jnp.full((SUBLANES, TQ), jnp.inf, F32)
    mx_ref[...] = jnp.full((SUBLANES, TQ), -jnp.inf, F32)

    def score_pair(j, carry):
        score_block(2 * j, False)
        score_block(2 * j + 1, False)
        return carry

    lax.fori_loop(0, i // 2, score_pair, 0)

    @pl.when(i % 2 == 1)
    def _():
        score_block(i - 1, False)

    score_block(i, True)

    @pl.when(nkb % 2 == 1)
    def _():
        score_ref[nkb] = jnp.full((TK, TQ), -jnp.inf, F32)
        sb_ref[nkb] = jnp.full((TK, TQ), -jnp.inf, BF16)

    qrow = q0 + lax.broadcasted_iota(jnp.int32, (1, TQ), 1)
    n_adm = ((qrow >> 6) + 1) << 6
    k_eff = jnp.minimum(n_adm, topk).astype(F32)

    lo = jnp.min(mn_ref[...], axis=0, keepdims=True)
    hi = jnp.max(mx_ref[...], axis=0, keepdims=True)

    def count_ge_coarse(t):
        tb = jnp.broadcast_to(t.astype(BF16), (PACK, TQ))
        one = jnp.ones((PACK, TQ), BF16)
        zero = jnp.zeros((PACK, TQ), BF16)

        def body(j, accs):
            accs = list(accs)
            for kb in (2 * j, 2 * j + 1):
                sb = sb_ref[kb]
                for r in range(TK // PACK):
                    hit = jnp.where(sb[r * PACK:(r + 1) * PACK, :] >= tb, one, zero)
                    accs[r % N_ACC] = accs[r % N_ACC] + hit
            return tuple(accs)

        accs = lax.fori_loop(0, npair, body, (zero,) * N_ACC)
        return jnp.sum(sum(a.astype(F32) for a in accs), axis=0, keepdims=True)

    def coarse(_, c):
        lo, hi = c
        mid = (0.5 * (lo + hi)).astype(BF16).astype(F32)
        ge = count_ge_coarse(mid) >= k_eff
        return jnp.where(ge, mid, lo), jnp.where(ge, hi, mid)

    lo, hi = lax.fori_loop(0, N_COARSE, coarse,
                           (lo.astype(BF16).astype(F32), hi.astype(BF16).astype(F32)))
    lo = lo - jnp.abs(lo) * 2.0 ** -8 - TINY
    hi = hi + jnp.abs(hi) * 2.0 ** -8 + TINY

    def count_ge(t):
        tb = jnp.broadcast_to(t, (SUBLANES, TQ))

        def body(j, accs):
            accs = list(accs)
            for kb in (2 * j, 2 * j + 1):
                s = score_ref[kb]
                for r in range(TK // SUBLANES):
                    hit = jnp.where(s[r * SUBLANES:(r + 1) * SUBLANES, :] >= tb, 1.0, 0.0)
                    accs[r % N_ACC] = accs[r % N_ACC] + hit
            return tuple(accs)

        z = jnp.zeros((SUBLANES, TQ), F32)
        accs = lax.fori_loop(0, npair, body, (z,) * N_ACC)
        return jnp.sum(sum(accs), axis=0, keepdims=True)

    def bisect(_, c):
        lo, hi, c_lo = c
        mid = 0.5 * (lo + hi)
        cnt = count_ge(mid)
        ge = cnt >= k_eff
        return jnp.where(ge, mid, lo), jnp.where(ge, hi, mid), jnp.where(ge, cnt, c_lo)

    c_lo = jnp.where(n_adm <= topk, k_eff, k_eff + 1.0)
    lo, hi, c_lo = lax.fori_loop(0, N_FINE, bisect, (lo, hi, c_lo))
    lo_ref[...] = lo

    for c in range(TQ // LANES):
        cs = slice(c * LANES, (c + 1) * LANES)

        @pl.when(jnp.max(c_lo[:, cs] - k_eff[:, cs]) > 0.0)
        def _():
            k_c = k_eff[:, cs]

            def counts(t):
                def body(j, c2):
                    ge, gt = c2
                    for kb in (2 * j, 2 * j + 1):
                        s = _fold8(score_ref[kb, :, cs])
                        ge = ge + jnp.sum(jnp.where(s >= t, 1.0, 0.0), axis=0)
                        gt = gt + jnp.sum(jnp.where(s > t, 1.0, 0.0), axis=0)
                    return ge, gt

                z = jnp.zeros((SUBLANES, LANES), F32)
                ge, gt = lax.fori_loop(0, npair, body, (z, z))
                return jnp.sum(ge, axis=0, keepdims=True), jnp.sum(gt, axis=0, keepdims=True)

            def largest(keep):
                def body(j, acc):
                    for kb in (2 * j, 2 * j + 1):
                        s = _fold8(score_ref[kb, :, cs])
                        acc = jnp.maximum(acc, jnp.max(jnp.where(keep(s), s, -jnp.inf), axis=0))
                    return acc

                acc = lax.fori_loop(0, npair, body, jnp.full((SUBLANES, LANES), -jnp.inf, F32))
                return jnp.max(acc, axis=0, keepdims=True)

            def pending(ge):
                return jnp.sum(jnp.where(ge >= k_c, 0.0, 1.0))

            def walk(c3):
                t, ge, _, _ = c3
                t = jnp.where(ge >= k_c, t, largest(lambda s: s < t))
                ge, gt = counts(t)
                return t, ge, gt, pending(ge)

            hi_c = hi[:, cs]
            t0 = largest(lambda s: s <= hi_c)
            ge0, gt0 = counts(t0)
            thr, _, c_gt, _ = lax.while_loop(lambda c3: c3[3] > 0.0, walk,
                                             (t0, ge0, gt0, pending(ge0)))
            need = k_c - c_gt

            def mark(j, seen):
                for kb in (2 * j, 2 * j + 1):
                    s = score_ref[kb, :, cs]
                    eqf = jnp.where(s == thr, 1.0, 0.0)
                    rank = seen + jnp.dot(ltri_ref[...], eqf.astype(BF16),
                                          preferred_element_type=F32)
                    tie = jnp.where(rank <= need, eqf, 0.0)
                    score_ref[kb, :, cs] = jnp.where(s > thr, 1.0, tie) * 2.0 - 1.0
                    seen = seen + jnp.sum(eqf, axis=0, keepdims=True)
                return seen

            lax.fori_loop(0, npair, mark, jnp.zeros((1, LANES), F32))
            lo_ref[:, cs] = jnp.zeros((1, LANES), F32)

    _flash_init(m_ref, acc_ref)
    lo_row = lo_ref[...]

    def logits(kb, buf, future=None):
        k0 = pl.multiple_of(kb * TK, TK)
        bias = jnp.where(score_ref[kb] >= lo_row, 0.0, NEG)
        for hd in range(A_HEADS):
            s = _nt_dot(ka_ref[hd, pl.ds(k0, TK), :], qa_ref[hd])
            s_ref[buf, hd] = s + (bias if future is None else bias - slopes_ref[hd] * future)

    def update(kb, buf):
        _flash_softmax_pv(A_HEADS, s_ref.at[buf],
                          lambda hd: vat_ref[kb, hd * V_ROWS:(hd + 1) * V_ROWS, :],
                          m_ref, acc_ref)

    _flash_loop(i, lambda buf: logits(i, buf, 2.0 * jnp.maximum(kidx - qidx, 0).astype(F32)),
                logits, update)
    _flash_store(out_ref, acc_ref, A_HEADS)


def _mla_kernel(qb_ref, kb_ref, vbt_ref, out_ref, s_ref, m_ref, acc_ref):
    i = pl.program_id(1)
    _flash_init(m_ref, acc_ref)

    def logits(kb, buf, bias=None):
        k0 = pl.multiple_of(kb * TK, TK)
        for hd in range(B_HEADS):
            s = _nt_dot(kb_ref[hd, pl.ds(k0, TK), :], qb_ref[hd])
            s_ref[buf, hd] = s if bias is None else s + bias

    def update(kb, buf):
        _flash_softmax_pv(B_HEADS, s_ref.at[buf],
                          lambda hd: vbt_ref[kb, hd * V_ROWS:(hd + 1) * V_ROWS, :],
                          m_ref, acc_ref)

    kidx = lax.broadcasted_iota(jnp.int32, (TK, TQ), 0)
    qidx = lax.broadcasted_iota(jnp.int32, (TK, TQ), 1)
    _flash_loop(i, lambda buf: logits(i, buf, jnp.where((kidx >> 6) <= (qidx >> 6), 0.0, NEG)),
                logits, update)
    _flash_store(out_ref, acc_ref, B_HEADS)


def _out_kernel(x_ref, ya_ref, yb_ref, z_ref, qm_ref, gmat_ref, gqm_ref, km_ref, vm_ref, wout_ref,
                out_ref):
    tm = x_ref.shape[0]
    low = lax.broadcasted_iota(jnp.int32, (tm, LANES), 1) < HEAD_DIM
    qm = _head64_norm(qm_ref[...].astype(F32), gmat_ref, gqm_ref[...])
    ym = []
    for g in range(M_HEADS // 2):
        pair = qm[:, g * LANES:(g + 1) * LANES]
        outs = []
        for q in (jnp.where(low, pair, 0.0), jnp.where(low, 0.0, pair)):
            s = _nt_dot(q.astype(BF16), km_ref[g])
            p = jnp.exp2(s - jnp.max(s, axis=1, keepdims=True))
            o = jnp.dot(p.astype(BF16), vm_ref[g], preferred_element_type=F32)
            outs.append(o / jnp.sum(p, axis=1, keepdims=True))
        ym.append(jnp.where(low, outs[0], outs[1]))
    y = jnp.concatenate([ya_ref[...].astype(F32), yb_ref[...].astype(F32)] + ym, axis=1)
    z = z_ref[...].astype(F32)
    y = (y * (z / (1.0 + jnp.exp(-z)))).astype(BF16)
    out_ref[...] = x_ref[...] + jnp.dot(y, wout_ref[...], preferred_element_type=F32)


def _full(shape):
    n = len(shape)
    return pl.BlockSpec(shape, lambda *_: (0,) * n)


def _params(*sem):
    return pltpu.CompilerParams(dimension_semantics=sem, vmem_limit_bytes=VMEM_LIMIT)


def _bf16_pieces(v, n):
    out, rest = [], v
    for _ in range(n):
        piece = rest.astype(BF16).astype(F32)
        out.append(piece)
        rest = rest - piece
    return jnp.stack(out, axis=-1)


def kernel(x, mem, g_in, w_in, w_uq, g_cq, w_ukv, g_ckv, g_qa, g_ka, g_qb, g_kb,
           g_mem, w_mk, w_mv, g_qm, g_km, w_out):
    B, S, D = x.shape
    n_mem = mem.shape[1]
    assert S % TM == 0 and TM % TK == 0 and TQ == TK and D == D_MIX and (S // TK) % 2 == 0
    assert S // PACK <= 256
    assert _C_QI == 0 and _C_KI % LANES == 0 and _C_WI % LANES == 0
    assert _C_QM % M_WIDTH == 0 and _C_GATE % D_MIX == 0
    topk = min(TOPK_MAX, S // 4)
    nkb = S // TK
    l = 0
    half = B_ROPE // 2

    wi = w_in[l]
    o = [0]
    for n in (A_WIDTH, A_WIDTH, A_WIDTH, IDX_HEADS * IDX_DIM, IDX_DIM, IDX_HEADS, A_WIDTH,
              Q_LORA, KV_LORA, B_ROPE, B_WIDTH, M_WIDTH, M_WIDTH):
        o.append(o[-1] + n)
    (c_qa, c_ka, c_va, c_qi, c_ki, c_wi, c_za, c_cq, c_ckv, c_kr, c_zb, c_qm, c_zm) = [
        wi[:, o[j]:o[j + 1]] for j in range(13)]
    zeros = lambda n: jnp.zeros((D, n), wi.dtype)
    w_pack = jnp.concatenate([
        c_qi,
        c_ki, c_ki,
        c_wi, zeros(LANES - IDX_HEADS),
        c_qm,
        c_za, c_zb, c_zm,
        c_qa, c_ka, c_va,
        c_cq, c_ckv,
        zeros(B_NOPE), c_kr, zeros(LANES - B_QK),
        zeros(B_NOPE), c_kr[:, half:], c_kr[:, :half], zeros(LANES - B_QK),
    ], axis=1).astype(BF16)
    assert w_pack.shape[1] == _C_END

    def rope_partner(v):
        z = jnp.zeros_like(v[..., :B_NOPE])
        return jnp.concatenate([z, v[..., B_NOPE + half:B_QK], v[..., B_NOPE:B_NOPE + half],
                                jnp.zeros_like(v[..., B_QK:])], axis=-1)

    wuq_h = jnp.pad(w_uq[l].reshape(Q_LORA, B_HEADS, B_QK), ((0, 0), (0, 0), (0, LANES - B_QK)))
    wuq_p = jnp.concatenate([wuq_h.reshape(Q_LORA, B_HEADS * LANES),
                             rope_partner(wuq_h).reshape(Q_LORA, B_HEADS * LANES)],
                            axis=1).astype(BF16)
    wukv = w_ukv[l].reshape(KV_LORA, B_HEADS, B_NOPE + B_V)
    wukv_k = jnp.pad(wukv[:, :, :B_NOPE], ((0, 0), (0, 0), (0, LANES - B_NOPE))
                     ).reshape(KV_LORA, B_HEADS * LANES).astype(BF16)
    wukv_v = wukv[:, :, B_NOPE:].reshape(KV_LORA, B_WIDTH).astype(BF16)

    idx = jnp.arange(A_WIDTH)
    gmat = jnp.where((idx[:, None] // HEAD_DIM) == (idx[None, :] // HEAD_DIM),
                     1.0 / HEAD_DIM, 0.0).astype(BF16)
    row2 = lambda v: v.reshape(1, -1).astype(F32)
    gqa_t = row2(jnp.tile(g_qa[l], A_HEADS) * (HEAD_DIM ** -0.5 * LOG2E))
    gka_t = row2(jnp.tile(g_ka[l], A_HEADS))
    gqm_t = row2(jnp.tile(g_qm[l], M_HEADS) * (HEAD_DIM ** -0.5 * LOG2E))
    gkm_t = row2(jnp.tile(g_km[l], M_HEADS))
    gqb_p = row2(jnp.pad(g_qb[l], (0, LANES - B_QK)) * (B_QK ** -0.5 * LOG2E))
    gkb_p = row2(jnp.pad(g_kb[l], (0, LANES - B_QK)))

    pos = jnp.arange(S, dtype=jnp.int32)
    inv = 1.0 / (ROPE_THETA ** (jnp.arange(half, dtype=F32) / half))
    ang = pos.astype(F32)[:, None] * inv[None, :]
    cos, sin = jnp.cos(ang), jnp.sin(ang)
    zs = lambda n: jnp.zeros((S, n), F32)
    rope_c = jnp.concatenate([jnp.ones((S, B_NOPE), F32), cos, cos, jnp.ones((S, LANES - B_QK), F32)], 1)
    rope_s = jnp.concatenate([zs(B_NOPE), -sin, sin, zs(LANES - B_QK)], 1)
    qa_tab, qb_tab = rope_c * gqb_p, rope_s * rope_partner(gqb_p)
    ka_tab, kb_tab = rope_c * gkb_p, rope_s * rope_partner(gkb_p)
    src = jnp.arange(A_WIDTH)
    place = (src[:, None] // HEAD_DIM * LANES + src[:, None] % HEAD_DIM
             == jnp.arange(A_HEADS * LANES)[None, :]).astype(BF16)

    slopes = 2.0 ** (-ALIBI_MAX * jnp.arange(1, A_HEADS + 1, dtype=F32) / A_HEADS) * LOG2E
    pieces = _bf16_pieces(slopes, N_ALIBI_PIECES)
    n_feat = 2 * N_ALIBI_PIECES
    qfeat = jnp.concatenate([jnp.zeros((A_HEADS, HEAD_DIM), F32), pieces * float(CHUNK), pieces,
                             jnp.zeros((A_HEADS, LANES - HEAD_DIM - n_feat), F32)], axis=1)
    pos_hi = jnp.repeat((pos // CHUNK).astype(F32)[:, None], N_ALIBI_PIECES, axis=1)
    pos_lo = jnp.repeat((pos % CHUNK).astype(F32)[:, None], N_ALIBI_PIECES, axis=1)
    kfeat = jnp.concatenate([zs(HEAD_DIM), pos_hi, pos_lo, zs(LANES - HEAD_DIM - n_feat)], axis=1)

    kk = jnp.arange(TK)
    ltri = (kk[None, :] <= kk[:, None]).astype(BF16)

    sds = jax.ShapeDtypeStruct
    tok = lambda w, j=0: pl.BlockSpec((None, TM, w), lambda b, t: (b, t, j))
    u = pl.pallas_call(
        _inproj_kernel,
        grid=(B, S // TM),
        in_specs=[tok(D), _full((1, D)), _full((D, _C_END))],
        out_specs=tok(_C_END),
        out_shape=sds((B, S, _C_END), BF16),
        compiler_params=_params("parallel", "parallel"),
        name="inproj",
    )(x, row2(g_in[l]), w_pack)

    grp = lambda n: pl.BlockSpec((None, n, TM, LANES), lambda b, t: (b, 0, t, 0))
    vt_out = lambda n: pl.BlockSpec((None, TM // TK, n * V_ROWS, TK), lambda b, t: (b, t, 0, 0))
    tab = pl.BlockSpec((TM, LANES), lambda b, t: (t, 0))
    qa, ka, vat, qb, kb, vbt = pl.pallas_call(
        _tokprep_kernel,
        grid=(B, S // TM),
        in_specs=[tok(_C_END), _full((A_WIDTH, A_WIDTH)), _full((A_WIDTH, A_HEADS * LANES)),
                  _full((1, A_WIDTH)), _full((1, A_WIDTH)), _full((A_HEADS, LANES)), tab,
                  _full((1, Q_LORA)), _full((Q_LORA, 2 * B_HEADS * LANES)), _full((1, KV_LORA)),
                  _full((KV_LORA, B_HEADS * LANES)), _full((KV_LORA, B_WIDTH)),
                  tab, tab, tab, tab],
        out_specs=[grp(A_HEADS), grp(A_HEADS), vt_out(A_HEADS),
                   grp(B_HEADS), grp(B_HEADS), vt_out(B_HEADS)],
        out_shape=[sds((B, A_HEADS, S, LANES), BF16), sds((B, A_HEADS, S, LANES), BF16),
                   sds((B, nkb, A_HEADS * V_ROWS, TK), BF16),
                   sds((B, B_HEADS, S, LANES), BF16), sds((B, B_HEADS, S, LANES), BF16),
                   sds((B, nkb, B_HEADS * V_ROWS, TK), BF16)],
        compiler_params=_params("parallel", "parallel"),
        name="tokprep",
    )(u, gmat, place, gqa_t, gka_t, qfeat, kfeat, row2(g_cq[l]), wuq_p,
      row2(g_ckv[l]), wukv_k, wukv_v, qa_tab, qb_tab, ka_tab, kb_tab)

    km, vm = pl.pallas_call(
        _memkv_kernel,
        grid=(B,),
        in_specs=[pl.BlockSpec((None, n_mem, D), lambda b: (b, 0, 0)), _full((1, D)),
                  _full((D, M_WIDTH)), _full((D, M_WIDTH)), _full((A_WIDTH, A_WIDTH)),
                  _full((1, M_WIDTH))],
        out_specs=[pl.BlockSpec((None, 2, n_mem, LANES), lambda b: (b, 0, 0, 0))] * 2,
        out_shape=[sds((B, 2, n_mem, LANES), BF16)] * 2,
        compiler_params=_params("parallel"),
        name="memkv",
    )(mem, row2(g_mem[l]), w_mk[l].astype(BF16), w_mv[l].astype(BF16), gmat, gkm_t)

    qtile = lambda w, j=0: pl.BlockSpec((None, TQ, w), lambda b, t: (b, t, j))
    seq = lambda n: pl.BlockSpec((None, n, S, LANES), lambda b, t: (b, 0, 0, 0))
    vt_in = lambda n: pl.BlockSpec((None, nkb, n * V_ROWS, TK), lambda b, t: (b, 0, 0, 0))
    heads = lambda n: pl.BlockSpec((None, n, TQ, LANES), lambda b, t: (b, 0, t, 0))
    flash_scratch = lambda n: [pltpu.VMEM((2, n, TK, TQ), F32), pltpu.VMEM((n, 1, TQ), F32),
                               pltpu.VMEM((n, V_ROWS, TQ), F32)]
    ya = pl.pallas_call(
        functools.partial(_dsa_kernel, topk=topk),
        grid=(B, S // TQ),
        in_specs=[pl.BlockSpec(memory_space=pltpu.SMEM),
                  qtile(IDX_HEADS * IDX_DIM, _C_QI // (IDX_HEADS * IDX_DIM)),
                  qtile(LANES, _C_WI // LANES),
                  pl.BlockSpec((None, S, LANES), lambda b, t: (b, 0, _C_KI // LANES)),
                  heads(A_HEADS), seq(A_HEADS), vt_in(A_HEADS), _full((TK, TK))],
        out_specs=qtile(A_WIDTH),
        out_shape=sds((B, S, A_WIDTH), BF16),
        scratch_shapes=[pltpu.VMEM((nkb, TK, TQ), F32), pltpu.VMEM((nkb, TK, TQ), BF16),
                        pltpu.VMEM((IDX_HEADS, TQ, LANES), BF16),
                        pltpu.VMEM((1, TQ), F32), pltpu.VMEM((SUBLANES, TQ), F32),
                        pltpu.VMEM((SUBLANES, TQ), F32)] + flash_scratch(A_HEADS),
        compiler_params=_params("parallel", "arbitrary"),
        name="dsa",
    )(slopes, u, u, u, qa, ka, vat, ltri)

    yb = pl.pallas_call(
        _mla_kernel,
        grid=(B, S // TQ),
        in_specs=[heads(B_HEADS), seq(B_HEADS), vt_in(B_HEADS)],
        out_specs=qtile(B_WIDTH),
        out_shape=sds((B, S, B_WIDTH), BF16),
        scratch_shapes=flash_scratch(B_HEADS),
        compiler_params=_params("parallel", "arbitrary"),
        name="mla",
    )(qb, kb, vbt)

    memkv = pl.BlockSpec((None, 2, n_mem, LANES), lambda b, t: (b, 0, 0, 0))
    return pl.pallas_call(
        _out_kernel,
        grid=(B, S // TM),
        in_specs=[tok(D), tok(A_WIDTH), tok(B_WIDTH), tok(D_MIX, _C_GATE // D_MIX),
                  tok(M_WIDTH, _C_QM // M_WIDTH), _full((A_WIDTH, A_WIDTH)), _full((1, M_WIDTH)),
                  memkv, memkv, _full((D, D))],
        out_specs=tok(D),
        out_shape=sds((B, S, D), x.dtype),
        compiler_params=_params("parallel", "parallel"),
        name="outproj",
    )(x, ya, yb, u, u, gmat, gqm_t, km, vm, w_out[l].astype(BF16))
```

```python
import functools
import math

import jax
import jax.numpy as jnp
from jax import lax
from jax.experimental import pallas as pl
from jax.experimental.pallas import tpu as pltpu

F32 = jnp.float32
BF16 = jnp.bfloat16

CHUNK = 64
EPS = 1e-6
HEAD_DIM = 64
A_HEADS = 6
IDX_HEADS = 8
IDX_DIM = 64
TOPK_MAX = 256
ALIBI_MAX = 8.0
B_HEADS = 6
B_NOPE = 64
B_ROPE = 32
B_V = 64
B_QK = B_NOPE + B_ROPE
Q_LORA = 256
KV_LORA = 128
ROPE_THETA = 10000.0
M_HEADS = 4
A_WIDTH = A_HEADS * HEAD_DIM
B_WIDTH = B_HEADS * B_V
M_WIDTH = M_HEADS * HEAD_DIM
D_MIX = A_WIDTH + B_WIDTH + M_WIDTH
LOG2E = math.log2(math.e)
N_ALIBI_PIECES = 3

LANES = 128
SUBLANES = 8
VMEM_LIMIT = 52 * 1024 * 1024

TM = 1024
TN = 512
TQ = 256
TK = 256
N_COARSE = 8
N_FINE = 12
PACK = 2 * SUBLANES
N_ACC = 4
TINY = 1e-37
V_ROWS = 80
NEG = -1e30

_C_QI = 0
_C_KI = _C_QI + IDX_HEADS * IDX_DIM
_C_WI = _C_KI + LANES
_C_QM = _C_WI + LANES
_C_GATE = _C_QM + M_WIDTH
_C_QA = _C_GATE + D_MIX
_C_KA = _C_QA + A_WIDTH
_C_VA = _C_KA + A_WIDTH
_C_CQ = _C_VA + A_WIDTH
_C_CKV = _C_CQ + Q_LORA
_C_KR = _C_CKV + KV_LORA
_C_KRP = _C_KR + LANES
_C_END = _C_KRP + LANES


def _nt_dot(a, b):
    return lax.dot_general(a, b, (((1,), (1,)), ((), ())), preferred_element_type=F32)


def _head64_norm(u, gmat_ref, g):
    n = u.shape[1]
    msq = jnp.dot((u * u).astype(BF16), gmat_ref[:n, :n], preferred_element_type=F32)
    return u * lax.rsqrt(msq + EPS) * g


def _inproj_kernel(x_ref, gin_ref, w_ref, u_ref):
    x = x_ref[...]
    ms = jnp.mean(x * x, axis=-1, keepdims=True)
    h = (x * lax.rsqrt(ms + EPS) * gin_ref[...]).astype(BF16)
    for a in range(0, _C_END, TN):
        b = min(a + TN, _C_END)
        u_ref[:, a:b] = jnp.dot(h, w_ref[:, a:b], preferred_element_type=F32).astype(BF16)


def _tokprep_kernel(u_ref, gmat_ref, place_ref, gqa_ref, gka_ref, qfeat_ref, kfeat_ref, gcq_ref,
                    wuq_ref, gckv_ref, wukvk_ref, wukvv_ref, qa_tab_ref, qb_tab_ref, ka_tab_ref,
                    kb_tab_ref, qa_ref, ka_ref, vat_ref, qb_ref, kb_ref, vbt_ref):
    def cols(a, b):
        return u_ref[:, a:b].astype(F32)

    ones_rows = (lax.broadcasted_iota(jnp.int32, (V_ROWS - HEAD_DIM, TK), 0) == 0).astype(BF16)

    def store_transposed(dst_ref, v):
        vt = v.T.astype(BF16)
        for j in range(TM // TK):
            for hd in range(v.shape[1] // HEAD_DIM):
                r = hd * V_ROWS
                dst_ref[j, r:r + HEAD_DIM, :] = vt[hd * HEAD_DIM:(hd + 1) * HEAD_DIM,
                                                   j * TK:(j + 1) * TK]
                dst_ref[j, r + HEAD_DIM:r + V_ROWS, :] = ones_rows

    def store_heads(dst_ref, v, feat_fn):
        vb = v.astype(BF16)
        for g in range(A_HEADS // 2):
            placed = jnp.dot(vb[:, g * LANES:(g + 1) * LANES], place_ref[...],
                             preferred_element_type=F32)
            for hd in (2 * g, 2 * g + 1):
                half = placed[:, (hd % 2) * LANES:(hd % 2 + 1) * LANES]
                dst_ref[hd] = (half + feat_fn(hd)).astype(BF16)

    store_heads(qa_ref, _head64_norm(cols(_C_QA, _C_KA), gmat_ref, gqa_ref[...]),
                lambda hd: qfeat_ref[hd:hd + 1, :])
    kfeat = kfeat_ref[...]
    store_heads(ka_ref, _head64_norm(cols(_C_KA, _C_VA), gmat_ref, gka_ref[...]), lambda hd: kfeat)
    store_transposed(vat_ref, cols(_C_VA, _C_CQ))

    def head96(uh, partner, tab_a, tab_b):
        ss = jnp.sum(uh * uh, axis=-1, keepdims=True) * (1.0 / B_QK)
        return ((uh * tab_a + partner * tab_b) * lax.rsqrt(ss + EPS)).astype(BF16)

    cq = cols(_C_CQ, _C_CKV)
    cq = cq * lax.rsqrt(jnp.mean(cq * cq, axis=-1, keepdims=True) + EPS) * gcq_ref[...]
    q = jnp.dot(cq.astype(BF16), wuq_ref[...], preferred_element_type=F32)
    qa_tab = qa_tab_ref[...]
    qb_tab = qb_tab_ref[...]
    for hd in range(B_HEADS):
        qb_ref[hd] = head96(q[:, hd * LANES:(hd + 1) * LANES],
                            q[:, (B_HEADS + hd) * LANES:(B_HEADS + hd + 1) * LANES], qa_tab, qb_tab)

    ckv = cols(_C_CKV, _C_KR)
    ckv = ckv * lax.rsqrt(jnp.mean(ckv * ckv, axis=-1, keepdims=True) + EPS) * gckv_ref[...]
    ckv = ckv.astype(BF16)
    kk = jnp.dot(ckv, wukvk_ref[...], preferred_element_type=F32)
    store_transposed(vbt_ref, jnp.dot(ckv, wukvv_ref[...], preferred_element_type=F32))
    krope = cols(_C_KR, _C_KRP)
    krope_partner = cols(_C_KRP, _C_END)
    ka_tab = ka_tab_ref[...]
    kb_tab = kb_tab_ref[...]
    for hd in range(B_HEADS):
        kb_ref[hd] = head96(kk[:, hd * LANES:(hd + 1) * LANES] + krope, krope_partner, ka_tab, kb_tab)


def _memkv_kernel(mem_ref, gmem_ref, wmk_ref, wmv_ref, gmat_ref, gkm_ref, km_ref, vm_ref):
    x = mem_ref[...]
    ms = jnp.mean(x * x, axis=-1, keepdims=True)
    m = (x * lax.rsqrt(ms + EPS) * gmem_ref[...]).astype(BF16)
    k = _head64_norm(jnp.dot(m, wmk_ref[...], preferred_element_type=F32), gmat_ref,
                     gkm_ref[...]).astype(BF16)
    v = jnp.dot(m, wmv_ref[...], preferred_element_type=F32).astype(BF16)
    for g in range(M_WIDTH // LANES):
        km_ref[g] = k[:, g * LANES:(g + 1) * LANES]
        vm_ref[g] = v[:, g * LANES:(g + 1) * LANES]


def _flash_init(m_ref, acc_ref):
    m_ref[...] = jnp.full(m_ref.shape, NEG, F32)
    acc_ref[...] = jnp.zeros(acc_ref.shape, F32)


def _flash_softmax_pv(n_heads, s_ref, vt_fn, m_ref, acc_ref):
    for hd in range(n_heads):
        s = s_ref[hd]
        m_prev = m_ref[hd]
        m_new = jnp.maximum(m_prev, jnp.max(s, axis=0, keepdims=True))
        p = jnp.exp2(s - m_new).astype(BF16)
        alpha = jnp.exp2(m_prev - m_new)
        m_ref[hd] = m_new
        acc_ref[hd] = alpha * acc_ref[hd] + jnp.dot(vt_fn(hd), p, preferred_element_type=F32)


def _flash_loop(i, diag_logits_fn, logits_fn, update_fn):
    def block(t):
        return jnp.minimum(t, i) - 1

    def consumed(t):
        return jnp.where(t == 0, i, t - 1)

    diag_logits_fn(0)
    n = i + 1
    n_quad = n // 4

    def quad(j, carry):
        t = 4 * j
        logits_fn(block(t + 1), 1)
        update_fn(consumed(t), 0)
        logits_fn(block(t + 2), 0)
        update_fn(t, 1)
        logits_fn(block(t + 3), 1)
        update_fn(t + 1, 0)
        logits_fn(block(t + 4), 0)
        update_fn(t + 2, 1)
        return carry

    lax.fori_loop(0, n_quad, quad, 0)
    t0 = 4 * n_quad
    rem = n - t0

    @pl.when(rem >= 2)
    def _():
        logits_fn(block(t0 + 1), 1)
        update_fn(consumed(t0), 0)
        logits_fn(block(t0 + 2), 0)
        update_fn(t0, 1)

    @pl.when(rem % 2 == 1)
    def _():
        update_fn(consumed(n - 1), 0)


def _flash_store(out_ref, acc_ref, n_heads):
    def head(hd):
        return acc_ref[hd, :HEAD_DIM, :] / acc_ref[hd, HEAD_DIM:HEAD_DIM + 1, :]

    for g in range(n_heads // 2):
        o = jnp.concatenate([head(2 * g), head(2 * g + 1)], axis=0)
        out_ref[:, g * LANES:(g + 1) * LANES] = o.T.astype(out_ref.dtype)


def _split_head_pairs(src_ref, dst_ref, n_pairs):
    low = lax.broadcasted_iota(jnp.int32, (src_ref.shape[0], LANES), 1) < HEAD_DIM
    for g in range(n_pairs):
        pair = src_ref[:, g * LANES:(g + 1) * LANES].astype(F32)
        dst_ref[2 * g] = jnp.where(low, pair, 0.0).astype(dst_ref.dtype)
        dst_ref[2 * g + 1] = jnp.where(low, 0.0, pair).astype(dst_ref.dtype)


def _fold8(x):
    return x.reshape(x.shape[0] // SUBLANES, SUBLANES, x.shape[1])


def _dsa_kernel(slopes_ref, qi_ref, wi_ref, ki_ref, qa_ref, ka_ref, vat_ref, ltri_ref,
                out_ref,
                score_ref, sb_ref, qim_ref, lo_ref, mn_ref, mx_ref, s_ref, m_ref, acc_ref,
                *, topk):
    i = pl.program_id(1)
    nkb = i + 1
    npair = (nkb + 1) // 2
    q0 = i * TQ

    _split_head_pairs(qi_ref, qim_ref, IDX_HEADS // 2)

    kidx = lax.broadcasted_iota(jnp.int32, (TK, TQ), 0)
    qidx = lax.broadcasted_iota(jnp.int32, (TK, TQ), 1)

    wt = wi_ref[...].astype(F32).T

    def score_block(kb, diagonal):
        k0 = pl.multiple_of(kb * TK, TK)
        kblk = ki_ref[pl.ds(k0, TK), :]
        acc = jnp.zeros((TK, TQ), F32)
        for hd in range(IDX_HEADS):
            acc = acc + jnp.maximum(_nt_dot(kblk, qim_ref[hd]), 0.0) * wt[hd:hd + 1, :]
        hi_part = lo_part = _fold8(acc)
        if diagonal:
            adm = (kidx >> 6) <= (qidx >> 6)
            acc = jnp.where(adm, acc, -jnp.inf)
            hi_part = _fold8(acc)
            lo_part = _fold8(jnp.where(adm, acc, jnp.inf))
        score_ref[kb] = acc
        sb_ref[kb] = acc.astype(BF16)
        mn_ref[...] = jnp.minimum(mn_ref[...], jnp.min(lo_part, axis=0))
        mx_ref[...] = jnp.maximum(mx_ref[...], jnp.max(hi_part, axis=0))

    mn_ref[...] = jnp.full((SUBLANES, TQ), jnp.inf, F32)
    mx_ref[...] = jnp.full((SUBLANES, TQ), -jnp.inf, F32)

    def score_quad(j, carry):
        for d in range(4):
            score_block(4 * j + d, False)
        return carry

    lax.fori_loop(0, i // 4, score_quad, 0)
    done = 4 * (i // 4)

    @pl.when(i - done >= 2)
    def _():
        score_block(done, False)
        score_block(done + 1, False)

    @pl.when(i % 2 == 1)
    def _():
        score_block(i - 1, False)

    score_block(i, True)

    @pl.when(nkb % 2 == 1)
    def _():
        score_ref[nkb] = jnp.full((TK, TQ), -jnp.inf, F32)
        sb_ref[nkb] = jnp.full((TK, TQ), -jnp.inf, BF16)

    qrow = q0 + lax.broadcasted_iota(jnp.int32, (1, TQ), 1)
    n_adm = ((qrow >> 6) + 1) << 6
    k_eff = jnp.minimum(n_adm, topk).astype(F32)

    lo = jnp.min(mn_ref[...], axis=0, keepdims=True)
    hi = jnp.max(mx_ref[...], axis=0, keepdims=True)

    def count_ge_coarse(t):
        tb = jnp.broadcast_to(t.astype(BF16), (PACK, TQ))
        one = jnp.ones((PACK, TQ), BF16)
        zero = jnp.zeros((PACK, TQ), BF16)

        def body(j, accs):
            accs = list(accs)
            for kb in (2 * j, 2 * j + 1):
                sb = sb_ref[kb]
                for r in range(TK // PACK):
                    hit = jnp.where(sb[r * PACK:(r + 1) * PACK, :] >= tb, one, zero)
                    accs[r % N_ACC] = accs[r % N_ACC] + hit
            return tuple(accs)

        accs = lax.fori_loop(0, npair, body, (zero,) * N_ACC)
        return jnp.sum(sum(a.astype(F32) for a in accs), axis=0, keepdims=True)

    def coarse(_, c):
        lo, hi = c
        mid = (0.5 * (lo + hi)).astype(BF16).astype(F32)
        ge = count_ge_coarse(mid) >= k_eff
        return jnp.where(ge, mid, lo), jnp.where(ge, hi, mid)

    lo, hi = lax.fori_loop(0, N_COARSE, coarse,
                           (lo.astype(BF16).astype(F32), hi.astype(BF16).astype(F32)))
    lo = lo - jnp.abs(lo) * 2.0 ** -8 - TINY
    hi = hi + jnp.abs(hi) * 2.0 ** -8 + TINY

    def count_ge(t):
        tb = jnp.broadcast_to(t, (SUBLANES, TQ))

        def body(j, accs):
            accs = list(accs)
            for kb in (2 * j, 2 * j + 1):
                s = score_ref[kb]
                for r in range(TK // SUBLANES):
                    hit = jnp.where(s[r * SUBLANES:(r + 1) * SUBLANES, :] >= tb, 1.0, 0.0)
                    accs[r % N_ACC] = accs[r % N_ACC] + hit
            return tuple(accs)

        z = jnp.zeros((SUBLANES, TQ), F32)
        accs = lax.fori_loop(0, npair, body, (z,) * N_ACC)
        return jnp.sum(sum(accs), axis=0, keepdims=True)

    def bisect(_, c):
        lo, hi, c_lo = c
        mid = 0.5 * (lo + hi)
        cnt = count_ge(mid)
        ge = cnt >= k_eff
        return jnp.where(ge, mid, lo), jnp.where(ge, hi, mid), jnp.where(ge, cnt, c_lo)

    c_lo = jnp.where(n_adm <= topk, k_eff, k_eff + 1.0)
    lo, hi, c_lo = lax.fori_loop(0, N_FINE, bisect, (lo, hi, c_lo))
    lo_ref[...] = lo

    for c in range(TQ // LANES):
        cs = slice(c * LANES, (c + 1) * LANES)

        @pl.when(jnp.max(c_lo[:, cs] - k_eff[:, cs]) > 0.0)
        def _():
            k_c = k_eff[:, cs]

            def counts(t):
                def body(j, c2):
                    ge, gt = c2
                    for kb in (2 * j, 2 * j + 1):
                        s = _fold8(score_ref[kb, :, cs])
                        ge = ge + jnp.sum(jnp.where(s >= t, 1.0, 0.0), axis=0)
                        gt = gt + jnp.sum(jnp.where(s > t, 1.0, 0.0), axis=0)
                    return ge, gt

                z = jnp.zeros((SUBLANES, LANES), F32)
                ge, gt = lax.fori_loop(0, npair, body, (z, z))
                return jnp.sum(ge, axis=0, keepdims=True), jnp.sum(gt, axis=0, keepdims=True)

            def largest(keep):
                def body(j, acc):
                    for kb in (2 * j, 2 * j + 1):
                        s = _fold8(score_ref[kb, :, cs])
                        acc = jnp.maximum(acc, jnp.max(jnp.where(keep(s), s, -jnp.inf), axis=0))
                    return acc

                acc = lax.fori_loop(0, npair, body, jnp.full((SUBLANES, LANES), -jnp.inf, F32))
                return jnp.max(acc, axis=0, keepdims=True)

            def pending(ge):
                return jnp.sum(jnp.where(ge >= k_c, 0.0, 1.0))

            def walk(c3):
                t, ge, _, _ = c3
                t = jnp.where(ge >= k_c, t, largest(lambda s: s < t))
                ge, gt = counts(t)
                return t, ge, gt, pending(ge)

            hi_c = hi[:, cs]
            t0 = largest(lambda s: s <= hi_c)
            ge0, gt0 = counts(t0)
            thr, _, c_gt, _ = lax.while_loop(lambda c3: c3[3] > 0.0, walk,
                                             (t0, ge0, gt0, pending(ge0)))
            need = k_c - c_gt

            def mark(j, seen):
                for kb in (2 * j, 2 * j + 1):
                    s = score_ref[kb, :, cs]
                    eqf = jnp.where(s == thr, 1.0, 0.0)
                    rank = seen + jnp.dot(ltri_ref[...], eqf.astype(BF16),
                                          preferred_element_type=F32)
                    tie = jnp.where(rank <= need, eqf, 0.0)
                    score_ref[kb, :, cs] = jnp.where(s > thr, 1.0, tie) * 2.0 - 1.0
                    seen = seen + jnp.sum(eqf, axis=0, keepdims=True)
                return seen

            lax.fori_loop(0, npair, mark, jnp.zeros((1, LANES), F32))
            lo_ref[:, cs] = jnp.zeros((1, LANES), F32)

    _flash_init(m_ref, acc_ref)
    lo_row = lo_ref[...]

    def logits(kb, buf, future=None):
        k0 = pl.multiple_of(kb * TK, TK)
        bias = jnp.where(score_ref[kb] >= lo_row, 0.0, NEG)
        for hd in range(A_HEADS):
            s = _nt_dot(ka_ref[hd, pl.ds(k0, TK), :], qa_ref[hd])
            s_ref[buf, hd] = s + (bias if future is None else bias - slopes_ref[hd] * future)

    def update(kb, buf):
        _flash_softmax_pv(A_HEADS, s_ref.at[buf],
                          lambda hd: vat_ref[kb, hd * V_ROWS:(hd + 1) * V_ROWS, :],
                          m_ref, acc_ref)

    _flash_loop(i, lambda buf: logits(i, buf, 2.0 * jnp.maximum(kidx - qidx, 0).astype(F32)),
                logits, update)
    _flash_store(out_ref, acc_ref, A_HEADS)


def _mla_kernel(qb_ref, kb_ref, vbt_ref, out_ref, s_ref, m_ref, acc_ref):
    i = pl.program_id(1)
    _flash_init(m_ref, acc_ref)

    def logits(kb, buf, bias=None):
        k0 = pl.multiple_of(kb * TK, TK)
        for hd in range(B_HEADS):
            s = _nt_dot(kb_ref[hd, pl.ds(k0, TK), :], qb_ref[hd])
            s_ref[buf, hd] = s if bias is None else s + bias

    def update(kb, buf):
        _flash_softmax_pv(B_HEADS, s_ref.at[buf],
                          lambda hd: vbt_ref[kb, hd * V_ROWS:(hd + 1) * V_ROWS, :],
                          m_ref, acc_ref)

    kidx = lax.broadcasted_iota(jnp.int32, (TK, TQ), 0)
    qidx = lax.broadcasted_iota(jnp.int32, (TK, TQ), 1)
    _flash_loop(i, lambda buf: logits(i, buf, jnp.where((kidx >> 6) <= (qidx >> 6), 0.0, NEG)),
                logits, update)
    _flash_store(out_ref, acc_ref, B_HEADS)


def _out_kernel(x_ref, ya_ref, yb_ref, z_ref, qm_ref, gmat_ref, gqm_ref, km_ref, vm_ref, wout_ref,
                out_ref):
    tm = x_ref.shape[0]
    low = lax.broadcasted_iota(jnp.int32, (tm, LANES), 1) < HEAD_DIM
    qm = _head64_norm(qm_ref[...].astype(F32), gmat_ref, gqm_ref[...])
    ym = []
    for g in range(M_HEADS // 2):
        pair = qm[:, g * LANES:(g + 1) * LANES]
        outs = []
        for q in (jnp.where(low, pair, 0.0), jnp.where(low, 0.0, pair)):
            s = _nt_dot(q.astype(BF16), km_ref[g])
            p = jnp.exp2(s - jnp.max(s, axis=1, keepdims=True))
            o = jnp.dot(p.astype(BF16), vm_ref[g], preferred_element_type=F32)
            outs.append(o / jnp.sum(p, axis=1, keepdims=True))
        ym.append(jnp.where(low, outs[0], outs[1]))
    y = jnp.concatenate([ya_ref[...].astype(F32), yb_ref[...].astype(F32)] + ym, axis=1)
    z = z_ref[...].astype(F32)
    y = (y * (z / (1.0 + jnp.exp(-z)))).astype(BF16)
    out_ref[...] = x_ref[...] + jnp.dot(y, wout_ref[...], preferred_element_type=F32)


def _full(shape):
    n = len(shape)
    return pl.BlockSpec(shape, lambda *_: (0,) * n)


def _params(*sem):
    return pltpu.CompilerParams(dimension_semantics=sem, vmem_limit_bytes=VMEM_LIMIT)


def _bf16_pieces(v, n):
    out, rest = [], v
    for _ in range(n):
        piece = rest.astype(BF16).astype(F32)
        out.append(piece)
        rest = rest - piece
    return jnp.stack(out, axis=-1)


def kernel(x, mem, g_in, w_in, w_uq, g_cq, w_ukv, g_ckv, g_qa, g_ka, g_qb, g_kb,
           g_mem, w_mk, w_mv, g_qm, g_km, w_out):
    B, S, D = x.shape
    n_mem = mem.shape[1]
    assert S % TM == 0 and TM % TK == 0 and TQ == TK and D == D_MIX and (S // TK) % 2 == 0
    assert S // PACK <= 256
    assert _C_QI == 0 and _C_KI % LANES == 0 and _C_WI % LANES == 0
    assert _C_QM % M_WIDTH == 0 and _C_GATE % D_MIX == 0
    topk = min(TOPK_MAX, S // 4)
    nkb = S // TK
    l = 0
    half = B_ROPE // 2

    wi = w_in[l]
    o = [0]
    for n in (A_WIDTH, A_WIDTH, A_WIDTH, IDX_HEADS * IDX_DIM, IDX_DIM, IDX_HEADS, A_WIDTH,
              Q_LORA, KV_LORA, B_ROPE, B_WIDTH, M_WIDTH, M_WIDTH):
        o.append(o[-1] + n)
    (c_qa, c_ka, c_va, c_qi, c_ki, c_wi, c_za, c_cq, c_ckv, c_kr, c_zb, c_qm, c_zm) = [
        wi[:, o[j]:o[j + 1]] for j in range(13)]
    zeros = lambda n: jnp.zeros((D, n), wi.dtype)
    w_pack = jnp.concatenate([
        c_qi,
        c_ki, c_ki,
        c_wi, zeros(LANES - IDX_HEADS),
        c_qm,
        c_za, c_zb, c_zm,
        c_qa, c_ka, c_va,
        c_cq, c_ckv,
        zeros(B_NOPE), c_kr, zeros(LANES - B_QK),
        zeros(B_NOPE), c_kr[:, half:], c_kr[:, :half], zeros(LANES - B_QK),
    ], axis=1).astype(BF16)
    assert w_pack.shape[1] == _C_END

    def rope_partner(v):
        z = jnp.zeros_like(v[..., :B_NOPE])
        return jnp.concatenate([z, v[..., B_NOPE + half:B_QK], v[..., B_NOPE:B_NOPE + half],
                                jnp.zeros_like(v[..., B_QK:])], axis=-1)

    wuq_h = jnp.pad(w_uq[l].reshape(Q_LORA, B_HEADS, B_QK), ((0, 0), (0, 0), (0, LANES - B_QK)))
    wuq_p = jnp.concatenate([wuq_h.reshape(Q_LORA, B_HEADS * LANES),
                             rope_partner(wuq_h).reshape(Q_LORA, B_HEADS * LANES)],
                            axis=1).astype(BF16)
    wukv = w_ukv[l].reshape(KV_LORA, B_HEADS, B_NOPE + B_V)
    wukv_k = jnp.pad(wukv[:, :, :B_NOPE], ((0, 0), (0, 0), (0, LANES - B_NOPE))
                     ).reshape(KV_LORA, B_HEADS * LANES).astype(BF16)
    wukv_v = wukv[:, :, B_NOPE:].reshape(KV_LORA, B_WIDTH).astype(BF16)

    idx = jnp.arange(A_WIDTH)
    gmat = jnp.where((idx[:, None] // HEAD_DIM) == (idx[None, :] // HEAD_DIM),
                     1.0 / HEAD_DIM, 0.0).astype(BF16)
    row2 = lambda v: v.reshape(1, -1).astype(F32)
    gqa_t = row2(jnp.tile(g_qa[l], A_HEADS) * (HEAD_DIM ** -0.5 * LOG2E))
    gka_t = row2(jnp.tile(g_ka[l], A_HEADS))
    gqm_t = row2(jnp.tile(g_qm[l], M_HEADS) * (HEAD_DIM ** -0.5 * LOG2E))
    gkm_t = row2(jnp.tile(g_km[l], M_HEADS))
    gqb_p = row2(jnp.pad(g_qb[l], (0, LANES - B_QK)) * (B_QK ** -0.5 * LOG2E))
    gkb_p = row2(jnp.pad(g_kb[l], (0, LANES - B_QK)))

    pos = jnp.arange(S, dtype=jnp.int32)
    inv = 1.0 / (ROPE_THETA ** (jnp.arange(half, dtype=F32) / half))
    ang = pos.astype(F32)[:, None] * inv[None, :]
    cos, sin = jnp.cos(ang), jnp.sin(ang)
    zs = lambda n: jnp.zeros((S, n), F32)
    rope_c = jnp.concatenate([jnp.ones((S, B_NOPE), F32), cos, cos, jnp.ones((S, LANES - B_QK), F32)], 1)
    rope_s = jnp.concatenate([zs(B_NOPE), -sin, sin, zs(LANES - B_QK)], 1)
    qa_tab, qb_tab = rope_c * gqb_p, rope_s * rope_partner(gqb_p)
    ka_tab, kb_tab = rope_c * gkb_p, rope_s * rope_partner(gkb_p)
    src = jnp.arange(LANES)
    place = (src[:, None] // HEAD_DIM * LANES + src[:, None] % HEAD_DIM
             == jnp.arange(2 * LANES)[None, :]).astype(BF16)

    slopes = 2.0 ** (-ALIBI_MAX * jnp.arange(1, A_HEADS + 1, dtype=F32) / A_HEADS) * LOG2E
    pieces = _bf16_pieces(slopes, N_ALIBI_PIECES)
    n_feat = 2 * N_ALIBI_PIECES
    qfeat = jnp.concatenate([jnp.zeros((A_HEADS, HEAD_DIM), F32), pieces * float(CHUNK), pieces,
                             jnp.zeros((A_HEADS, LANES - HEAD_DIM - n_feat), F32)], axis=1)
    pos_hi = jnp.repeat((pos // CHUNK).astype(F32)[:, None], N_ALIBI_PIECES, axis=1)
    pos_lo = jnp.repeat((pos % CHUNK).astype(F32)[:, None], N_ALIBI_PIECES, axis=1)
    kfeat = jnp.concatenate([zs(HEAD_DIM), pos_hi, pos_lo, zs(LANES - HEAD_DIM - n_feat)], axis=1)

    kk = jnp.arange(TK)
    ltri = (kk[None, :] <= kk[:, None]).astype(BF16)

    sds = jax.ShapeDtypeStruct
    tok = lambda w, j=0: pl.BlockSpec((None, TM, w), lambda b, t: (b, t, j))
    u = pl.pallas_call(
        _inproj_kernel,
        grid=(B, S // TM),
        in_specs=[tok(D), _full((1, D)), _full((D, _C_END))],
        out_specs=tok(_C_END),
        out_shape=sds((B, S, _C_END), BF16),
        compiler_params=_params("parallel", "parallel"),
        name="inproj",
    )(x, row2(g_in[l]), w_pack)

    grp = lambda n: pl.BlockSpec((None, n, TM, LANES), lambda b, t: (b, 0, t, 0))
    vt_out = lambda n: pl.BlockSpec((None, TM // TK, n * V_ROWS, TK), lambda b, t: (b, t, 0, 0))
    tab = pl.BlockSpec((TM, LANES), lambda b, t: (t, 0))
    qa, ka, vat, qb, kb, vbt = pl.pallas_call(
        _tokprep_kernel,
        grid=(B, S // TM),
        in_specs=[tok(_C_END), _full((A_WIDTH, A_WIDTH)), _full((LANES, 2 * LANES)),
                  _full((1, A_WIDTH)), _full((1, A_WIDTH)), _full((A_HEADS, LANES)), tab,
                  _full((1, Q_LORA)), _full((Q_LORA, 2 * B_HEADS * LANES)), _full((1, KV_LORA)),
                  _full((KV_LORA, B_HEADS * LANES)), _full((KV_LORA, B_WIDTH)),
                  tab, tab, tab, tab],
        out_specs=[grp(A_HEADS), grp(A_HEADS), vt_out(A_HEADS),
                   grp(B_HEADS), grp(B_HEADS), vt_out(B_HEADS)],
        out_shape=[sds((B, A_HEADS, S, LANES), BF16), sds((B, A_HEADS, S, LANES), BF16),
                   sds((B, nkb, A_HEADS * V_ROWS, TK), BF16),
                   sds((B, B_HEADS, S, LANES), BF16), sds((B, B_HEADS, S, LANES), BF16),
                   sds((B, nkb, B_HEADS * V_ROWS, TK), BF16)],
        compiler_params=_params("parallel", "parallel"),
        name="tokprep",
    )(u, gmat, place, gqa_t, gka_t, qfeat, kfeat, row2(g_cq[l]), wuq_p,
      row2(g_ckv[l]), wukv_k, wukv_v, qa_tab, qb_tab, ka_tab, kb_tab)

    km, vm = pl.pallas_call(
        _memkv_kernel,
        grid=(B,),
        in_specs=[pl.BlockSpec((None, n_mem, D), lambda b: (b, 0, 0)), _full((1, D)),
                  _full((D, M_WIDTH)), _full((D, M_WIDTH)), _full((A_WIDTH, A_WIDTH)),
                  _full((1, M_WIDTH))],
        out_specs=[pl.BlockSpec((None, 2, n_mem, LANES), lambda b: (b, 0, 0, 0))] * 2,
        out_shape=[sds((B, 2, n_mem, LANES), BF16)] * 2,
        compiler_params=_params("parallel"),
        name="memkv",
    )(mem, row2(g_mem[l]), w_mk[l].astype(BF16), w_mv[l].astype(BF16), gmat, gkm_t)

    qtile = lambda w, j=0: pl.BlockSpec((None, TQ, w), lambda b, t: (b, t, j))
    seq = lambda n: pl.BlockSpec((None, n, S, LANES), lambda b, t: (b, 0, 0, 0))
    vt_in = lambda n: pl.BlockSpec((None, nkb, n * V_ROWS, TK), lambda b, t: (b, 0, 0, 0))
    heads = lambda n: pl.BlockSpec((None, n, TQ, LANES), lambda b, t: (b, 0, t, 0))
    flash_scratch = lambda n: [pltpu.VMEM((2, n, TK, TQ), F32), pltpu.VMEM((n, 1, TQ), F32),
                               pltpu.VMEM((n, V_ROWS, TQ), F32)]
    ya = pl.pallas_call(
        functools.partial(_dsa_kernel, topk=topk),
        grid=(B, S // TQ),
        in_specs=[pl.BlockSpec(memory_space=pltpu.SMEM),
                  qtile(IDX_HEADS * IDX_DIM, _C_QI // (IDX_HEADS * IDX_DIM)),
                  qtile(LANES, _C_WI // LANES),
                  pl.BlockSpec((None, S, LANES), lambda b, t: (b, 0, _C_KI // LANES)),
                  heads(A_HEADS), seq(A_HEADS), vt_in(A_HEADS), _full((TK, TK))],
        out_specs=qtile(A_WIDTH),
        out_shape=sds((B, S, A_WIDTH), BF16),
        scratch_shapes=[pltpu.VMEM((nkb, TK, TQ), F32), pltpu.VMEM((nkb, TK, TQ), BF16),
                        pltpu.VMEM((IDX_HEADS, TQ, LANES), BF16),
                        pltpu.VMEM((1, TQ), F32), pltpu.VMEM((SUBLANES, TQ), F32),
                        pltpu.VMEM((SUBLANES, TQ), F32)] + flash_scratch(A_HEADS),
        compiler_params=_params("parallel", "arbitrary"),
        name="dsa",
    )(slopes, u, u, u, qa, ka, vat, ltri)

    yb = pl.pallas_call(
        _mla_kernel,
        grid=(B, S // TQ),
        in_specs=[heads(B_HEADS), seq(B_HEADS), vt_in(B_HEADS)],
        out_specs=qtile(B_WIDTH),
        out_shape=sds((B, S, B_WIDTH), BF16),
        scratch_shapes=flash_scratch(B_HEADS),
        compiler_params=_params("parallel", "arbitrary"),
        name="mla",
    )(qb, kb, vbt)

    memkv = pl.BlockSpec((None, 2, n_mem, LANES), lambda b, t: (b, 0, 0, 0))
    return pl.pallas_call(
        _out_kernel,
        grid=(B, S // TM),
        in_specs=[tok(D), tok(A_WIDTH), tok(B_WIDTH), tok(D_MIX, _C_GATE // D_MIX),
                  tok(M_WIDTH, _C_QM // M_WIDTH), _full((A_WIDTH, A_WIDTH)), _full((1, M_WIDTH)),
                  memkv, memkv, _full((D, D))],
        out_specs=tok(D),
        out_shape=sds((B, S, D), x.dtype),
        compiler_params=_params("parallel", "parallel"),
        name="outproj",
    )(x, ya, yb, u, u, gmat, gqm_t, km, vm, w_out[l].astype(BF16))
```

```python
import functools
import math

import jax
import jax.numpy as jnp
from jax import lax
from jax.experimental import pallas as pl
from jax.experimental.pallas import tpu as pltpu

F32 = jnp.float32
BF16 = jnp.bfloat16

CHUNK = 64
EPS = 1e-6
HEAD_DIM = 64
A_HEADS = 6
IDX_HEADS = 8
IDX_DIM = 64
TOPK_MAX = 256
ALIBI_MAX = 8.0
B_HEADS = 6
B_NOPE = 64
B_ROPE = 32
B_V = 64
B_QK = B_NOPE + B_ROPE
Q_LORA = 256
KV_LORA = 128
ROPE_THETA = 10000.0
M_HEADS = 4
A_WIDTH = A_HEADS * HEAD_DIM
B_WIDTH = B_HEADS * B_V
M_WIDTH = M_HEADS * HEAD_DIM
D_MIX = A_WIDTH + B_WIDTH + M_WIDTH
LOG2E = math.log2(math.e)
N_ALIBI_PIECES = 3

LANES = 128
SUBLANES = 8
VMEM_LIMIT = 52 * 1024 * 1024

TM = 1024
TN = 512
TQ = 256
TK = 256
N_COARSE = 8
N_FINE = 12
PACK = 2 * SUBLANES
N_ACC = 4
TINY = 1e-37
V_ROWS = 80
NEG = -1e30

_C_QI = 0
_C_KI = _C_QI + IDX_HEADS * IDX_DIM
_C_WI = _C_KI + LANES
_C_QM = _C_WI + LANES
_C_GATE = _C_QM + M_WIDTH
_C_QA = _C_GATE + D_MIX
_C_KA = _C_QA + A_WIDTH
_C_VA = _C_KA + A_WIDTH
_C_CQ = _C_VA + A_WIDTH
_C_CKV = _C_CQ + Q_LORA
_C_KR = _C_CKV + KV_LORA
_C_KRP = _C_KR + LANES
_C_END = _C_KRP + LANES


def _nt_dot(a, b):
    return lax.dot_general(a, b, (((1,), (1,)), ((), ())), preferred_element_type=F32)


def _head64_norm(u, gmat_ref, g):
    sq = (u * u).astype(BF16)
    parts = []
    for a in range(0, u.shape[1], 2 * LANES):
        b = min(a + 2 * LANES, u.shape[1])
        parts.append(jnp.dot(sq[:, a:b], gmat_ref[:b - a, :b - a], preferred_element_type=F32))
    msq = parts[0] if len(parts) == 1 else jnp.concatenate(parts, axis=1)
    return u * lax.rsqrt(msq + EPS) * g


def _inproj_kernel(x_ref, gin_ref, w_ref, u_ref):
    x = x_ref[...]
    ms = jnp.mean(x * x, axis=-1, keepdims=True)
    h = (x * lax.rsqrt(ms + EPS) * gin_ref[...]).astype(BF16)
    for a in range(0, _C_END, TN):
        b = min(a + TN, _C_END)
        u_ref[:, a:b] = jnp.dot(h, w_ref[:, a:b], preferred_element_type=F32).astype(BF16)


def _tokprep_kernel(u_ref, gmat_ref, place_ref, gqa_ref, gka_ref, qfeat_ref, kfeat_ref, gcq_ref,
                    wuq_ref, gckv_ref, wukvk_ref, wukvv_ref, qa_tab_ref, qb_tab_ref, ka_tab_ref,
                    kb_tab_ref, qa_ref, ka_ref, vat_ref, qb_ref, kb_ref, vbt_ref):
    def cols(a, b):
        return u_ref[:, a:b].astype(F32)

    ones_rows = (lax.broadcasted_iota(jnp.int32, (V_ROWS - HEAD_DIM, TK), 0) == 0).astype(BF16)

    def store_transposed(dst_ref, v):
        vt = v.T.astype(BF16)
        for j in range(TM // TK):
            for hd in range(v.shape[1] // HEAD_DIM):
                r = hd * V_ROWS
                dst_ref[j, r:r + HEAD_DIM, :] = vt[hd * HEAD_DIM:(hd + 1) * HEAD_DIM,
                                                   j * TK:(j + 1) * TK]
                dst_ref[j, r + HEAD_DIM:r + V_ROWS, :] = ones_rows

    def store_heads(dst_ref, v, feat_fn):
        vb = v.astype(BF16)
        for g in range(A_HEADS // 2):
            placed = jnp.dot(vb[:, g * LANES:(g + 1) * LANES], place_ref[...],
                             preferred_element_type=F32)
            for hd in (2 * g, 2 * g + 1):
                half = placed[:, (hd % 2) * LANES:(hd % 2 + 1) * LANES]
                dst_ref[hd] = (half + feat_fn(hd)).astype(BF16)

    store_heads(qa_ref, _head64_norm(cols(_C_QA, _C_KA), gmat_ref, gqa_ref[...]),
                lambda hd: qfeat_ref[hd:hd + 1, :])
    kfeat = kfeat_ref[...]
    store_heads(ka_ref, _head64_norm(cols(_C_KA, _C_VA), gmat_ref, gka_ref[...]), lambda hd: kfeat)
    store_transposed(vat_ref, cols(_C_VA, _C_CQ))

    def head96(uh, partner, tab_a, tab_b):
        ss = jnp.sum(uh * uh, axis=-1, keepdims=True) * (1.0 / B_QK)
        return ((uh * tab_a + partner * tab_b) * lax.rsqrt(ss + EPS)).astype(BF16)

    cq = cols(_C_CQ, _C_CKV)
    cq = cq * lax.rsqrt(jnp.mean(cq * cq, axis=-1, keepdims=True) + EPS) * gcq_ref[...]
    q = jnp.dot(cq.astype(BF16), wuq_ref[...], preferred_element_type=F32)
    qa_tab = qa_tab_ref[...]
    qb_tab = qb_tab_ref[...]
    for hd in range(B_HEADS):
        qb_ref[hd] = head96(q[:, hd * LANES:(hd + 1) * LANES],
                            q[:, (B_HEADS + hd) * LANES:(B_HEADS + hd + 1) * LANES], qa_tab, qb_tab)

    ckv = cols(_C_CKV, _C_KR)
    ckv = ckv * lax.rsqrt(jnp.mean(ckv * ckv, axis=-1, keepdims=True) + EPS) * gckv_ref[...]
    ckv = ckv.astype(BF16)
    kk = jnp.dot(ckv, wukvk_ref[...], preferred_element_type=F32)
    store_transposed(vbt_ref, jnp.dot(ckv, wukvv_ref[...], preferred_element_type=F32))
    krope = cols(_C_KR, _C_KRP)
    krope_partner = cols(_C_KRP, _C_END)
    ka_tab = ka_tab_ref[...]
    kb_tab = kb_tab_ref[...]
    for hd in range(B_HEADS):
        kb_ref[hd] = head96(kk[:, hd * LANES:(hd + 1) * LANES] + krope, krope_partner, ka_tab, kb_tab)


def _memkv_kernel(mem_ref, gmem_ref, wmk_ref, wmv_ref, gmat_ref, gkm_ref, km_ref, vm_ref):
    x = mem_ref[...]
    ms = jnp.mean(x * x, axis=-1, keepdims=True)
    m = (x * lax.rsqrt(ms + EPS) * gmem_ref[...]).astype(BF16)
    k = _head64_norm(jnp.dot(m, wmk_ref[...], preferred_element_type=F32), gmat_ref,
                     gkm_ref[...]).astype(BF16)
    v = jnp.dot(m, wmv_ref[...], preferred_element_type=F32).astype(BF16)
    for g in range(M_WIDTH // LANES):
        km_ref[g] = k[:, g * LANES:(g + 1) * LANES]
        vm_ref[g] = v[:, g * LANES:(g + 1) * LANES]


def _flash_init(m_ref, acc_ref):
    m_ref[...] = jnp.full(m_ref.shape, NEG, F32)
    acc_ref[...] = jnp.zeros(acc_ref.shape, F32)


def _flash_softmax_pv(n_heads, s_ref, vt_fn, m_ref, acc_ref):
    for hd in range(n_heads):
        s = s_ref[hd]
        m_prev = m_ref[hd]
        m_new = jnp.maximum(m_prev, jnp.max(s, axis=0, keepdims=True))
        p = jnp.exp2(s - m_new).astype(BF16)
        alpha = jnp.exp2(m_prev - m_new)
        m_ref[hd] = m_new
        acc_ref[hd] = alpha * acc_ref[hd] + jnp.dot(vt_fn(hd), p, preferred_element_type=F32)


def _flash_loop(i, diag_logits_fn, logits_fn, update_fn):
    def block(t):
        return jnp.minimum(t, i) - 1

    def consumed(t):
        return jnp.where(t == 0, i, t - 1)

    diag_logits_fn(0)
    n = i + 1
    n_quad = n // 4

    def quad(j, carry):
        t = 4 * j
        logits_fn(block(t + 1), 1)
        update_fn(consumed(t), 0)
        logits_fn(block(t + 2), 0)
        update_fn(t, 1)
        logits_fn(block(t + 3), 1)
        update_fn(t + 1, 0)
        logits_fn(block(t + 4), 0)
        update_fn(t + 2, 1)
        return carry

    lax.fori_loop(0, n_quad, quad, 0)
    t0 = 4 * n_quad
    rem = n - t0

    @pl.when(rem >= 2)
    def _():
        logits_fn(block(t0 + 1), 1)
        update_fn(consumed(t0), 0)
        logits_fn(block(t0 + 2), 0)
        update_fn(t0, 1)

    @pl.when(rem % 2 == 1)
    def _():
        update_fn(consumed(n - 1), 0)


def _flash_store(out_ref, acc_ref, n_heads):
    def head(hd):
        return acc_ref[hd, :HEAD_DIM, :] / acc_ref[hd, HEAD_DIM:HEAD_DIM + 1, :]

    for g in range(n_heads // 2):
        o = jnp.concatenate([head(2 * g), head(2 * g + 1)], axis=0)
        out_ref[:, g * LANES:(g + 1) * LANES] = o.T.astype(out_ref.dtype)


def _split_head_pairs(src_ref, dst_ref, n_pairs):
    low = lax.broadcasted_iota(jnp.int32, (src_ref.shape[0], LANES), 1) < HEAD_DIM
    for g in range(n_pairs):
        pair = src_ref[:, g * LANES:(g + 1) * LANES].astype(F32)
        dst_ref[2 * g] = jnp.where(low, pair, 0.0).astype(dst_ref.dtype)
        dst_ref[2 * g + 1] = jnp.where(low, 0.0, pair).astype(dst_ref.dtype)


def _fold8(x):
    return x.reshape(x.shape[0] // SUBLANES, SUBLANES, x.shape[1])


def _dsa_kernel(slopes_ref, qi_ref, wi_ref, ki_ref, qa_ref, ka_ref, vat_ref, ltri_ref,
                out_ref,
                score_ref, sb_ref, qim_ref, lo_ref, mn_ref, mx_ref, s_ref, m_ref, acc_ref,
                *, topk):
    i = pl.program_id(1)
    nkb = i + 1
    npair = (nkb + 1) // 2
    q0 = i * TQ

    _split_head_pairs(qi_ref, qim_ref, IDX_HEADS // 2)

    kidx = lax.broadcasted_iota(jnp.int32, (TK, TQ), 0)
    qidx = lax.broadcasted_iota(jnp.int32, (TK, TQ), 1)

    wt = wi_ref[...].astype(F32).T

    def score_block(kb, diagonal):
        k0 = pl.multiple_of(kb * TK, TK)
        kblk = ki_ref[pl.ds(k0, TK), :]
        acc = jnp.zeros((TK, TQ), F32)
        for hd in range(IDX_HEADS):
            acc = acc + jnp.maximum(_nt_dot(kblk, qim_ref[hd]), 0.0) * wt[hd:hd + 1, :]
        hi_part = lo_part = _fold8(acc)
        if diagonal:
            adm = (kidx >> 6) <= (qidx >> 6)
            acc = jnp.where(adm, acc, -jnp.inf)
            hi_part = _fold8(acc)
            lo_part = _fold8(jnp.where(adm, acc, jnp.inf))
        score_ref[kb] = acc
        sb_ref[kb] = acc.astype(BF16)
        mn_ref[...] = jnp.minimum(mn_ref[...], jnp.min(lo_part, axis=0))
        mx_ref[...] = jnp.maximum(mx_ref[...], jnp.max(hi_part, axis=0))

    mn_ref[...] = jnp.full((SUBLANES, TQ), jnp.inf, F32)
    mx_ref[...] = jnp.full((SUBLANES, TQ), -jnp.inf, F32)

    def score_quad(j, carry):
        for d in range(4):
            score_block(4 * j + d, False)
        return carry

    lax.fori_loop(0, i // 4, score_quad, 0)
    done = 4 * (i // 4)

    @pl.when(i - done >= 2)
    def _():
        score_block(done, False)
        score_block(done + 1, False)

    @pl.when(i % 2 == 1)
    def _():
        score_block(i - 1, False)

    score_block(i, True)

    @pl.when(nkb % 2 == 1)
    def _():
        score_ref[nkb] = jnp.full((TK, TQ), -jnp.inf, F32)
        sb_ref[nkb] = jnp.full((TK, TQ), -jnp.inf, BF16)

    qrow = q0 + lax.broadcasted_iota(jnp.int32, (1, TQ), 1)
    n_adm = ((qrow >> 6) + 1) << 6
    k_eff = jnp.minimum(n_adm, topk).astype(F32)

    lo = jnp.min(mn_ref[...], axis=0, keepdims=True)
    hi = jnp.max(mx_ref[...], axis=0, keepdims=True)

    def count_ge_coarse(t):
        tb = jnp.broadcast_to(t.astype(BF16), (PACK, TQ))
        one = jnp.ones((PACK, TQ), BF16)
        zero = jnp.zeros((PACK, TQ), BF16)

        def body(j, accs):
            accs = list(accs)
            for kb in (2 * j, 2 * j + 1):
                sb = sb_ref[kb]
                for r in range(TK // PACK):
                    hit = jnp.where(sb[r * PACK:(r + 1) * PACK, :] >= tb, one, zero)
                    accs[r % N_ACC] = accs[r % N_ACC] + hit
            return tuple(accs)

        accs = lax.fori_loop(0, npair, body, (zero,) * N_ACC)
        return jnp.sum(sum(a.astype(F32) for a in accs), axis=0, keepdims=True)

    def coarse(_, c):
        lo, hi = c
        mid = (0.5 * (lo + hi)).astype(BF16).astype(F32)
        ge = count_ge_coarse(mid) >= k_eff
        return jnp.where(ge, mid, lo), jnp.where(ge, hi, mid)

    lo, hi = lax.fori_loop(0, N_COARSE, coarse,
                           (lo.astype(BF16).astype(F32), hi.astype(BF16).astype(F32)))
    lo = lo - jnp.abs(lo) * 2.0 ** -8 - TINY
    hi = hi + jnp.abs(hi) * 2.0 ** -8 + TINY

    def count_ge(t):
        tb = jnp.broadcast_to(t, (SUBLANES, TQ))

        def body(j, accs):
            accs = list(accs)
            for kb in (2 * j, 2 * j + 1):
                s = score_ref[kb]
                for r in range(TK // SUBLANES):
                    hit = jnp.where(s[r * SUBLANES:(r + 1) * SUBLANES, :] >= tb, 1.0, 0.0)
                    accs[r % N_ACC] = accs[r % N_ACC] + hit
            return tuple(accs)

        z = jnp.zeros((SUBLANES, TQ), F32)
        accs = lax.fori_loop(0, npair, body, (z,) * N_ACC)
        return jnp.sum(sum(accs), axis=0, keepdims=True)

    def bisect(_, c):
        lo, hi, c_lo = c
        mid = 0.5 * (lo + hi)
        cnt = count_ge(mid)
        ge = cnt >= k_eff
        return jnp.where(ge, mid, lo), jnp.where(ge, hi, mid), jnp.where(ge, cnt, c_lo)

    c_lo = jnp.where(n_adm <= topk, k_eff, k_eff + 1.0)
    lo, hi, c_lo = lax.fori_loop(0, N_FINE, bisect, (lo, hi, c_lo))
    lo_ref[...] = lo

    for c in range(TQ // LANES):
        cs = slice(c * LANES, (c + 1) * LANES)

        @pl.when(jnp.max(c_lo[:, cs] - k_eff[:, cs]) > 0.0)
        def _():
            k_c = k_eff[:, cs]

            def counts(t):
                def body(j, c2):
                    ge, gt = c2
                    for kb in (2 * j, 2 * j + 1):
                        s = _fold8(score_ref[kb, :, cs])
                        ge = ge + jnp.sum(jnp.where(s >= t, 1.0, 0.0), axis=0)
                        gt = gt + jnp.sum(jnp.where(s > t, 1.0, 0.0), axis=0)
                    return ge, gt

                z = jnp.zeros((SUBLANES, LANES), F32)
                ge, gt = lax.fori_loop(0, npair, body, (z, z))
                return jnp.sum(ge, axis=0, keepdims=True), jnp.sum(gt, axis=0, keepdims=True)

            def largest(keep):
                def body(j, acc):
                    for kb in (2 * j, 2 * j + 1):
                        s = _fold8(score_ref[kb, :, cs])
                        acc = jnp.maximum(acc, jnp.max(jnp.where(keep(s), s, -jnp.inf), axis=0))
                    return acc

                acc = lax.fori_loop(0, npair, body, jnp.full((SUBLANES, LANES), -jnp.inf, F32))
                return jnp.max(acc, axis=0, keepdims=True)

            def pending(ge):
                return jnp.sum(jnp.where(ge >= k_c, 0.0, 1.0))

            def walk(c3):
                t, ge, _, _ = c3
                t = jnp.where(ge >= k_c, t, largest(lambda s: s < t))
                ge, gt = counts(t)
                return t, ge, gt, pending(ge)

            hi_c = hi[:, cs]
            t0 = largest(lambda s: s <= hi_c)
            ge0, gt0 = counts(t0)
            thr, _, c_gt, _ = lax.while_loop(lambda c3: c3[3] > 0.0, walk,
                                             (t0, ge0, gt0, pending(ge0)))
            need = k_c - c_gt

            def mark(j, seen):
                for kb in (2 * j, 2 * j + 1):
                    s = score_ref[kb, :, cs]
                    eqf = jnp.where(s == thr, 1.0, 0.0)
                    rank = seen + jnp.dot(ltri_ref[...], eqf.astype(BF16),
                                          preferred_element_type=F32)
                    tie = jnp.where(rank <= need, eqf, 0.0)
                    score_ref[kb, :, cs] = jnp.where(s > thr, 1.0, tie) * 2.0 - 1.0
                    seen = seen + jnp.sum(eqf, axis=0, keepdims=True)
                return seen

            lax.fori_loop(0, npair, mark, jnp.zeros((1, LANES), F32))
            lo_ref[:, cs] = jnp.zeros((1, LANES), F32)

    _flash_init(m_ref, acc_ref)
    lo_row = lo_ref[...]

    def logits(kb, buf, future=None):
        k0 = pl.multiple_of(kb * TK, TK)
        bias = jnp.where(score_ref[kb] >= lo_row, 0.0, NEG)
        for hd in range(A_HEADS):
            s = _nt_dot(ka_ref[hd, pl.ds(k0, TK), :], qa_ref[hd])
            s_ref[buf, hd] = s + (bias if future is None else bias - slopes_ref[hd] * future)

    def update(kb, buf):
        _flash_softmax_pv(A_HEADS, s_ref.at[buf],
                          lambda hd: vat_ref[kb, hd * V_ROWS:(hd + 1) * V_ROWS, :],
                          m_ref, acc_ref)

    _flash_loop(i, lambda buf: logits(i, buf, 2.0 * jnp.maximum(kidx - qidx, 0).astype(F32)),
                logits, update)
    _flash_store(out_ref, acc_ref, A_HEADS)


def _mla_kernel(qb_ref, kb_ref, vbt_ref, out_ref, s_ref, m_ref, acc_ref):
    i = pl.program_id(1)
    _flash_init(m_ref, acc_ref)

    def logits(kb, buf, bias=None):
        k0 = pl.multiple_of(kb * TK, TK)
        for hd in range(B_HEADS):
            s = _nt_dot(kb_ref[hd, pl.ds(k0, TK), :], qb_ref[hd])
            s_ref[buf, hd] = s if bias is None else s + bias

    def update(kb, buf):
        _flash_softmax_pv(B_HEADS, s_ref.at[buf],
                          lambda hd: vbt_ref[kb, hd * V_ROWS:(hd + 1) * V_ROWS, :],
                          m_ref, acc_ref)

    kidx = lax.broadcasted_iota(jnp.int32, (TK, TQ), 0)
    qidx = lax.broadcasted_iota(jnp.int32, (TK, TQ), 1)
    _flash_loop(i, lambda buf: logits(i, buf, jnp.where((kidx >> 6) <= (qidx >> 6), 0.0, NEG)),
                logits, update)
    _flash_store(out_ref, acc_ref, B_HEADS)


def _out_kernel(x_ref, ya_ref, yb_ref, z_ref, qm_ref, gmat_ref, gqm_ref, km_ref, vm_ref, wout_ref,
                out_ref):
    tm = x_ref.shape[0]
    low = lax.broadcasted_iota(jnp.int32, (tm, LANES), 1) < HEAD_DIM
    qm = _head64_norm(qm_ref[...].astype(F32), gmat_ref, gqm_ref[...])
    ym = []
    for g in range(M_HEADS // 2):
        pair = qm[:, g * LANES:(g + 1) * LANES]
        outs = []
        for q in (jnp.where(low, pair, 0.0), jnp.where(low, 0.0, pair)):
            s = _nt_dot(q.astype(BF16), km_ref[g])
            p = jnp.exp2(s - jnp.max(s, axis=1, keepdims=True))
            o = jnp.dot(p.astype(BF16), vm_ref[g], preferred_element_type=F32)
            outs.append(o / jnp.sum(p, axis=1, keepdims=True))
        ym.append(jnp.where(low, outs[0], outs[1]))
    y = jnp.concatenate([ya_ref[...].astype(F32), yb_ref[...].astype(F32)] + ym, axis=1)
    z = z_ref[...].astype(F32)
    y = (y * (z / (1.0 + jnp.exp(-z)))).astype(BF16)
    out_ref[...] = x_ref[...] + jnp.dot(y, wout_ref[...], preferred_element_type=F32)


def _full(shape):
    n = len(shape)
    return pl.BlockSpec(shape, lambda *_: (0,) * n)


def _params(*sem):
    return pltpu.CompilerParams(dimension_semantics=sem, vmem_limit_bytes=VMEM_LIMIT)


def _bf16_pieces(v, n):
    out, rest = [], v
    for _ in range(n):
        piece = rest.astype(BF16).astype(F32)
        out.append(piece)
        rest = rest - piece
    return jnp.stack(out, axis=-1)


def kernel(x, mem, g_in, w_in, w_uq, g_cq, w_ukv, g_ckv, g_qa, g_ka, g_qb, g_kb,
           g_mem, w_mk, w_mv, g_qm, g_km, w_out):
    B, S, D = x.shape
    n_mem = mem.shape[1]
    assert S % TM == 0 and TM % TK == 0 and TQ == TK and D == D_MIX and (S // TK) % 2 == 0
    assert S // PACK <= 256
    assert _C_QI == 0 and _C_KI % LANES == 0 and _C_WI % LANES == 0
    assert _C_QM % M_WIDTH == 0 and _C_GATE % D_MIX == 0
    topk = min(TOPK_MAX, S // 4)
    nkb = S // TK
    l = 0
    half = B_ROPE // 2

    wi = w_in[l]
    o = [0]
    for n in (A_WIDTH, A_WIDTH, A_WIDTH, IDX_HEADS * IDX_DIM, IDX_DIM, IDX_HEADS, A_WIDTH,
              Q_LORA, KV_LORA, B_ROPE, B_WIDTH, M_WIDTH, M_WIDTH):
        o.append(o[-1] + n)
    (c_qa, c_ka, c_va, c_qi, c_ki, c_wi, c_za, c_cq, c_ckv, c_kr, c_zb, c_qm, c_zm) = [
        wi[:, o[j]:o[j + 1]] for j in range(13)]
    zeros = lambda n: jnp.zeros((D, n), wi.dtype)
    w_pack = jnp.concatenate([
        c_qi,
        c_ki, c_ki,
        c_wi, zeros(LANES - IDX_HEADS),
        c_qm,
        c_za, c_zb, c_zm,
        c_qa, c_ka, c_va,
        c_cq, c_ckv,
        zeros(B_NOPE), c_kr, zeros(LANES - B_QK),
        zeros(B_NOPE), c_kr[:, half:], c_kr[:, :half], zeros(LANES - B_QK),
    ], axis=1).astype(BF16)
    assert w_pack.shape[1] == _C_END

    def rope_partner(v):
        z = jnp.zeros_like(v[..., :B_NOPE])
        return jnp.concatenate([z, v[..., B_NOPE + half:B_QK], v[..., B_NOPE:B_NOPE + half],
                                jnp.zeros_like(v[..., B_QK:])], axis=-1)

    wuq_h = jnp.pad(w_uq[l].reshape(Q_LORA, B_HEADS, B_QK), ((0, 0), (0, 0), (0, LANES - B_QK)))
    wuq_p = jnp.concatenate([wuq_h.reshape(Q_LORA, B_HEADS * LANES),
                             rope_partner(wuq_h).reshape(Q_LORA, B_HEADS * LANES)],
                            axis=1).astype(BF16)
    wukv = w_ukv[l].reshape(KV_LORA, B_HEADS, B_NOPE + B_V)
    wukv_k = jnp.pad(wukv[:, :, :B_NOPE], ((0, 0), (0, 0), (0, LANES - B_NOPE))
                     ).reshape(KV_LORA, B_HEADS * LANES).astype(BF16)
    wukv_v = wukv[:, :, B_NOPE:].reshape(KV_LORA, B_WIDTH).astype(BF16)

    idx = jnp.arange(A_WIDTH)
    gmat = jnp.where((idx[:, None] // HEAD_DIM) == (idx[None, :] // HEAD_DIM),
                     1.0 / HEAD_DIM, 0.0).astype(BF16)
    row2 = lambda v: v.reshape(1, -1).astype(F32)
    gqa_t = row2(jnp.tile(g_qa[l], A_HEADS) * (HEAD_DIM ** -0.5 * LOG2E))
    gka_t = row2(jnp.tile(g_ka[l], A_HEADS))
    gqm_t = row2(jnp.tile(g_qm[l], M_HEADS) * (HEAD_DIM ** -0.5 * LOG2E))
    gkm_t = row2(jnp.tile(g_km[l], M_HEADS))
    gqb_p = row2(jnp.pad(g_qb[l], (0, LANES - B_QK)) * (B_QK ** -0.5 * LOG2E))
    gkb_p = row2(jnp.pad(g_kb[l], (0, LANES - B_QK)))

    pos = jnp.arange(S, dtype=jnp.int32)
    inv = 1.0 / (ROPE_THETA ** (jnp.arange(half, dtype=F32) / half))
    ang = pos.astype(F32)[:, None] * inv[None, :]
    cos, sin = jnp.cos(ang), jnp.sin(ang)
    zs = lambda n: jnp.zeros((S, n), F32)
    rope_c = jnp.concatenate([jnp.ones((S, B_NOPE), F32), cos, cos, jnp.ones((S, LANES - B_QK), F32)], 1)
    rope_s = jnp.concatenate([zs(B_NOPE), -sin, sin, zs(LANES - B_QK)], 1)
    qa_tab, qb_tab = rope_c * gqb_p, rope_s * rope_partner(gqb_p)
    ka_tab, kb_tab = rope_c * gkb_p, rope_s * rope_partner(gkb_p)
    src = jnp.arange(LANES)
    place = (src[:, None] // HEAD_DIM * LANES + src[:, None] % HEAD_DIM
             == jnp.arange(2 * LANES)[None, :]).astype(BF16)

    slopes = 2.0 ** (-ALIBI_MAX * jnp.arange(1, A_HEADS + 1, dtype=F32) / A_HEADS) * LOG2E
    pieces = _bf16_pieces(slopes, N_ALIBI_PIECES)
    n_feat = 2 * N_ALIBI_PIECES
    qfeat = jnp.concatenate([jnp.zeros((A_HEADS, HEAD_DIM), F32), pieces * float(CHUNK), pieces,
                             jnp.zeros((A_HEADS, LANES - HEAD_DIM - n_feat), F32)], axis=1)
    pos_hi = jnp.repeat((pos // CHUNK).astype(F32)[:, None], N_ALIBI_PIECES, axis=1)
    pos_lo = jnp.repeat((pos % CHUNK).astype(F32)[:, None], N_ALIBI_PIECES, axis=1)
    kfeat = jnp.concatenate([zs(HEAD_DIM), pos_hi, pos_lo, zs(LANES - HEAD_DIM - n_feat)], axis=1)

    kk = jnp.arange(TK)
    ltri = (kk[None, :] <= kk[:, None]).astype(BF16)

    sds = jax.ShapeDtypeStruct
    tok = lambda w, j=0: pl.BlockSpec((None, TM, w), lambda b, t: (b, t, j))
    u = pl.pallas_call(
        _inproj_kernel,
        grid=(B, S // TM),
        in_specs=[tok(D), _full((1, D)), _full((D, _C_END))],
        out_specs=tok(_C_END),
        out_shape=sds((B, S, _C_END), BF16),
        compiler_params=_params("parallel", "parallel"),
        name="inproj",
    )(x, row2(g_in[l]), w_pack)

    grp = lambda n: pl.BlockSpec((None, n, TM, LANES), lambda b, t: (b, 0, t, 0))
    vt_out = lambda n: pl.BlockSpec((None, TM // TK, n * V_ROWS, TK), lambda b, t: (b, t, 0, 0))
    tab = pl.BlockSpec((TM, LANES), lambda b, t: (t, 0))
    qa, ka, vat, qb, kb, vbt = pl.pallas_call(
        _tokprep_kernel,
        grid=(B, S // TM),
        in_specs=[tok(_C_END), _full((A_WIDTH, A_WIDTH)), _full((LANES, 2 * LANES)),
                  _full((1, A_WIDTH)), _full((1, A_WIDTH)), _full((A_HEADS, LANES)), tab,
                  _full((1, Q_LORA)), _full((Q_LORA, 2 * B_HEADS * LANES)), _full((1, KV_LORA)),
                  _full((KV_LORA, B_HEADS * LANES)), _full((KV_LORA, B_WIDTH)),
                  tab, tab, tab, tab],
        out_specs=[grp(A_HEADS), grp(A_HEADS), vt_out(A_HEADS),
                   grp(B_HEADS), grp(B_HEADS), vt_out(B_HEADS)],
        out_shape=[sds((B, A_HEADS, S, LANES), BF16), sds((B, A_HEADS, S, LANES), BF16),
                   sds((B, nkb, A_HEADS * V_ROWS, TK), BF16),
                   sds((B, B_HEADS, S, LANES), BF16), sds((B, B_HEADS, S, LANES), BF16),
                   sds((B, nkb, B_HEADS * V_ROWS, TK), BF16)],
        compiler_params=_params("parallel", "parallel"),
        name="tokprep",
    )(u, gmat, place, gqa_t, gka_t, qfeat, kfeat, row2(g_cq[l]), wuq_p,
      row2(g_ckv[l]), wukv_k, wukv_v, qa_tab, qb_tab, ka_tab, kb_tab)

    km, vm = pl.pallas_call(
        _memkv_kernel,
        grid=(B,),
        in_specs=[pl.BlockSpec((None, n_mem, D), lambda b: (b, 0, 0)), _full((1, D)),
                  _full((D, M_WIDTH)), _full((D, M_WIDTH)), _full((A_WIDTH, A_WIDTH)),
                  _full((1, M_WIDTH))],
        out_specs=[pl.BlockSpec((None, 2, n_mem, LANES), lambda b: (b, 0, 0, 0))] * 2,
        out_shape=[sds((B, 2, n_mem, LANES), BF16)] * 2,
        compiler_params=_params("parallel"),
        name="memkv",
    )(mem, row2(g_mem[l]), w_mk[l].astype(BF16), w_mv[l].astype(BF16), gmat, gkm_t)

    qtile = lambda w, j=0: pl.BlockSpec((None, TQ, w), lambda b, t: (b, t, j))
    seq = lambda n: pl.BlockSpec((None, n, S, LANES), lambda b, t: (b, 0, 0, 0))
    vt_in = lambda n: pl.BlockSpec((None, nkb, n * V_ROWS, TK), lambda b, t: (b, 0, 0, 0))
    heads = lambda n: pl.BlockSpec((None, n, TQ, LANES), lambda b, t: (b, 0, t, 0))
    flash_scratch = lambda n: [pltpu.VMEM((2, n, TK, TQ), F32), pltpu.VMEM((n, 1, TQ), F32),
                               pltpu.VMEM((n, V_ROWS, TQ), F32)]
    ya = pl.pallas_call(
        functools.partial(_dsa_kernel, topk=topk),
        grid=(B, S // TQ),
        in_specs=[pl.BlockSpec(memory_space=pltpu.SMEM),
                  qtile(IDX_HEADS * IDX_DIM, _C_QI // (IDX_HEADS * IDX_DIM)),
                  qtile(LANES, _C_WI // LANES),
                  pl.BlockSpec((None, S, LANES), lambda b, t: (b, 0, _C_KI // LANES)),
                  heads(A_HEADS), seq(A_HEADS), vt_in(A_HEADS), _full((TK, TK))],
        out_specs=qtile(A_WIDTH),
        out_shape=sds((B, S, A_WIDTH), BF16),
        scratch_shapes=[pltpu.VMEM((nkb, TK, TQ), F32), pltpu.VMEM((nkb, TK, TQ), BF16),
                        pltpu.VMEM((IDX_HEADS, TQ, LANES), BF16),
                        pltpu.VMEM((1, TQ), F32), pltpu.VMEM((SUBLANES, TQ), F32),
                        pltpu.VMEM((SUBLANES, TQ), F32)] + flash_scratch(A_HEADS),
        compiler_params=_params("parallel", "arbitrary"),
        name="dsa",
    )(slopes, u, u, u, qa, ka, vat, ltri)

    yb = pl.pallas_call(
        _mla_kernel,
        grid=(B, S // TQ),
        in_specs=[heads(B_HEADS), seq(B_HEADS), vt_in(B_HEADS)],
        out_specs=qtile(B_WIDTH),
        out_shape=sds((B, S, B_WIDTH), BF16),
        scratch_shapes=flash_scratch(B_HEADS),
        compiler_params=_params("parallel", "arbitrary"),
        name="mla",
    )(qb, kb, vbt)

    memkv = pl.BlockSpec((None, 2, n_mem, LANES), lambda b, t: (b, 0, 0, 0))
    return pl.pallas_call(
        _out_kernel,
        grid=(B, S // TM),
        in_specs=[tok(D), tok(A_WIDTH), tok(B_WIDTH), tok(D_MIX, _C_GATE // D_MIX),
                  tok(M_WIDTH, _C_QM // M_WIDTH), _full((A_WIDTH, A_WIDTH)), _full((1, M_WIDTH)),
                  memkv, memkv, _full((D, D))],
        out_specs=tok(D),
        out_shape=sds((B, S, D), x.dtype),
        compiler_params=_params("parallel", "parallel"),
        name="outproj",
    )(x, ya, yb, u, u, gmat, gqm_t, km, vm, w_out[l].astype(BF16))
```

```python
import functools
import math

import jax
import jax.numpy as jnp
from jax import lax
from jax.experimental import pallas as pl
from jax.experimental.pallas import tpu as pltpu

F32 = jnp.float32
BF16 = jnp.bfloat16

CHUNK = 64
EPS = 1e-6
HEAD_DIM = 64
A_HEADS = 6
IDX_HEADS = 8
IDX_DIM = 64
TOPK_MAX = 256
ALIBI_MAX = 8.0
B_HEADS = 6
B_NOPE = 64
B_ROPE = 32
B_V = 64
B_QK = B_NOPE + B_ROPE
Q_LORA = 256
KV_LORA = 128
ROPE_THETA = 10000.0
M_HEADS = 4
A_WIDTH = A_HEADS * HEAD_DIM
B_WIDTH = B_HEADS * B_V
M_WIDTH = M_HEADS * HEAD_DIM
D_MIX = A_WIDTH + B_WIDTH + M_WIDTH
LOG2E = math.log2(math.e)
N_ALIBI_PIECES = 3

LANES = 128
SUBLANES = 8
VMEM_LIMIT = 52 * 1024 * 1024

TM = 1024
TN = 512
TQ = 256
TK = 256
N_COARSE = 8
N_FINE = 12
PACK = 2 * SUBLANES
N_ACC = 4
TINY = 1e-37
V_ROWS = 80
NEG = -1e30

_C_QI = 0
_C_KI = _C_QI + IDX_HEADS * IDX_DIM
_C_WI = _C_KI + LANES
_C_QM = _C_WI + LANES
_C_GATE = _C_QM + M_WIDTH
_C_QA = _C_GATE + D_MIX
_C_KA = _C_QA + A_WIDTH
_C_VA = _C_KA + A_WIDTH
_C_CQ = _C_VA + A_WIDTH
_C_CKV = _C_CQ + Q_LORA
_C_KR = _C_CKV + KV_LORA
_C_KRP = _C_KR + LANES
_C_END = _C_KRP + LANES


def _nt_dot(a, b):
    return lax.dot_general(a, b, (((1,), (1,)), ((), ())), preferred_element_type=F32)


def _head64_norm(u, gmat_ref, g):
    sq = (u * u).astype(BF16)
    parts = []
    for a in range(0, u.shape[1], 2 * LANES):
        b = min(a + 2 * LANES, u.shape[1])
        parts.append(jnp.dot(sq[:, a:b], gmat_ref[:b - a, :b - a], preferred_element_type=F32))
    msq = parts[0] if len(parts) == 1 else jnp.concatenate(parts, axis=1)
    return u * lax.rsqrt(msq + EPS) * g


def _inproj_kernel(x_ref, gin_ref, w_ref, u_ref):
    x = x_ref[...]
    ms = jnp.mean(x * x, axis=-1, keepdims=True)
    h = (x * lax.rsqrt(ms + EPS) * gin_ref[...]).astype(BF16)
    for a in range(0, _C_END, TN):
        b = min(a + TN, _C_END)
        u_ref[:, a:b] = jnp.dot(h, w_ref[:, a:b], preferred_element_type=F32).astype(BF16)


def _tokprep_kernel(u_ref, gmat_ref, place_ref, gqa_ref, gka_ref, qfeat_ref, kfeat_ref, gcq_ref,
                    wuq_ref, gckv_ref, wukvk_ref, wukvv_ref, qa_tab_ref, qb_tab_ref, ka_tab_ref,
                    kb_tab_ref, qa_ref, ka_ref, vat_ref, qb_ref, kb_ref, vbt_ref):
    def cols(a, b):
        return u_ref[:, a:b].astype(F32)

    ones_rows = (lax.broadcasted_iota(jnp.int32, (V_ROWS - HEAD_DIM, TK), 0) == 0).astype(BF16)

    def store_transposed(dst_ref, v):
        vt = v.T.astype(BF16)
        for j in range(TM // TK):
            for hd in range(v.shape[1] // HEAD_DIM):
                r = hd * V_ROWS
                dst_ref[j, r:r + HEAD_DIM, :] = vt[hd * HEAD_DIM:(hd + 1) * HEAD_DIM,
                                                   j * TK:(j + 1) * TK]
                dst_ref[j, r + HEAD_DIM:r + V_ROWS, :] = ones_rows

    def store_heads(dst_ref, v, feat_fn):
        vb = v.astype(BF16)
        for g in range(A_HEADS // 2):
            placed = jnp.dot(vb[:, g * LANES:(g + 1) * LANES], place_ref[...],
                             preferred_element_type=F32)
            for hd in (2 * g, 2 * g + 1):
                half = placed[:, (hd % 2) * LANES:(hd % 2 + 1) * LANES]
                dst_ref[hd] = (half + feat_fn(hd)).astype(BF16)

    store_heads(qa_ref, _head64_norm(cols(_C_QA, _C_KA), gmat_ref, gqa_ref[...]),
                lambda hd: qfeat_ref[hd:hd + 1, :])
    kfeat = kfeat_ref[...]
    store_heads(ka_ref, _head64_norm(cols(_C_KA, _C_VA), gmat_ref, gka_ref[...]), lambda hd: kfeat)
    store_transposed(vat_ref, cols(_C_VA, _C_CQ))

    def head96(uh, partner, tab_a, tab_b):
        ss = jnp.sum(uh * uh, axis=-1, keepdims=True) * (1.0 / B_QK)
        return ((uh * tab_a + partner * tab_b) * lax.rsqrt(ss + EPS)).astype(BF16)

    cq = cols(_C_CQ, _C_CKV)
    cq = cq * lax.rsqrt(jnp.mean(cq * cq, axis=-1, keepdims=True) + EPS) * gcq_ref[...]
    q = jnp.dot(cq.astype(BF16), wuq_ref[...], preferred_element_type=F32)
    qa_tab = qa_tab_ref[...]
    qb_tab = qb_tab_ref[...]
    for hd in range(B_HEADS):
        qb_ref[hd] = head96(q[:, hd * LANES:(hd + 1) * LANES],
                            q[:, (B_HEADS + hd) * LANES:(B_HEADS + hd + 1) * LANES], qa_tab, qb_tab)

    ckv = cols(_C_CKV, _C_KR)
    ckv = ckv * lax.rsqrt(jnp.mean(ckv * ckv, axis=-1, keepdims=True) + EPS) * gckv_ref[...]
    ckv = ckv.astype(BF16)
    kk = jnp.dot(ckv, wukvk_ref[...], preferred_element_type=F32)
    store_transposed(vbt_ref, jnp.dot(ckv, wukvv_ref[...], preferred_element_type=F32))
    krope = cols(_C_KR, _C_KRP)
    krope_partner = cols(_C_KRP, _C_END)
    ka_tab = ka_tab_ref[...]
    kb_tab = kb_tab_ref[...]
    for hd in range(B_HEADS):
        kb_ref[hd] = head96(kk[:, hd * LANES:(hd + 1) * LANES] + krope, krope_partner, ka_tab, kb_tab)


def _memkv_kernel(mem_ref, gmem_ref, wmk_ref, wmv_ref, gmat_ref, gkm_ref, km_ref, vm_ref):
    x = mem_ref[...]
    ms = jnp.mean(x * x, axis=-1, keepdims=True)
    m = (x * lax.rsqrt(ms + EPS) * gmem_ref[...]).astype(BF16)
    k = _head64_norm(jnp.dot(m, wmk_ref[...], preferred_element_type=F32), gmat_ref,
                     gkm_ref[...]).astype(BF16)
    v = jnp.dot(m, wmv_ref[...], preferred_element_type=F32).astype(BF16)
    for g in range(M_WIDTH // LANES):
        km_ref[g] = k[:, g * LANES:(g + 1) * LANES]
        vm_ref[g] = v[:, g * LANES:(g + 1) * LANES]


def _flash_init(m_ref, acc_ref):
    m_ref[...] = jnp.full(m_ref.shape, NEG, F32)
    acc_ref[...] = jnp.zeros(acc_ref.shape, F32)


def _flash_softmax_pv(n_heads, s_ref, vt_fn, m_ref, acc_ref):
    for hd in range(n_heads):
        s = s_ref[hd]
        m_prev = m_ref[hd]
        m_new = jnp.maximum(m_prev, jnp.max(s, axis=0, keepdims=True))
        p = jnp.exp2(s - m_new).astype(BF16)
        alpha = jnp.exp2(m_prev - m_new)
        m_ref[hd] = m_new
        acc_ref[hd] = alpha * acc_ref[hd] + jnp.dot(vt_fn(hd), p, preferred_element_type=F32)


def _flash_loop(i, diag_logits_fn, logits_fn, update_fn):
    def block(t):
        return jnp.minimum(t, i) - 1

    def consumed(t):
        return jnp.where(t == 0, i, t - 1)

    diag_logits_fn(0)
    n = i + 1

    def run(t, count):
        logits_fn(block(t + 1), 1)
        update_fn(consumed(t), 0)
        for d in range(1, count):
            logits_fn(block(t + d + 1), (d + 1) % 2)
            update_fn(t + d - 1, d % 2)

    def octet(j, carry):
        run(8 * j, 8)
        return carry

    lax.fori_loop(0, n // 8, octet, 0)
    t8 = 8 * (n // 8)

    @pl.when(n - t8 >= 4)
    def _():
        run(t8, 4)

    t0 = 4 * (n // 4)
    rem = n - t0

    @pl.when(rem >= 2)
    def _():
        logits_fn(block(t0 + 1), 1)
        update_fn(consumed(t0), 0)
        logits_fn(block(t0 + 2), 0)
        update_fn(t0, 1)

    @pl.when(rem % 2 == 1)
    def _():
        update_fn(consumed(n - 1), 0)


def _flash_store(out_ref, acc_ref, n_heads):
    def head(hd):
        return acc_ref[hd, :HEAD_DIM, :] / acc_ref[hd, HEAD_DIM:HEAD_DIM + 1, :]

    for g in range(n_heads // 2):
        o = jnp.concatenate([head(2 * g), head(2 * g + 1)], axis=0)
        out_ref[:, g * LANES:(g + 1) * LANES] = o.T.astype(out_ref.dtype)


def _split_head_pairs(src_ref, dst_ref, n_pairs):
    low = lax.broadcasted_iota(jnp.int32, (src_ref.shape[0], LANES), 1) < HEAD_DIM
    for g in range(n_pairs):
        pair = src_ref[:, g * LANES:(g + 1) * LANES].astype(F32)
        dst_ref[2 * g] = jnp.where(low, pair, 0.0).astype(dst_ref.dtype)
        dst_ref[2 * g + 1] = jnp.where(low, 0.0, pair).astype(dst_ref.dtype)


def _fold8(x):
    return x.reshape(x.shape[0] // SUBLANES, SUBLANES, x.shape[1])


def _dsa_kernel(slopes_ref, qi_ref, wi_ref, ki_ref, qa_ref, ka_ref, vat_ref, ltri_ref,
                out_ref,
                score_ref, sb_ref, qim_ref, lo_ref, mn_ref, mx_ref, s_ref, m_ref, acc_ref,
                *, topk):
    i = pl.program_id(1)
    nkb = i + 1
    npair = (nkb + 1) // 2
    q0 = i * TQ

    _split_head_pairs(qi_ref, qim_ref, IDX_HEADS // 2)

    kidx = lax.broadcasted_iota(jnp.int32, (TK, TQ), 0)
    qidx = lax.broadcasted_iota(jnp.int32, (TK, TQ), 1)

    wt = wi_ref[...].astype(F32).T

    def score_block(kb, diagonal):
        k0 = pl.multiple_of(kb * TK, TK)
        kblk = ki_ref[pl.ds(k0, TK), :]
        acc = jnp.zeros((TK, TQ), F32)
        for hd in range(IDX_HEADS):
            acc = acc + jnp.maximum(_nt_dot(kblk, qim_ref[hd]), 0.0) * wt[hd:hd + 1, :]
        hi_part = lo_part = _fold8(acc)
        if diagonal:
            adm = (kidx >> 6) <= (qidx >> 6)
            acc = jnp.where(adm, acc, -jnp.inf)
            hi_part = _fold8(acc)
            lo_part = _fold8(jnp.where(adm, acc, jnp.inf))
        score_ref[kb] = acc
        sb_ref[kb] = acc.astype(BF16)
        mn_ref[...] = jnp.minimum(mn_ref[...], jnp.min(lo_part, axis=0))
        mx_ref[...] = jnp.maximum(mx_ref[...], jnp.max(hi_part, axis=0))

    mn_ref[...] = jnp.full((SUBLANES, TQ), jnp.inf, F32)
    mx_ref[...] = jnp.full((SUBLANES, TQ), -jnp.inf, F32)

    def score_quad(j, carry):
        for d in range(4):
            score_block(4 * j + d, False)
        return carry

    lax.fori_loop(0, i // 4, score_quad, 0)
    done = 4 * (i // 4)

    @pl.when(i - done >= 2)
    def _():
        score_block(done, False)
        score_block(done + 1, False)

    @pl.when(i % 2 == 1)
    def _():
        score_block(i - 1, False)

    score_block(i, True)

    @pl.when(nkb % 2 == 1)
    def _():
        score_ref[nkb] = jnp.full((TK, TQ), -jnp.inf, F32)
        sb_ref[nkb] = jnp.full((TK, TQ), -jnp.inf, BF16)

    qrow = q0 + lax.broadcasted_iota(jnp.int32, (1, TQ), 1)
    n_adm = ((qrow >> 6) + 1) << 6
    k_eff = jnp.minimum(n_adm, topk).astype(F32)

    lo = jnp.min(mn_ref[...], axis=0, keepdims=True)
    hi = jnp.max(mx_ref[...], axis=0, keepdims=True)

    def count_ge_coarse(t):
        tb = jnp.broadcast_to(t.astype(BF16), (PACK, TQ))
        one = jnp.ones((PACK, TQ), BF16)
        zero = jnp.zeros((PACK, TQ), BF16)

        def body(j, accs):
            accs = list(accs)
            for kb in (2 * j, 2 * j + 1):
                sb = sb_ref[kb]
                for r in range(TK // PACK):
                    hit = jnp.where(sb[r * PACK:(r + 1) * PACK, :] >= tb, one, zero)
                    accs[r % N_ACC] = accs[r % N_ACC] + hit
            return tuple(accs)

        accs = lax.fori_loop(0, npair, body, (zero,) * N_ACC)
        return jnp.sum(sum(a.astype(F32) for a in accs), axis=0, keepdims=True)

    def coarse(_, c):
        lo, hi = c
        mid = (0.5 * (lo + hi)).astype(BF16).astype(F32)
        ge = count_ge_coarse(mid) >= k_eff
        return jnp.where(ge, mid, lo), jnp.where(ge, hi, mid)

    lo, hi = lax.fori_loop(0, N_COARSE, coarse,
                           (lo.astype(BF16).astype(F32), hi.astype(BF16).astype(F32)))
    lo = lo - jnp.abs(lo) * 2.0 ** -8 - TINY
    hi = hi + jnp.abs(hi) * 2.0 ** -8 + TINY

    def count_ge(t):
        tb = jnp.broadcast_to(t, (SUBLANES, TQ))

        def body(j, accs):
            accs = list(accs)
            for kb in (2 * j, 2 * j + 1):
                s = score_ref[kb]
                for r in range(TK // SUBLANES):
                    hit = jnp.where(s[r * SUBLANES:(r + 1) * SUBLANES, :] >= tb, 1.0, 0.0)
                    accs[r % N_ACC] = accs[r % N_ACC] + hit
            return tuple(accs)

        z = jnp.zeros((SUBLANES, TQ), F32)
        accs = lax.fori_loop(0, npair, body, (z,) * N_ACC)
        return jnp.sum(sum(accs), axis=0, keepdims=True)

    def bisect(_, c):
        lo, hi, c_lo = c
        mid = 0.5 * (lo + hi)
        cnt = count_ge(mid)
        ge = cnt >= k_eff
        return jnp.where(ge, mid, lo), jnp.where(ge, hi, mid), jnp.where(ge, cnt, c_lo)

    c_lo = jnp.where(n_adm <= topk, k_eff, k_eff + 1.0)
    lo, hi, c_lo = lax.fori_loop(0, N_FINE, bisect, (lo, hi, c_lo))
    lo_ref[...] = lo

    for c in range(TQ // LANES):
        cs = slice(c * LANES, (c + 1) * LANES)

        @pl.when(jnp.max(c_lo[:, cs] - k_eff[:, cs]) > 0.0)
        def _():
            k_c = k_eff[:, cs]

            def counts(t):
                def body(j, c2):
                    ge, gt = c2
                    for kb in (2 * j, 2 * j + 1):
                        s = _fold8(score_ref[kb, :, cs])
                        ge = ge + jnp.sum(jnp.where(s >= t, 1.0, 0.0), axis=0)
                        gt = gt + jnp.sum(jnp.where(s > t, 1.0, 0.0), axis=0)
                    return ge, gt

                z = jnp.zeros((SUBLANES, LANES), F32)
                ge, gt = lax.fori_loop(0, npair, body, (z, z))
                return jnp.sum(ge, axis=0, keepdims=True), jnp.sum(gt, axis=0, keepdims=True)

            def largest(keep):
                def body(j, acc):
                    for kb in (2 * j, 2 * j + 1):
                        s = _fold8(score_ref[kb, :, cs])
                        acc = jnp.maximum(acc, jnp.max(jnp.where(keep(s), s, -jnp.inf), axis=0))
                    return acc

                acc = lax.fori_loop(0, npair, body, jnp.full((SUBLANES, LANES), -jnp.inf, F32))
                return jnp.max(acc, axis=0, keepdims=True)

            def pending(ge):
                return jnp.sum(jnp.where(ge >= k_c, 0.0, 1.0))

            def walk(c3):
                t, ge, _, _ = c3
                t = jnp.where(ge >= k_c, t, largest(lambda s: s < t))
                ge, gt = counts(t)
                return t, ge, gt, pending(ge)

            hi_c = hi[:, cs]
            t0 = largest(lambda s: s <= hi_c)
            ge0, gt0 = counts(t0)
            thr, _, c_gt, _ = lax.while_loop(lambda c3: c3[3] > 0.0, walk,
                                             (t0, ge0, gt0, pending(ge0)))
            need = k_c - c_gt

            def mark(j, seen):
                for kb in (2 * j, 2 * j + 1):
                    s = score_ref[kb, :, cs]
                    eqf = jnp.where(s == thr, 1.0, 0.0)
                    rank = seen + jnp.dot(ltri_ref[...], eqf.astype(BF16),
                                          preferred_element_type=F32)
                    tie = jnp.where(rank <= need, eqf, 0.0)
                    score_ref[kb, :, cs] = jnp.where(s > thr, 1.0, tie) * 2.0 - 1.0
                    seen = seen + jnp.sum(eqf, axis=0, keepdims=True)
                return seen

            lax.fori_loop(0, npair, mark, jnp.zeros((1, LANES), F32))
            lo_ref[:, cs] = jnp.zeros((1, LANES), F32)

    _flash_init(m_ref, acc_ref)
    lo_row = lo_ref[...]

    def logits(kb, buf, future=None):
        k0 = pl.multiple_of(kb * TK, TK)
        bias = jnp.where(score_ref[kb] >= lo_row, 0.0, NEG)
        for hd in range(A_HEADS):
            s = _nt_dot(ka_ref[hd, pl.ds(k0, TK), :], qa_ref[hd])
            s_ref[buf, hd] = s + (bias if future is None else bias - slopes_ref[hd] * future)

    def update(kb, buf):
        _flash_softmax_pv(A_HEADS, s_ref.at[buf],
                          lambda hd: vat_ref[kb, hd * V_ROWS:(hd + 1) * V_ROWS, :],
                          m_ref, acc_ref)

    _flash_loop(i, lambda buf: logits(i, buf, 2.0 * jnp.maximum(kidx - qidx, 0).astype(F32)),
                logits, update)
    _flash_store(out_ref, acc_ref, A_HEADS)


def _mla_kernel(qb_ref, kb_ref, vbt_ref, out_ref, s_ref, m_ref, acc_ref):
    i = pl.program_id(1)
    _flash_init(m_ref, acc_ref)

    def logits(kb, buf, bias=None):
        k0 = pl.multiple_of(kb * TK, TK)
        for hd in range(B_HEADS):
            s = _nt_dot(kb_ref[hd, pl.ds(k0, TK), :], qb_ref[hd])
            s_ref[buf, hd] = s if bias is None else s + bias

    def update(kb, buf):
        _flash_softmax_pv(B_HEADS, s_ref.at[buf],
                          lambda hd: vbt_ref[kb, hd * V_ROWS:(hd + 1) * V_ROWS, :],
                          m_ref, acc_ref)

    kidx = lax.broadcasted_iota(jnp.int32, (TK, TQ), 0)
    qidx = lax.broadcasted_iota(jnp.int32, (TK, TQ), 1)
    _flash_loop(i, lambda buf: logits(i, buf, jnp.where((kidx >> 6) <= (qidx >> 6), 0.0, NEG)),
                logits, update)
    _flash_store(out_ref, acc_ref, B_HEADS)


def _out_kernel(x_ref, ya_ref, yb_ref, z_ref, qm_ref, gmat_ref, gqm_ref, km_ref, vm_ref, wout_ref,
                out_ref):
    tm = x_ref.shape[0]
    low = lax.broadcasted_iota(jnp.int32, (tm, LANES), 1) < HEAD_DIM
    qm = _head64_norm(qm_ref[...].astype(F32), gmat_ref, gqm_ref[...])
    ym = []
    for g in range(M_HEADS // 2):
        pair = qm[:, g * LANES:(g + 1) * LANES]
        outs = []
        for q in (jnp.where(low, pair, 0.0), jnp.where(low, 0.0, pair)):
            s = _nt_dot(q.astype(BF16), km_ref[g])
            p = jnp.exp2(s - jnp.max(s, axis=1, keepdims=True))
            o = jnp.dot(p.astype(BF16), vm_ref[g], preferred_element_type=F32)
            outs.append(o / jnp.sum(p, axis=1, keepdims=True))
        ym.append(jnp.where(low, outs[0], outs[1]))
    y = jnp.concatenate([ya_ref[...].astype(F32), yb_ref[...].astype(F32)] + ym, axis=1)
    z = z_ref[...].astype(F32)
    y = (y * (z / (1.0 + jnp.exp(-z)))).astype(BF16)
    out_ref[...] = x_ref[...] + jnp.dot(y, wout_ref[...], preferred_element_type=F32)


def _full(shape):
    n = len(shape)
    return pl.BlockSpec(shape, lambda *_: (0,) * n)


def _params(*sem):
    return pltpu.CompilerParams(dimension_semantics=sem, vmem_limit_bytes=VMEM_LIMIT)


def _bf16_pieces(v, n):
    out, rest = [], v
    for _ in range(n):
        piece = rest.astype(BF16).astype(F32)
        out.append(piece)
        rest = rest - piece
    return jnp.stack(out, axis=-1)


def kernel(x, mem, g_in, w_in, w_uq, g_cq, w_ukv, g_ckv, g_qa, g_ka, g_qb, g_kb,
           g_mem, w_mk, w_mv, g_qm, g_km, w_out):
    B, S, D = x.shape
    n_mem = mem.shape[1]
    assert S % TM == 0 and TM % TK == 0 and TQ == TK and D == D_MIX and (S // TK) % 2 == 0
    assert S // PACK <= 256
    assert _C_QI == 0 and _C_KI % LANES == 0 and _C_WI % LANES == 0
    assert _C_QM % M_WIDTH == 0 and _C_GATE % D_MIX == 0
    topk = min(TOPK_MAX, S // 4)
    nkb = S // TK
    l = 0
    half = B_ROPE // 2

    wi = w_in[l]
    o = [0]
    for n in (A_WIDTH, A_WIDTH, A_WIDTH, IDX_HEADS * IDX_DIM, IDX_DIM, IDX_HEADS, A_WIDTH,
              Q_LORA, KV_LORA, B_ROPE, B_WIDTH, M_WIDTH, M_WIDTH):
        o.append(o[-1] + n)
    (c_qa, c_ka, c_va, c_qi, c_ki, c_wi, c_za, c_cq, c_ckv, c_kr, c_zb, c_qm, c_zm) = [
        wi[:, o[j]:o[j + 1]] for j in range(13)]
    zeros = lambda n: jnp.zeros((D, n), wi.dtype)
    w_pack = jnp.concatenate([
        c_qi,
        c_ki, c_ki,
        c_wi, zeros(LANES - IDX_HEADS),
        c_qm,
        c_za, c_zb, c_zm,
        c_qa, c_ka, c_va,
        c_cq, c_ckv,
        zeros(B_NOPE), c_kr, zeros(LANES - B_QK),
        zeros(B_NOPE), c_kr[:, half:], c_kr[:, :half], zeros(LANES - B_QK),
    ], axis=1).astype(BF16)
    assert w_pack.shape[1] == _C_END

    def rope_partner(v):
        z = jnp.zeros_like(v[..., :B_NOPE])
        return jnp.concatenate([z, v[..., B_NOPE + half:B_QK], v[..., B_NOPE:B_NOPE + half],
                                jnp.zeros_like(v[..., B_QK:])], axis=-1)

    wuq_h = jnp.pad(w_uq[l].reshape(Q_LORA, B_HEADS, B_QK), ((0, 0), (0, 0), (0, LANES - B_QK)))
    wuq_p = jnp.concatenate([wuq_h.reshape(Q_LORA, B_HEADS * LANES),
                             rope_partner(wuq_h).reshape(Q_LORA, B_HEADS * LANES)],
                            axis=1).astype(BF16)
    wukv = w_ukv[l].reshape(KV_LORA, B_HEADS, B_NOPE + B_V)
    wukv_k = jnp.pad(wukv[:, :, :B_NOPE], ((0, 0), (0, 0), (0, LANES - B_NOPE))
                     ).reshape(KV_LORA, B_HEADS * LANES).astype(BF16)
    wukv_v = wukv[:, :, B_NOPE:].reshape(KV_LORA, B_WIDTH).astype(BF16)

    idx = jnp.arange(A_WIDTH)
    gmat = jnp.where((idx[:, None] // HEAD_DIM) == (idx[None, :] // HEAD_DIM),
                     1.0 / HEAD_DIM, 0.0).astype(BF16)
    row2 = lambda v: v.reshape(1, -1).astype(F32)
    gqa_t = row2(jnp.tile(g_qa[l], A_HEADS) * (HEAD_DIM ** -0.5 * LOG2E))
    gka_t = row2(jnp.tile(g_ka[l], A_HEADS))
    gqm_t = row2(jnp.tile(g_qm[l], M_HEADS) * (HEAD_DIM ** -0.5 * LOG2E))
    gkm_t = row2(jnp.tile(g_km[l], M_HEADS))
    gqb_p = row2(jnp.pad(g_qb[l], (0, LANES - B_QK)) * (B_QK ** -0.5 * LOG2E))
    gkb_p = row2(jnp.pad(g_kb[l], (0, LANES - B_QK)))

    pos = jnp.arange(S, dtype=jnp.int32)
    inv = 1.0 / (ROPE_THETA ** (jnp.arange(half, dtype=F32) / half))
    ang = pos.astype(F32)[:, None] * inv[None, :]
    cos, sin = jnp.cos(ang), jnp.sin(ang)
    zs = lambda n: jnp.zeros((S, n), F32)
    rope_c = jnp.concatenate([jnp.ones((S, B_NOPE), F32), cos, cos, jnp.ones((S, LANES - B_QK), F32)], 1)
    rope_s = jnp.concatenate([zs(B_NOPE), -sin, sin, zs(LANES - B_QK)], 1)
    qa_tab, qb_tab = rope_c * gqb_p, rope_s * rope_partner(gqb_p)
    ka_tab, kb_tab = rope_c * gkb_p, rope_s * rope_partner(gkb_p)
    src = jnp.arange(LANES)
    place = (src[:, None] // HEAD_DIM * LANES + src[:, None] % HEAD_DIM
             == jnp.arange(2 * LANES)[None, :]).astype(BF16)

    slopes = 2.0 ** (-ALIBI_MAX * jnp.arange(1, A_HEADS + 1, dtype=F32) / A_HEADS) * LOG2E
    pieces = _bf16_pieces(slopes, N_ALIBI_PIECES)
    n_feat = 2 * N_ALIBI_PIECES
    qfeat = jnp.concatenate([jnp.zeros((A_HEADS, HEAD_DIM), F32), pieces * float(CHUNK), pieces,
                             jnp.zeros((A_HEADS, LANES - HEAD_DIM - n_feat), F32)], axis=1)
    pos_hi = jnp.repeat((pos // CHUNK).astype(F32)[:, None], N_ALIBI_PIECES, axis=1)
    pos_lo = jnp.repeat((pos % CHUNK).astype(F32)[:, None], N_ALIBI_PIECES, axis=1)
    kfeat = jnp.concatenate([zs(HEAD_DIM), pos_hi, pos_lo, zs(LANES - HEAD_DIM - n_feat)], axis=1)

    kk = jnp.arange(TK)
    ltri = (kk[None, :] <= kk[:, None]).astype(BF16)

    sds = jax.ShapeDtypeStruct
    tok = lambda w, j=0: pl.BlockSpec((None, TM, w), lambda b, t: (b, t, j))
    u = pl.pallas_call(
        _inproj_kernel,
        grid=(B, S // TM),
        in_specs=[tok(D), _full((1, D)), _full((D, _C_END))],
        out_specs=tok(_C_END),
        out_shape=sds((B, S, _C_END), BF16),
        compiler_params=_params("parallel", "parallel"),
        name="inproj",
    )(x, row2(g_in[l]), w_pack)

    grp = lambda n: pl.BlockSpec((None, n, TM, LANES), lambda b, t: (b, 0, t, 0))
    vt_out = lambda n: pl.BlockSpec((None, TM // TK, n * V_ROWS, TK), lambda b, t: (b, t, 0, 0))
    tab = pl.BlockSpec((TM, LANES), lambda b, t: (t, 0))
    qa, ka, vat, qb, kb, vbt = pl.pallas_call(
        _tokprep_kernel,
        grid=(B, S // TM),
        in_specs=[tok(_C_END), _full((A_WIDTH, A_WIDTH)), _full((LANES, 2 * LANES)),
                  _full((1, A_WIDTH)), _full((1, A_WIDTH)), _full((A_HEADS, LANES)), tab,
                  _full((1, Q_LORA)), _full((Q_LORA, 2 * B_HEADS * LANES)), _full((1, KV_LORA)),
                  _full((KV_LORA, B_HEADS * LANES)), _full((KV_LORA, B_WIDTH)),
                  tab, tab, tab, tab],
        out_specs=[grp(A_HEADS), grp(A_HEADS), vt_out(A_HEADS),
                   grp(B_HEADS), grp(B_HEADS), vt_out(B_HEADS)],
        out_shape=[sds((B, A_HEADS, S, LANES), BF16), sds((B, A_HEADS, S, LANES), BF16),
                   sds((B, nkb, A_HEADS * V_ROWS, TK), BF16),
                   sds((B, B_HEADS, S, LANES), BF16), sds((B, B_HEADS, S, LANES), BF16),
                   sds((B, nkb, B_HEADS * V_ROWS, TK), BF16)],
        compiler_params=_params("parallel", "parallel"),
        name="tokprep",
    )(u, gmat, place, gqa_t, gka_t, qfeat, kfeat, row2(g_cq[l]), wuq_p,
      row2(g_ckv[l]), wukv_k, wukv_v, qa_tab, qb_tab, ka_tab, kb_tab)

    km, vm = pl.pallas_call(
        _memkv_kernel,
        grid=(B,),
        in_specs=[pl.BlockSpec((None, n_mem, D), lambda b: (b, 0, 0)), _full((1, D)),
                  _full((D, M_WIDTH)), _full((D, M_WIDTH)), _full((A_WIDTH, A_WIDTH)),
                  _full((1, M_WIDTH))],
        out_specs=[pl.BlockSpec((None, 2, n_mem, LANES), lambda b: (b, 0, 0, 0))] * 2,
        out_shape=[sds((B, 2, n_mem, LANES), BF16)] * 2,
        compiler_params=_params("parallel"),
        name="memkv",
    )(mem, row2(g_mem[l]), w_mk[l].astype(BF16), w_mv[l].astype(BF16), gmat, gkm_t)

    qtile = lambda w, j=0: pl.BlockSpec((None, TQ, w), lambda b, t: (b, t, j))
    seq = lambda n: pl.BlockSpec((None, n, S, LANES), lambda b, t: (b, 0, 0, 0))
    vt_in = lambda n: pl.BlockSpec((None, nkb, n * V_ROWS, TK), lambda b, t: (b, 0, 0, 0))
    heads = lambda n: pl.BlockSpec((None, n, TQ, LANES), lambda b, t: (b, 0, t, 0))
    flash_scratch = lambda n: [pltpu.VMEM((2, n, TK, TQ), F32), pltpu.VMEM((n, 1, TQ), F32),
                               pltpu.VMEM((n, V_ROWS, TQ), F32)]
    ya = pl.pallas_call(
        functools.partial(_dsa_kernel, topk=topk),
        grid=(B, S // TQ),
        in_specs=[pl.BlockSpec(memory_space=pltpu.SMEM),
                  qtile(IDX_HEADS * IDX_DIM, _C_QI // (IDX_HEADS * IDX_DIM)),
                  qtile(LANES, _C_WI // LANES),
                  pl.BlockSpec((None, S, LANES), lambda b, t: (b, 0, _C_KI // LANES)),
                  heads(A_HEADS), seq(A_HEADS), vt_in(A_HEADS), _full((TK, TK))],
        out_specs=qtile(A_WIDTH),
        out_shape=sds((B, S, A_WIDTH), BF16),
        scratch_shapes=[pltpu.VMEM((nkb, TK, TQ), F32), pltpu.VMEM((nkb, TK, TQ), BF16),
                        pltpu.VMEM((IDX_HEADS, TQ, LANES), BF16),
                        pltpu.VMEM((1, TQ), F32), pltpu.VMEM((SUBLANES, TQ), F32),
                        pltpu.VMEM((SUBLANES, TQ), F32)] + flash_scratch(A_HEADS),
        compiler_params=_params("parallel", "arbitrary"),
        name="dsa",
    )(slopes, u, u, u, qa, ka, vat, ltri)

    yb = pl.pallas_call(
        _mla_kernel,
        grid=(B, S // TQ),
        in_specs=[heads(B_HEADS), seq(B_HEADS), vt_in(B_HEADS)],
        out_specs=qtile(B_WIDTH),
        out_shape=sds((B, S, B_WIDTH), BF16),
        scratch_shapes=flash_scratch(B_HEADS),
        compiler_params=_params("parallel", "arbitrary"),
        name="mla",
    )(qb, kb, vbt)

    memkv = pl.BlockSpec((None, 2, n_mem, LANES), lambda b, t: (b, 0, 0, 0))
    return pl.pallas_call(
        _out_kernel,
        grid=(B, S // TM),
        in_specs=[tok(D), tok(A_WIDTH), tok(B_WIDTH), tok(D_MIX, _C_GATE // D_MIX),
                  tok(M_WIDTH, _C_QM // M_WIDTH), _full((A_WIDTH, A_WIDTH)), _full((1, M_WIDTH)),
                  memkv, memkv, _full((D, D))],
        out_specs=tok(D),
        out_shape=sds((B, S, D), x.dtype),
        compiler_params=_params("parallel", "parallel"),
        name="outproj",
    )(x, ya, yb, u, u, gmat, gqm_t, km, vm, w_out[l].astype(BF16))
```
